```python
import jax, jax.numpy as jnp
from jax import lax
import numpy as np

D_MODEL = 1024
BATCH = 2
SEQ = 16384
DEPTH = 4
DEC_BATCH = 16
DEC_SEQ = 4096
PAST_LEN = 128

SSD_HEADS = 24
SSD_HEAD_DIM = 64
SSD_INNER = SSD_HEADS * SSD_HEAD_DIM
SSD_GROUPS = 4
SSD_STATE = 128
SSD_CONV = 5
SSD_CHUNK = 128
SSD_CONV_DIM = SSD_INNER + 2 * SSD_GROUPS * SSD_STATE
SC_WIDTH = 512
SC_CONV = 3
N_BRANCH = 2
N_EXPERTS = 16
EXPERT_FF = 2048
CAPACITY_FACTOR = 2
EPS = 1e-6

OFF_Z = 0
OFF_XBC = OFF_Z + SSD_INNER
OFF_DT = OFF_XBC + SSD_CONV_DIM
OFF_SCB = OFF_DT + 2 * SSD_HEADS
OFF_SCC = OFF_SCB + SC_WIDTH
OFF_SCX = OFF_SCC + SC_WIDTH
OFF_GATE = OFF_SCX + SC_WIDTH
N_IN = OFF_GATE + N_BRANCH * D_MODEL

kernel_name = "hybrid_ssd_shortconv_ec_encoder"


def rms_norm(x, w):
    xf = x.astype(jnp.float32)
    y = xf * lax.rsqrt(jnp.mean(xf * xf, axis=-1, keepdims=True) + EPS)
    return (y * w.astype(jnp.float32)).astype(x.dtype)


def centred_dwconv(u, w):
    k = w.shape[0]
    pad = k // 2
    L = u.shape[1]
    up = jnp.pad(u, ((0, 0), (pad, pad), (0, 0)))
    out = up[:, 0:L] * w[0]
    for i in range(1, k):
        out = out + up[:, i:i + L] * w[i]
    return out


def segsum(a):
    T = a.shape[-1]
    cs = jnp.cumsum(a, axis=-1)
    diff = cs[..., :, None] - cs[..., None, :]
    mask = jnp.tril(jnp.ones((T, T), dtype=bool))
    return jnp.where(mask, diff, -jnp.inf)


def ssd_scan(x, dt, a, b, c):
    Bsz, L, H, P = x.shape
    G, N = b.shape[-2], b.shape[-1]
    R = H // G
    Q = SSD_CHUNK
    nc = L // Q
    f32 = jnp.float32
    xd = (x.astype(f32) * dt[..., None]).reshape(Bsz, nc, Q, G, R, P)
    da = jnp.moveaxis((dt * a).reshape(Bsz, nc, Q, G, R), 2, -1)
    cs = jnp.cumsum(da, axis=-1)
    bq = b.astype(f32).reshape(Bsz, nc, Q, G, N)
    cq = c.astype(f32).reshape(Bsz, nc, Q, G, N)
    lmat = jnp.exp(segsum(da))
    cb = jnp.einsum("bclgn,bcsgn->bcgls", cq, bq)
    wts = cb[:, :, :, None] * lmat
    y_diag = jnp.einsum("bcgrls,bcsgrp->bclgrp", wts, xd)
    decay_to_end = jnp.moveaxis(jnp.exp(cs[..., -1:] - cs), -1, 2)
    states = jnp.einsum("bclgn,bclgrp->bcgrpn", bq, xd * decay_to_end[..., None])
    chunk_decay = jnp.exp(cs[..., -1])

    def step(carry, inp):
        st, dec = inp
        return carry * dec[..., None, None] + st, carry

    init = jnp.zeros((Bsz, G, R, P, N), f32)
    _, prev = lax.scan(step, init, (jnp.moveaxis(states, 1, 0), jnp.moveaxis(chunk_decay, 1, 0)))
    prev = jnp.moveaxis(prev, 0, 1)
    decay_from_start = jnp.moveaxis(jnp.exp(cs), -1, 2)
    y_off = jnp.einsum("bclgn,bcgrpn->bclgrp", cq, prev) * decay_from_start[..., None]
    return (y_diag + y_off).reshape(Bsz, L, H, P)


def ssd_branch(z, xbc, dt_raw, conv_w, conv_b, dt_bias, a_log, d_skip, norm_w, w_out):
    Bsz, L, _ = z.shape
    xbc = jax.nn.silu(centred_dwconv(xbc, conv_w) + conv_b)
    xs = xbc[..., :SSD_INNER].reshape(Bsz, L, SSD_HEADS, SSD_HEAD_DIM)
    bs = xbc[..., SSD_INNER:SSD_INNER + SSD_GROUPS * SSD_STATE].reshape(Bsz, L, SSD_GROUPS, SSD_STATE)
    cs = xbc[..., SSD_INNER + SSD_GROUPS * SSD_STATE:].reshape(Bsz, L, SSD_GROUPS, SSD_STATE)
    a = -jnp.exp(a_log.astype(jnp.float32))
    dt = jax.nn.softplus(dt_raw.astype(jnp.float32).reshape(Bsz, L, 2, SSD_HEADS)
                         + dt_bias.astype(jnp.float32))
    flip = lambda t: jnp.flip(t, axis=1)
    y_fwd = ssd_scan(xs, dt[:, :, 0], a[0], bs, cs)
    y_bwd = flip(ssd_scan(flip(xs), flip(dt[:, :, 1]), a[1], flip(bs), flip(cs)))
    y = y_fwd + y_bwd + xs.astype(jnp.float32) * d_skip.astype(jnp.float32)[:, None]
    y = y.reshape(Bsz, L, SSD_INNER) * jax.nn.silu(z.astype(jnp.float32))
    yg = y.reshape(Bsz, L, SSD_GROUPS, SSD_INNER // SSD_GROUPS)
    yg = yg * lax.rsqrt(jnp.mean(yg * yg, axis=-1, keepdims=True) + EPS)
    y = yg.reshape(Bsz, L, SSD_INNER) * norm_w.astype(jnp.float32)
    return y.astype(z.dtype) @ w_out


def short_conv_branch(gb, gc, u, conv_w, w_out):
    v = centred_dwconv(gc * u, conv_w)
    return (gb * v) @ w_out


def expert_choice_ffn(h, router_w, w_gate, w_up, w_down):
    Bsz, L, D = h.shape
    T = Bsz * L
    cap = CAPACITY_FACTOR * T // N_EXPERTS
    hf = h.reshape(T, D)
    probs = jax.nn.softmax((hf @ router_w).astype(jnp.float32), axis=-1)
    gate, idx = lax.top_k(probs.T, cap)
    xe = hf[idx]
    act = jax.nn.silu(jnp.einsum("ecd,edf->ecf", xe, w_gate)) * jnp.einsum("ecd,edf->ecf", xe, w_up)
    ye = jnp.einsum("ecf,efd->ecd", act, w_down) * gate[..., None].astype(hf.dtype)
    out = jnp.zeros_like(hf).at[idx.reshape(-1)].add(ye.reshape(-1, D))
    return out.reshape(Bsz, L, D)


def trunk(x, norm1_w, w_in, ssd_conv_w, ssd_conv_b, ssd_dt_bias, ssd_a_log, ssd_d,
          ssd_norm_w, w_ssd_out, sc_conv_w, w_sc_out, gate_b, w_o, norm2_w,
          router_w, w_gate, w_up, w_down, final_norm_w):
    Bsz, L, D = x.shape
    for l in range(DEPTH):
        h = rms_norm(x, norm1_w[l])
        proj = h @ w_in[l]
        o_ssd = ssd_branch(proj[..., OFF_Z:OFF_XBC], proj[..., OFF_XBC:OFF_DT],
                           proj[..., OFF_DT:OFF_SCB], ssd_conv_w[l], ssd_conv_b[l],
                           ssd_dt_bias[l], ssd_a_log[l], ssd_d[l], ssd_norm_w[l], w_ssd_out[l])
        o_sc = short_conv_branch(proj[..., OFF_SCB:OFF_SCC], proj[..., OFF_SCC:OFF_SCX],
                                 proj[..., OFF_SCX:OFF_GATE], sc_conv_w[l], w_sc_out[l])
        g = jax.nn.sigmoid(proj[..., OFF_GATE:] + gate_b[l]).reshape(Bsz, L, N_BRANCH, D)
        mix = g[:, :, 0] * o_ssd + g[:, :, 1] * o_sc
        x = x + mix @ w_o[l]
        x = x + expert_choice_ffn(rms_norm(x, norm2_w[l]), router_w[l], w_gate[l], w_up[l], w_down[l])
    return rms_norm(x, final_norm_w)


def setup_inputs(seed: int = 0) -> dict:
    key = jax.random.key(seed)
    ks = jax.random.split(key, 24)
    f32 = jnp.float32
    nrm = lambda k, shape, s: jax.random.normal(k, shape, f32) * s
    dt0 = jnp.exp(jax.random.uniform(ks[6], (DEPTH, 2, SSD_HEADS), f32,
                                     np.float32(np.log(1e-3)), np.float32(np.log(1e-1))))
    dt_bias = dt0 + jnp.log(-jnp.expm1(-dt0))
    a_log = jnp.log(jax.random.uniform(ks[7], (DEPTH, 2, SSD_HEADS), f32, 1.0, 16.0))
    return {
        "x_prompt": jax.random.normal(ks[0], (BATCH, SEQ, D_MODEL), f32),
        "x_sample": jax.random.normal(ks[1], (DEC_BATCH, DEC_SEQ, D_MODEL), f32),
        "norm1_w": 1.0 + nrm(ks[2], (DEPTH, D_MODEL), 0.02),
        "w_in": nrm(ks[3], (DEPTH, D_MODEL, N_IN), D_MODEL ** -0.5),
        "ssd_conv_w": nrm(ks[4], (DEPTH, SSD_CONV, SSD_CONV_DIM), SSD_CONV ** -0.5),
        "ssd_conv_b": nrm(ks[5], (DEPTH, SSD_CONV_DIM), 0.01),
        "ssd_dt_bias": dt_bias,
        "ssd_a_log": a_log,
        "ssd_d": 1.0 + nrm(ks[8], (DEPTH, SSD_HEADS), 0.1),
        "ssd_norm_w": 1.0 + nrm(ks[9], (DEPTH, SSD_INNER), 0.02),
        "w_ssd_out": nrm(ks[10], (DEPTH, SSD_INNER, D_MODEL), SSD_INNER ** -0.5),
        "sc_conv_w": nrm(ks[11], (DEPTH, SC_CONV, SC_WIDTH), SC_CONV ** -0.5),
        "w_sc_out": nrm(ks[12], (DEPTH, SC_WIDTH, D_MODEL), SC_WIDTH ** -0.5),
        "gate_b": nrm(ks[13], (DEPTH, N_BRANCH * D_MODEL), 0.01),
        "w_o": nrm(ks[14], (DEPTH, D_MODEL, D_MODEL), D_MODEL ** -0.5),
        "norm2_w": 1.0 + nrm(ks[15], (DEPTH, D_MODEL), 0.02),
        "router_w": nrm(ks[16], (DEPTH, D_MODEL, N_EXPERTS), D_MODEL ** -0.5),
        "w_gate": nrm(ks[17], (DEPTH, N_EXPERTS, D_MODEL, EXPERT_FF), D_MODEL ** -0.5),
        "w_up": nrm(ks[18], (DEPTH, N_EXPERTS, D_MODEL, EXPERT_FF), D_MODEL ** -0.5),
        "w_down": nrm(ks[19], (DEPTH, N_EXPERTS, EXPERT_FF, D_MODEL), EXPERT_FF ** -0.5),
        "final_norm_w": 1.0 + nrm(ks[20], (D_MODEL,), 0.02),
    }


def reference(x_prompt, x_sample, norm1_w, w_in, ssd_conv_w, ssd_conv_b, ssd_dt_bias,
              ssd_a_log, ssd_d, ssd_norm_w, w_ssd_out, sc_conv_w, w_sc_out, gate_b, w_o,
              norm2_w, router_w, w_gate, w_up, w_down, final_norm_w):
    y_prompt = trunk(x_prompt, norm1_w, w_in, ssd_conv_w, ssd_conv_b, ssd_dt_bias, ssd_a_log,
                     ssd_d, ssd_norm_w, w_ssd_out, sc_conv_w, w_sc_out, gate_b, w_o, norm2_w,
                     router_w, w_gate, w_up, w_down, final_norm_w)
    y_sample = trunk(x_sample, norm1_w, w_in, ssd_conv_w, ssd_conv_b, ssd_dt_bias, ssd_a_log,
                     ssd_d, ssd_norm_w, w_ssd_out, sc_conv_w, w_sc_out, gate_b, w_o, norm2_w,
                     router_w, w_gate, w_up, w_down, final_norm_w)
    return (y_prompt, y_sample)
```

```python
import functools

import numpy as np
import jax
import jax.numpy as jnp
from jax import lax
from jax.experimental import pallas as pl
from jax.experimental.pallas import tpu as pltpu

F32 = jnp.float32
BF16 = jnp.bfloat16
I32 = jnp.int32

D_MODEL = 1024
HEADS = 24
HEAD_DIM = 64
INNER = HEADS * HEAD_DIM
GROUPS = 4
GROUP_W = INNER // GROUPS
STATE = 128
CHUNK = 128
SSD_CONV = 5
SC_WIDTH = 512
SC_CONV = 3
N_EXPERTS = 16
EXPERT_FF = 2048
CAPACITY_FACTOR = 2
EPS = 1e-6
LANES = 128
SUBLANES = 8

_OFF_XBC = INNER
_OFF_DT = _OFF_XBC + INNER + 2 * GROUPS * STATE
_OFF_SCB = _OFF_DT + 2 * HEADS
_OFF_GATE = _OFF_SCB + 3 * SC_WIDTH
_N_IN = _OFF_GATE + 2 * D_MODEL
COL_X = INNER
COL_B = 2 * INNER
COL_C = COL_B + GROUPS * STATE
COL_GATE = COL_C + GROUPS * STATE
COL_SCB = COL_GATE + 2 * D_MODEL
N_MAIN = COL_SCB + 3 * SC_WIDTH
CB = 512

VMEM_LIMIT = 50 * 1024 * 1024


def _cparams(sem):
    return pltpu.CompilerParams(dimension_semantics=sem, vmem_limit_bytes=VMEM_LIMIT)


def _sigmoid(v):
    return 1.0 / (1.0 + jnp.exp(-v))


def _split3(v):
    hi = v.astype(BF16)
    r = v - hi.astype(F32)
    mid = r.astype(BF16)
    lo = (r - mid.astype(F32)).astype(BF16)
    return hi, mid, lo


def _dot(a, b):
    return jnp.dot(a, b, preferred_element_type=F32)


def _dot_nt(a, b):
    return lax.dot_general(a, b, (((1,), (1,)), ((), ())), preferred_element_type=F32)


def _dot_tn(a, b):
    return lax.dot_general(a, b, (((0,), (0,)), ((), ())), preferred_element_type=F32)


def _dot3_l(a_f32, b_exact):
    hi, mid, lo = _split3(a_f32)
    return _dot(hi, b_exact) + _dot(mid, b_exact) + _dot(lo, b_exact)


def _dot3_r(a_exact, b_f32):
    hi, mid, lo = _split3(b_f32)
    return _dot(a_exact, hi) + _dot(a_exact, mid) + _dot(a_exact, lo)


class _Stream:
    def __init__(self, groups, tm, tq, tk, ts):
        self.groups = tuple(groups)
        self.tm, self.tq, self.tk, self.ts = tm, tq, tk, ts
        self.t_g = [n * l for n, l in self.groups]
        self.off_g = [int(v) for v in np.cumsum([0] + self.t_g[:-1])]
        self.total = int(sum(self.t_g))
        self.cap_g = [CAPACITY_FACTOR * t // N_EXPERTS for t in self.t_g]
        self.j_g = [c // ts for c in self.cap_g]
        self.nkb_g = [t // tk for t in self.t_g]
        self.tile_off_g = [int(v) for v in np.cumsum([0] + [N_EXPERTS * j for j in self.j_g[:-1]])]
        self.n_tiles = int(sum(N_EXPERTS * j for j in self.j_g))
        for (n, l), t, c in zip(self.groups, self.t_g, self.cap_g):
            assert l % CHUNK == 0 and l % tq == 0 and t % tk == 0 and c % ts == 0 and tk % LANES == 0
        assert self.total % tm == 0 and self.total % tq == 0

    def seq_flags(self, tile):
        n_tiles = self.total // tile
        first = np.zeros((n_tiles,), np.int32)
        last = np.zeros((n_tiles,), np.int32)
        for (n, l), off in zip(self.groups, self.off_g):
            for s in range(n):
                first[(off + s * l) // tile] = 1
                last[(off + (s + 1) * l) // tile - 1] = 1
        return jnp.asarray(first), jnp.asarray(last)


def _inproj_body(x_ref, nw_ref, w_ref, wdt_ref, proj_ref, dt_ref, h_scr):
    @pl.when(pl.program_id(1) == 0)
    def _():
        x = x_ref[...]
        ms = jnp.mean(x * x, axis=-1, keepdims=True)
        h = (x * lax.rsqrt(ms + EPS) * nw_ref[...]).astype(BF16)
        h_scr[...] = h
        dt_ref[...] = _dot(h, wdt_ref[...])

    proj_ref[...] = _dot(h_scr[...], w_ref[...]).astype(BF16)


def _inproj(x, nw, w_main, w_dt, layer, st):
    t, tm = st.total, st.tm
    tn = INNER
    return pl.pallas_call(
        _inproj_body,
        grid=(t // tm, N_MAIN // tn),
        in_specs=[
            pl.BlockSpec((tm, D_MODEL), lambda i, n: (i, 0)),
            pl.BlockSpec((None, 1, D_MODEL), lambda i, n: (layer, 0, 0)),
            pl.BlockSpec((None, D_MODEL, tn), lambda i, n: (layer, 0, n)),
            pl.BlockSpec((None, D_MODEL, LANES), lambda i, n: (layer, 0, 0)),
        ],
        out_specs=[
            pl.BlockSpec((tm, tn), lambda i, n: (i, n)),
            pl.BlockSpec((tm, LANES), lambda i, n: (i, 0)),
        ],
        out_shape=[jax.ShapeDtypeStruct((t, N_MAIN), BF16), jax.ShapeDtypeStruct((t, LANES), F32)],
        scratch_shapes=[pltpu.VMEM((tm, D_MODEL), BF16)],
        compiler_params=_cparams(("arbitrary", "arbitrary")),
        name="inproj",
    )(x, nw, w_main, w_dt)


def _conv_body(topz, botz, main_ref, prev_ref, next_ref, w_ref, b_ref, o_ref, ext):
    i = pl.program_id(0)
    tq = main_ref.shape[0]
    halo = SUBLANES
    pad = SSD_CONV // 2
    ext[0:halo, :] = jnp.where(topz[i] == 1, 0.0, prev_ref[...].astype(F32))
    ext[halo:halo + tq, :] = main_ref[...].astype(F32)
    ext[halo + tq:2 * halo + tq, :] = jnp.where(botz[i] == 1, 0.0, next_ref[...].astype(F32))
    acc = b_ref[...] + ext[halo - pad:halo - pad + tq, :] * w_ref[0:1, :]
    for k in range(1, SSD_CONV):
        acc = acc + ext[halo - pad + k:halo - pad + k + tq, :] * w_ref[k:k + 1, :]
    o_ref[...] = (acc * _sigmoid(acc)).astype(BF16)


def _conv_xbc(proj, conv_w, conv_b, layer, st, flags):
    t, tq = st.total, st.tq
    ncb = (COL_GATE - COL_X) // CB
    cb0 = COL_X // CB
    rpt = tq // SUBLANES
    last_rb = t // SUBLANES - 1
    grid_spec = pltpu.PrefetchScalarGridSpec(
        num_scalar_prefetch=2,
        grid=(t // tq, ncb),
        in_specs=[
            pl.BlockSpec((tq, CB), lambda i, j, a, b: (i, cb0 + j)),
            pl.BlockSpec((SUBLANES, CB), lambda i, j, a, b: (jnp.maximum(i * rpt - 1, 0), cb0 + j)),
            pl.BlockSpec((SUBLANES, CB), lambda i, j, a, b: (jnp.minimum((i + 1) * rpt, last_rb), cb0 + j)),
            pl.BlockSpec((None, SSD_CONV, CB), lambda i, j, a, b: (layer, 0, j)),
            pl.BlockSpec((None, 1, CB), lambda i, j, a, b: (layer, 0, j)),
        ],
        out_specs=pl.BlockSpec((tq, CB), lambda i, j, a, b: (i, j)),
        scratch_shapes=[pltpu.VMEM((tq + 2 * SUBLANES, CB), F32)],
    )
    return pl.pallas_call(
        _conv_body,
        grid_spec=grid_spec,
        out_shape=jax.ShapeDtypeStruct((t, COL_GATE - COL_X), BF16),
        compiler_params=_cparams(("arbitrary", "arbitrary")),
        name="conv_xbc",
    )(flags[0], flags[1], proj, proj, proj, conv_w, conv_b)


def _ssd_scalars(dt_ref, dtb_ref, alog_ref):
    v = dt_ref[...] + dtb_ref[...]
    dt = jnp.maximum(v, 0.0) + jnp.log1p(jnp.exp(-jnp.abs(v)))
    da = dt * (-jnp.exp(alog_ref[...]))
    r = lax.broadcasted_iota(I32, (CHUNK, CHUNK), 0)
    c = lax.broadcasted_iota(I32, (CHUNK, CHUNK), 1)
    tri = jnp.where(c <= r, 1.0, 0.0).astype(BF16)
    upper = jnp.where(r <= c, 1.0, 0.0).astype(BF16)
    cs = _dot3_r(tri, da)
    da_t = da.T
    dt_t = dt.T
    cs_t = _dot3_l(da_t, upper)
    return dt, cs, cs - da, dt_t, cs_t, cs_t - da_t, r, c


def _expand(v, e_ref):
    hi, mid, lo = _split3(v)
    e = e_ref[...]
    return _dot(hi, e) + _dot(mid, e) + _dot(lo, e)


def _state_step(s_scr, xs_f32, w_tok, dec_col, b_ref, et_ref):
    xw = (xs_f32 * w_tok).astype(BF16)
    dec = jnp.broadcast_to(dec_col, (CHUNK, STATE))
    dec_rows = _dot3_r(et_ref[...], dec)
    for g in range(GROUPS):
        sl = slice(g * GROUP_W, (g + 1) * GROUP_W)
        sc = _dot_tn(xw[:, sl], b_ref[:, g * STATE:(g + 1) * STATE])
        s_scr[sl, :] = s_scr[sl, :] * dec_rows[sl, :] + sc


def _ssd_fwd_body(first, xs_ref, b_ref, c_ref, dt_ref, dtb_ref, alog_ref, ef_ref, et_ref, y_ref, s_scr):
    ci = pl.program_id(0)

    @pl.when(first[ci] == 1)
    def _():
        s_scr[...] = jnp.zeros_like(s_scr)

    dt, cs, ex, dt_t, cs_t, ex_t, r, c = _ssd_scalars(dt_ref, dtb_ref, alog_ref)
    w_out = _expand(jnp.exp(cs), ef_ref)
    for g in range(GROUPS):
        sl = slice(g * GROUP_W, (g + 1) * GROUP_W)
        yo = _dot_nt(c_ref[:, g * STATE:(g + 1) * STATE], s_scr[sl, :].astype(BF16))
        y_ref[:, sl] = yo * w_out[:, sl]
    lane = lax.broadcasted_iota(I32, (CHUNK, 2 * HEAD_DIM), 1)
    lower = r >= c
    pairs_per_group = HEADS // GROUPS // 2
    for pi in range(HEADS // 2):
        g = pi // pairs_per_group
        if pi % pairs_per_group == 0:
            cb = _dot_nt(c_ref[:, g * STATE:(g + 1) * STATE], b_ref[:, g * STATE:(g + 1) * STATE])
        ms = []
        for h in (2 * pi, 2 * pi + 1):
            hb = HEADS + h
            arg = jnp.where(lower, cs[:, h:h + 1] - cs_t[h:h + 1, :], ex_t[hb:hb + 1, :] - ex[:, hb:hb + 1])
            d0 = dt_t[h:h + 1, :]
            d1 = dt_t[hb:hb + 1, :]
            w = jnp.where(r > c, d0, jnp.where(r < c, d1, d0 + d1))
            ms.append((jnp.exp(arg) * w * cb).astype(BF16))
        xp = xs_ref[:, pi * 2 * HEAD_DIM:(pi + 1) * 2 * HEAD_DIM]
        zero = jnp.zeros_like(xp)
        rhs = jnp.concatenate([jnp.where(lane < HEAD_DIM, xp, zero), jnp.where(lane >= HEAD_DIM, xp, zero)], axis=0)
        psl = slice(pi * 2 * HEAD_DIM, (pi + 1) * 2 * HEAD_DIM)
        y_ref[:, psl] = y_ref[:, psl] + _dot(jnp.concatenate(ms, axis=1), rhs)
    w_st = _expand(jnp.exp(cs[CHUNK - 1:CHUNK, :] - cs) * dt, ef_ref)
    _state_step(s_scr, xs_ref[...].astype(F32), w_st, jnp.exp(cs_t[:, CHUNK - 1:CHUNK]), b_ref, et_ref)


def _ssd_bwd_body(last, xs_ref, b_ref, c_ref, dt_ref, y1_ref, z_ref, dtb_ref, alog_ref, eb_ref, et_ref,
                  dskip_ref, nw_ref, wout_ref, o_ref, s_scr):
    ci = pl.program_id(0)
    nc = pl.num_programs(0)

    @pl.when(last[nc - 1 - ci] == 1)
    def _():
        s_scr[...] = jnp.zeros_like(s_scr)

    dt, cs, ex, dt_t, cs_t, ex_t, r, c = _ssd_scalars(dt_ref, dtb_ref, alog_ref)
    tot = cs[CHUNK - 1:CHUNK, :]
    w_out = _expand(jnp.exp(tot - ex), eb_ref)
    xs = xs_ref[...].astype(F32)
    z = z_ref[...].astype(F32)
    zs = z * _sigmoid(z)
    ys = []
    for g in range(GROUPS):
        sl = slice(g * GROUP_W, (g + 1) * GROUP_W)
        yo = _dot_nt(c_ref[:, g * STATE:(g + 1) * STATE], s_scr[sl, :].astype(BF16))
        y = (y1_ref[:, sl] + yo * w_out[:, sl] + xs[:, sl] * dskip_ref[:, sl]) * zs[:, sl]
        y = y * lax.rsqrt(jnp.mean(y * y, axis=-1, keepdims=True) + EPS)
        ys.append((y * nw_ref[:, sl]).astype(BF16))
    o_ref[...] = _dot(jnp.concatenate(ys, axis=1), wout_ref[...])
    w_st = _expand(jnp.exp(ex) * dt, eb_ref)
    _state_step(s_scr, xs, w_st, jnp.exp(cs_t[:, CHUNK - 1:CHUNK]), b_ref, et_ref)


def _expansion_matrices():
    rows = np.arange(LANES)[:, None]
    cols = np.arange(INNER)[None, :] // HEAD_DIM
    ef = (rows == cols).astype(np.float32)
    eb = (rows == cols + HEADS).astype(np.float32)
    return jnp.asarray(ef, BF16), jnp.asarray(eb, BF16), jnp.asarray(ef.T, BF16), jnp.asarray(eb.T, BF16)


def _ssd_fwd(xbc, dt_raw, dtb, alog, ef, eft, layer, st, first):
    t = st.total
    nb = GROUPS * STATE // CB
    grid_spec = pltpu.PrefetchScalarGridSpec(
        num_scalar_prefetch=1,
        grid=(t // CHUNK,),
        in_specs=[
            pl.BlockSpec((CHUNK, INNER), lambda i, f: (i, 0)),
            pl.BlockSpec((CHUNK, CB), lambda i, f: (i, INNER // CB)),
            pl.BlockSpec((CHUNK, CB), lambda i, f: (i, INNER // CB + nb)),
            pl.BlockSpec((CHUNK, LANES), lambda i, f: (i, 0)),
            pl.BlockSpec((None, 1, LANES), lambda i, f: (layer, 0, 0)),
            pl.BlockSpec((None, 1, LANES), lambda i, f: (layer, 0, 0)),
            pl.BlockSpec((LANES, INNER), lambda i, f: (0, 0)),
            pl.BlockSpec((INNER, LANES), lambda i, f: (0, 0)),
        ],
        out_specs=pl.BlockSpec((CHUNK, INNER), lambda i, f: (i, 0)),
        scratch_shapes=[pltpu.VMEM((INNER, STATE), F32)],
    )
    return pl.pallas_call(
        _ssd_fwd_body,
        grid_spec=grid_spec,
        out_shape=jax.ShapeDtypeStruct((t, INNER), F32),
        compiler_params=_cparams(("arbitrary",)),
        name="ssd_fwd",
    )(first, xbc, xbc, xbc, dt_raw, dtb, alog, ef, eft)


def _ssd_bwd(xbc, dt_raw, y1, proj, dtb, alog, eb, ebt, dskip, norm_w, w_out, layer, st, last):
    t = st.total
    nc = t // CHUNK
    nb = GROUPS * STATE // CB
    rev = lambda i: nc - 1 - i
    grid_spec = pltpu.PrefetchScalarGridSpec(
        num_scalar_prefetch=1,
        grid=(nc,),
        in_specs=[
            pl.BlockSpec((CHUNK, INNER), lambda i, f: (rev(i), 0)),
            pl.BlockSpec((CHUNK, CB), lambda i, f: (rev(i), INNER // CB)),
            pl.BlockSpec((CHUNK, CB), lambda i, f: (rev(i), INNER // CB + nb)),
            pl.BlockSpec((CHUNK, LANES), lambda i, f: (rev(i), 0)),
            pl.BlockSpec((CHUNK, INNER), lambda i, f: (rev(i), 0)),
            pl.BlockSpec((CHUNK, INNER), lambda i, f: (rev(i), 0)),
            pl.BlockSpec((None, 1, LANES), lambda i, f: (layer, 0, 0)),
            pl.BlockSpec((None, 1, LANES), lambda i, f: (layer, 0, 0)),
            pl.BlockSpec((LANES, INNER), lambda i, f: (0, 0)),
            pl.BlockSpec((INNER, LANES), lambda i, f: (0, 0)),
            pl.BlockSpec((None, 1, INNER), lambda i, f: (layer, 0, 0)),
            pl.BlockSpec((None, 1, INNER), lambda i, f: (layer, 0, 0)),
            pl.BlockSpec((None, INNER, D_MODEL), lambda i, f: (layer, 0, 0)),
        ],
        out_specs=pl.BlockSpec((CHUNK, D_MODEL), lambda i, f: (rev(i), 0)),
        scratch_shapes=[pltpu.VMEM((INNER, STATE), F32)],
    )
    return pl.pallas_call(
        _ssd_bwd_body,
        grid_spec=grid_spec,
        out_shape=jax.ShapeDtypeStruct((t, D_MODEL), F32),
        compiler_params=_cparams(("arbitrary",)),
        name="ssd_bwd",
    )(last, xbc, xbc, xbc, dt_raw, y1, proj, dtb, alog, eb, ebt, dskip, norm_w, w_out)


def _merge_body(topz, botz, x_ref, ossd_ref, gate_ref, scb_ref, scc_ref, scx_ref, ccp_ref, cxp_ref, ccn_ref, cxn_ref,
                cw_ref, wsc_ref, gb_ref, wo_ref, nw_ref, rw_ref, x1_ref, h2_ref, pt_ref, ext):
    i = pl.program_id(0)
    tm = x_ref.shape[0]
    halo = SUBLANES
    pad = SC_CONV // 2
    ext[0:halo, :] = jnp.where(topz[i] == 1, 0.0, ccp_ref[...].astype(F32) * cxp_ref[...].astype(F32))
    ext[halo:halo + tm, :] = scc_ref[...].astype(F32) * scx_ref[...].astype(F32)
    ext[halo + tm:2 * halo + tm, :] = jnp.where(botz[i] == 1, 0.0, ccn_ref[...].astype(F32) * cxn_ref[...].astype(F32))
    v = ext[halo - pad:halo - pad + tm, :] * cw_ref[0:1, :]
    for k in range(1, SC_CONV):
        v = v + ext[halo - pad + k:halo - pad + k + tm, :] * cw_ref[k:k + 1, :]
    o_sc = _dot((scb_ref[...].astype(F32) * v).astype(BF16), wsc_ref[...])
    g = _sigmoid(gate_ref[...].astype(F32) + gb_ref[...])
    mix = g[:, :D_MODEL] * ossd_ref[...] + g[:, D_MODEL:] * o_sc
    x1 = x_ref[...] + _dot(mix.astype(BF16), wo_ref[...])
    x1_ref[...] = x1
    h2 = x1 * lax.rsqrt(jnp.mean(x1 * x1, axis=-1, keepdims=True) + EPS) * nw_ref[...]
    h2_ref[...] = h2.astype(BF16)
    hh, hm, _ = _split3(h2)
    rw = rw_ref[...]
    rh, rm, _ = _split3(rw)
    logits = _dot_nt(rh, hh) + _dot_nt(rh, hm) + _dot_nt(rm, hh)
    mx = jnp.max(logits, axis=0, keepdims=True)
    ex = jnp.exp(logits - mx)
    pt_ref[...] = ex / jnp.sum(ex, axis=0, keepdims=True)


def _merge(x, o_ssd, proj, sc_w, w_sc, gate_b, w_o, norm2, router_t, layer, st, flags):
    t, tm = st.total, st.tq
    rpt = tm // SUBLANES
    last_rb = t // SUBLANES - 1
    cscb, cscc, cscx = COL_SCB // CB, COL_SCB // CB + 1, COL_SCB // CB + 2
    prev = lambda i: jnp.maximum(i * rpt - 1, 0)
    nxt = lambda i: jnp.minimum((i + 1) * rpt, last_rb)
    grid_spec = pltpu.PrefetchScalarGridSpec(
        num_scalar_prefetch=2,
        grid=(t // tm,),
        in_specs=[
            pl.BlockSpec((tm, D_MODEL), lambda i, a, b: (i, 0)),
            pl.BlockSpec((tm, D_MODEL), lambda i, a, b: (i, 0)),
            pl.BlockSpec((tm, 2 * D_MODEL), lambda i, a, b: (i, COL_GATE // (2 * D_MODEL))),
            pl.BlockSpec((tm, CB), lambda i, a, b: (i, cscb)),
            pl.BlockSpec((tm, CB), lambda i, a, b: (i, cscc)),
            pl.BlockSpec((tm, CB), lambda i, a, b: (i, cscx)),
            pl.BlockSpec((SUBLANES, CB), lambda i, a, b: (prev(i), cscc)),
            pl.BlockSpec((SUBLANES, CB), lambda i, a, b: (prev(i), cscx)),
            pl.BlockSpec((SUBLANES, CB), lambda i, a, b: (nxt(i), cscc)),
            pl.BlockSpec((SUBLANES, CB), lambda i, a, b: (nxt(i), cscx)),
            pl.BlockSpec((None, SC_CONV, SC_WIDTH), lambda i, a, b: (layer, 0, 0)),
            pl.BlockSpec((None, SC_WIDTH, D_MODEL), lambda i, a, b: (layer, 0, 0)),
            pl.BlockSpec((None, 1, 2 * D_MODEL), lambda i, a, b: (layer, 0, 0)),
            pl.BlockSpec((None, D_MODEL, D_MODEL), lambda i, a, b: (layer, 0, 0)),
            pl.BlockSpec((None, 1, D_MODEL), lambda i, a, b: (layer, 0, 0)),
            pl.BlockSpec((None, N_EXPERTS, D_MODEL), lambda i, a, b: (layer, 0, 0)),
        ],
        out_specs=[
            pl.BlockSpec((tm, D_MODEL), lambda i, a, b: (i, 0)),
            pl.BlockSpec((tm, D_MODEL), lambda i, a, b: (i, 0)),
            pl.BlockSpec((N_EXPERTS, tm), lambda i, a, b: (0, i)),
        ],
        scratch_shapes=[pltpu.VMEM((tm + 2 * SUBLANES, SC_WIDTH), F32)],
    )
    return pl.pallas_call(
        _merge_body,
        grid_spec=grid_spec,
        out_shape=[jax.ShapeDtypeStruct((t, D_MODEL), F32), jax.ShapeDtypeStruct((t, D_MODEL), BF16),
                   jax.ShapeDtypeStruct((N_EXPERTS, t), F32)],
        compiler_params=_cparams(("arbitrary",)),
        name="merge",
    )(flags[0], flags[1], x, o_ssd, proj, proj, proj, proj, proj, proj, proj, proj,
      sc_w, w_sc, gate_b, w_o, norm2, router_t)


def _route_body(p_ref, dest_ref, posx_ref, *, cap):
    rows = p_ref.shape[1]
    r = lax.broadcasted_iota(I32, (LANES, LANES), 0)
    c = lax.broadcasted_iota(I32, (LANES, LANES), 1)
    upper = jnp.where(r <= c, 1.0, 0.0).astype(BF16)
    last_col = jnp.where(r == LANES - 1, 1.0, 0.0).astype(BF16)
    rr = lax.broadcasted_iota(I32, (rows, rows), 0)
    rc = lax.broadcasted_iota(I32, (rows, rows), 1)
    below = jnp.where(rc < rr, 1.0, 0.0).astype(BF16)

    def count(mask):
        s = jnp.sum(jnp.where(mask, 1.0, 0.0), axis=0, keepdims=True)
        return jnp.sum(s, axis=1, keepdims=True)

    def excl_prefix(mask):
        x = jnp.where(mask, 1.0, 0.0)
        incl = _dot(x.astype(BF16), upper)
        row_tot = _dot(incl.astype(BF16), last_col)
        row_off = _dot(below, row_tot.astype(BF16))
        return incl + row_off - x

    def search(i, prefix):
        bit = jnp.left_shift(jnp.int32(1), 30 - i)
        out = []
        for e in range(N_EXPERTS):
            keys = pltpu.bitcast(p_ref[e], I32)
            cand = prefix[e] | bit
            out.append(jnp.where(count(keys >= cand) >= cap, cand, prefix[e]))
        return tuple(out)

    thr = lax.fori_loop(0, 31, search, tuple(jnp.zeros((1, 1), I32) for _ in range(N_EXPERTS)))
    for e in range(N_EXPERTS):
        keys = pltpu.bitcast(p_ref[e], I32)
        gt = keys > thr[e]
        eq = keys == thr[e]
        need = cap - count(gt)
        sel = gt | (eq & (excl_prefix(eq) < need))
        pos = excl_prefix(sel).astype(I32)
        posx_ref[e] = pos
        dest_ref[e] = jnp.where(sel, pos, -1)


def _route(probs3, cap):
    e, rows, _ = probs3.shape
    return pl.pallas_call(
        functools.partial(_route_body, cap=cap),
        out_shape=[jax.ShapeDtypeStruct((e, rows, LANES), I32), jax.ShapeDtypeStruct((e, rows, LANES), I32)],
        compiler_params=pltpu.CompilerParams(vmem_limit_bytes=VMEM_LIMIT),
        name="route",
    )(probs3)


def _pair_lists(posx_list, st):
    ts, tk = st.ts, st.tk
    rb = tk // LANES
    g_kb, g_jg, g_meta = [], [], []
    s_kb, s_jg, s_meta, s_e = [], [], [], []
    for g, posx in enumerate(posx_list):
        cap, nj, nkb = st.cap_g[g], st.j_g[g], st.nkb_g[g]
        kb_off, tile_off = st.off_g[g] // tk, st.tile_off_g[g]
        c0 = posx[:, ::rb, 0]
        c1 = jnp.concatenate([c0[:, 1:], jnp.full((N_EXPERTS, 1), cap, I32)], axis=1)
        jlo = jnp.minimum(c0 // ts, nj - 1)
        jhi = jnp.where(c1 > c0, (c1 - 1) // ts, jlo)
        n = jhi - jlo + 1
        npg = nkb + nj
        cum = jnp.cumsum(n, axis=1)
        p = jnp.arange(npg, dtype=I32)
        kb = jax.vmap(lambda cr: jnp.searchsorted(cr, p, side="right"))(cum).astype(I32)
        valid = p[None, :] < cum[:, -1:]
        kb = jnp.minimum(kb, nkb - 1)
        start = jnp.take_along_axis(cum - n, kb, axis=1)
        j = jnp.take_along_axis(jlo, kb, axis=1) + p[None, :] - start
        j = jnp.where(valid, j, nj - 1)
        jprev = jnp.concatenate([jnp.full((N_EXPERTS, 1), -1, I32), j[:, :-1]], axis=1)
        jnext = jnp.concatenate([j[:, 1:], jnp.full((N_EXPERTS, 1), -1, I32)], axis=1)
        vnext = jnp.concatenate([valid[:, 1:], jnp.zeros((N_EXPERTS, 1), bool)], axis=1)
        first = valid & (j != jprev)
        last = valid & ((j != jnext) | ~vnext)
        g_kb.append(kb + kb_off)
        g_jg.append(tile_off + jnp.arange(N_EXPERTS, dtype=I32)[:, None] * nj + j)
        g_meta.append(j * 8 + first.astype(I32) + 2 * last.astype(I32) + 4 * valid.astype(I32))
        nps = N_EXPERTS * npg
        nf = n.T.reshape(-1)
        cumf = jnp.cumsum(nf)
        q = jnp.arange(nps, dtype=I32)
        idx = jnp.minimum(jnp.searchsorted(cumf, q, side="right").astype(I32), nkb * N_EXPERTS - 1)
        validf = q < cumf[-1]
        kbf = idx // N_EXPERTS
        ef = idx % N_EXPERTS
        jf = jlo.T.reshape(-1)[idx] + q - (cumf - nf)[idx]
        jf = jnp.where(validf, jf, jhi[N_EXPERTS - 1, nkb - 1])
        kprev = jnp.concatenate([jnp.full((1,), -1, I32), kbf[:-1]])
        firstf = validf & (kbf != kprev)
        s_kb.append(kbf + kb_off)
        s_jg.append(tile_off + ef * nj + jf)
        s_e.append(ef)
        s_meta.append(jf * 8 + firstf.astype(I32) + 4 * validf.astype(I32))
    gather = tuple(jnp.concatenate(v, axis=1).reshape(-1) for v in (g_kb, g_jg, g_meta))
    scatter = tuple(jnp.concatenate(v, axis=0) for v in (s_kb, s_jg, s_meta, s_e))
    return gather, scatter


def _one_hot(dest_row, j_local, ts):
    tk = dest_row.shape[-1]
    slot = j_local * ts + lax.broadcasted_iota(I32, (ts, tk), 0)
    return dest_row == slot


def _moe_ffn_body(kb_s, jg_s, meta_s, dest_ref, gate_ref, h_ref, wg_ref, wu_ref, wd_ref, o_ref, acc, gacc):
    p = pl.program_id(0) * pl.num_programs(1) + pl.program_id(1)
    meta = meta_s[p]
    ts = o_ref.shape[0]

    @pl.when((meta & 1) == 1)
    def _():
        acc[...] = jnp.zeros_like(acc)
        gacc[...] = jnp.zeros_like(gacc)

    @pl.when((meta & 4) == 4)
    def _():
        hit = _one_hot(dest_ref[...], meta >> 3, ts)
        acc[...] += _dot(jnp.where(hit, 1.0, 0.0).astype(BF16), h_ref[...])
        gacc[...] += jnp.sum(jnp.where(hit, gate_ref[...], 0.0), axis=1, keepdims=True)

    @pl.when((meta & 2) == 2)
    def _():
        xe = acc[...].astype(BF16)
        a = _dot(xe, wg_ref[...])
        u = _dot(xe, wu_ref[...])
        act = (a * _sigmoid(a) * u).astype(BF16)
        o_ref[...] = (_dot(act, wd_ref[...]) * gacc[...]).astype(BF16)


def _moe_ffn(lists, dest3, gate3, h2, wg, wu, wd, layer, st):
    ts, tk = st.ts, st.tk
    npe = sum(nkb + nj for nkb, nj in zip(st.nkb_g, st.j_g))
    idx = lambda e, q: e * npe + q
    grid_spec = pltpu.PrefetchScalarGridSpec(
        num_scalar_prefetch=3,
        grid=(N_EXPERTS, npe),
        in_specs=[
            pl.BlockSpec((None, 1, tk), lambda e, q, kb, jg, m: (e, 0, kb[idx(e, q)])),
            pl.BlockSpec((None, 1, tk), lambda e, q, kb, jg, m: (e, 0, kb[idx(e, q)])),
            pl.BlockSpec((tk, D_MODEL), lambda e, q, kb, jg, m: (kb[idx(e, q)], 0)),
            pl.BlockSpec((None, None, D_MODEL, EXPERT_FF), lambda e, q, kb, jg, m: (layer, e, 0, 0)),
            pl.BlockSpec((None, None, D_MODEL, EXPERT_FF), lambda e, q, kb, jg, m: (layer, e, 0, 0)),
            pl.BlockSpec((None, None, EXPERT_FF, D_MODEL), lambda e, q, kb, jg, m: (layer, e, 0, 0)),
        ],
        out_specs=pl.BlockSpec((ts, D_MODEL), lambda e, q, kb, jg, m: (jg[idx(e, q)], 0)),
        scratch_shapes=[pltpu.VMEM((ts, D_MODEL), F32), pltpu.VMEM((ts, 1), F32)],
    )
    return pl.pallas_call(
        _moe_ffn_body,
        grid_spec=grid_spec,
        out_shape=jax.ShapeDtypeStruct((st.n_tiles * ts, D_MODEL), BF16),
        compiler_params=_cparams(("arbitrary", "arbitrary")),
        name="moe_ffn",
    )(*lists, dest3, gate3, h2, wg, wu, wd)


def _moe_scatter_body(kb_s, jg_s, meta_s, e_s, dest_ref, ye_ref, x_ref, o_ref):
    p = pl.program_id(0)
    meta = meta_s[p]
    ts = ye_ref.shape[0]

    @pl.when((meta & 1) == 1)
    def _():
        o_ref[...] = x_ref[...]

    @pl.when((meta & 4) == 4)
    def _():
        hit = _one_hot(dest_ref[...], meta >> 3, ts)
        o_ref[...] += _dot_tn(jnp.where(hit, 1.0, 0.0).astype(BF16), ye_ref[...])


def _moe_scatter(lists, dest3, ye, x1, st):
    ts, tk = st.ts, st.tk
    n_steps = lists[0].shape[0]
    grid_spec = pltpu.PrefetchScalarGridSpec(
        num_scalar_prefetch=4,
        grid=(n_steps,),
        in_specs=[
            pl.BlockSpec((None, 1, tk), lambda p, kb, jg, m, e: (e[p], 0, kb[p])),
            pl.BlockSpec((ts, D_MODEL), lambda p, kb, jg, m, e: (jg[p], 0)),
            pl.BlockSpec((tk, D_MODEL), lambda p, kb, jg, m, e: (kb[p], 0)),
        ],
        out_specs=pl.BlockSpec((tk, D_MODEL), lambda p, kb, jg, m, e: (kb[p], 0)),
    )
    return pl.pallas_call(
        _moe_scatter_body,
        grid_spec=grid_spec,
        out_shape=jax.ShapeDtypeStruct((st.total, D_MODEL), F32),
        compiler_params=_cparams(("arbitrary",)),
        name="moe_scatter",
    )(*lists, dest3, ye, x1)


def _final_norm_body(x_ref, w_ref, o_ref):
    x = x_ref[...]
    o_ref[...] = x * lax.rsqrt(jnp.mean(x * x, axis=-1, keepdims=True) + EPS) * w_ref[...]


def _final_norm(x, w, st):
    t, tm = st.total, st.tm
    return pl.pallas_call(
        _final_norm_body,
        grid=(t // tm,),
        in_specs=[pl.BlockSpec((tm, D_MODEL), lambda i: (i, 0)), pl.BlockSpec((1, D_MODEL), lambda i: (0, 0))],
        out_specs=pl.BlockSpec((tm, D_MODEL), lambda i: (i, 0)),
        out_shape=jax.ShapeDtypeStruct((t, D_MODEL), F32),
        compiler_params=_cparams(("arbitrary",)),
        name="final_norm",
    )(x, w)


def _pad_lanes(v, width=LANES):
    return jnp.pad(v, [(0, 0)] * (v.ndim - 1) + [(0, width - v.shape[-1])])


def _trunk_stream(xs, params, tm, tq, tk, ts):
    (norm1_w, w_in, ssd_conv_w, ssd_conv_b, ssd_dt_bias, ssd_a_log, ssd_d, ssd_norm_w, w_ssd_out, sc_conv_w,
     w_sc_out, gate_b, w_o, norm2_w, router_w, w_gate, w_up, w_down, final_norm_w) = params
    depth = w_in.shape[0]
    st = _Stream([(x.shape[0], x.shape[1]) for x in xs], tm, tq, tk, ts)
    x = jnp.concatenate([v.reshape(-1, D_MODEL) for v in xs], axis=0)

    w_main = jnp.concatenate([w_in[:, :, :_OFF_DT], w_in[:, :, _OFF_GATE:], w_in[:, :, _OFF_SCB:_OFF_GATE]],
                             axis=2).astype(BF16)
    w_dt = _pad_lanes(w_in[:, :, _OFF_DT:_OFF_SCB]).astype(BF16)
    dtb = _pad_lanes(ssd_dt_bias.reshape(depth, 1, 2 * HEADS))
    alog = _pad_lanes(ssd_a_log.reshape(depth, 1, 2 * HEADS))
    dskip = jnp.repeat(ssd_d, HEAD_DIM, axis=-1).reshape(depth, 1, INNER)
    ef, eb, eft, ebt = _expansion_matrices()
    router_t = jnp.swapaxes(router_w, 1, 2)
    wg, wu, wd = w_gate.astype(BF16), w_up.astype(BF16), w_down.astype(BF16)
    row = lambda v: v.reshape(depth, 1, v.shape[-1])

    chunk_first, chunk_last = st.seq_flags(CHUNK)
    tile_flags = st.seq_flags(tq)

    for l in range(depth):
        proj, dt_raw = _inproj(x, row(norm1_w), w_main, w_dt, l, st)
        xbc = _conv_xbc(proj, ssd_conv_w, row(ssd_conv_b), l, st, tile_flags)
        y1 = _ssd_fwd(xbc, dt_raw, dtb, alog, ef, eft, l, st, chunk_first)
        o_ssd = _ssd_bwd(xbc, dt_raw, y1, proj, dtb, alog, eb, ebt, dskip, row(ssd_norm_w),
                         w_ssd_out.astype(BF16), l, st, chunk_last)
        x1, h2, probs_t = _merge(x, o_ssd, proj, sc_conv_w, w_sc_out.astype(BF16), row(gate_b), w_o.astype(BF16),
                                 row(norm2_w), router_t, l, st, tile_flags)
        dests, posxs = [], []
        for g in range(len(xs)):
            pg = probs_t[:, st.off_g[g]:st.off_g[g] + st.t_g[g]].reshape(N_EXPERTS, st.t_g[g] // LANES, LANES)
            dest, posx = _route(pg, st.cap_g[g])
            dests.append(dest.reshape(N_EXPERTS, 1, st.t_g[g]))
            posxs.append(posx)
        dest3 = jnp.concatenate(dests, axis=2)
        gather_lists, scatter_lists = _pair_lists(posxs, st)
        ye = _moe_ffn(gather_lists, dest3, probs_t.reshape(N_EXPERTS, 1, st.total), h2, wg, wu, wd, l, st)
        x = _moe_scatter(scatter_lists, dest3, ye, x1, st)

    y = _final_norm(x, final_norm_w.reshape(1, D_MODEL), st)
    return [y[o:o + t].reshape(v.shape) for o, t, v in zip(st.off_g, st.t_g, xs)]


def kernel(x_prompt, x_sample, norm1_w, w_in, ssd_conv_w, ssd_conv_b, ssd_dt_bias, ssd_a_log, ssd_d, ssd_norm_w, w_ssd_out, sc_conv_w, w_sc_out, gate_b, w_o, norm2_w, router_w, w_gate, w_up, w_down, final_norm_w):
    params = (norm1_w, w_in, ssd_conv_w, ssd_conv_b, ssd_dt_bias, ssd_a_log, ssd_d, ssd_norm_w, w_ssd_out, sc_conv_w,
              w_sc_out, gate_b, w_o, norm2_w, router_w, w_gate, w_up, w_down, final_norm_w)
    y_prompt, y_sample = _trunk_stream([x_prompt, x_sample], params, tm=1024, tq=512, tk=1024, ts=128)
    return (y_prompt, y_sample)
```

```python
import functools

import numpy as np
import jax
import jax.numpy as jnp
from jax import lax
from jax.experimental import pallas as pl
from jax.experimental.pallas import tpu as pltpu

F32 = jnp.float32
BF16 = jnp.bfloat16
I32 = jnp.int32

D_MODEL = 1024
HEADS = 24
HEAD_DIM = 64
INNER = HEADS * HEAD_DIM
GROUPS = 4
GROUP_W = INNER // GROUPS
STATE = 128
CHUNK = 128
SSD_CONV = 5
SC_WIDTH = 512
SC_CONV = 3
N_EXPERTS = 16
EXPERT_FF = 2048
CAPACITY_FACTOR = 2
EPS = 1e-6
LANES = 128
SUBLANES = 8

_OFF_XBC = INNER
_OFF_DT = _OFF_XBC + INNER + 2 * GROUPS * STATE
_OFF_SCB = _OFF_DT + 2 * HEADS
_OFF_GATE = _OFF_SCB + 3 * SC_WIDTH
_N_IN = _OFF_GATE + 2 * D_MODEL
COL_X = INNER
COL_B = 2 * INNER
COL_C = COL_B + GROUPS * STATE
COL_GATE = COL_C + GROUPS * STATE
COL_SCB = COL_GATE + 2 * D_MODEL
N_MAIN = COL_SCB + 3 * SC_WIDTH
CB = 512

BF16_ROWS = 2 * SUBLANES
PASS = 64
WIN = PASS + BF16_ROWS
XW = D_MODEL + LANES
GATE_STRIDE = N_EXPERTS

VMEM_LIMIT = 50 * 1024 * 1024


def _cparams(sem):
    return pltpu.CompilerParams(dimension_semantics=sem, vmem_limit_bytes=VMEM_LIMIT)


def _sigmoid(v):
    return 1.0 / (1.0 + jnp.exp(-v))


def _split3(v):
    hi = v.astype(BF16)
    r = v - hi.astype(F32)
    mid = r.astype(BF16)
    lo = (r - mid.astype(F32)).astype(BF16)
    return hi, mid, lo


def _dot(a, b):
    return jnp.dot(a, b, preferred_element_type=F32)


def _dot_nt(a, b):
    return lax.dot_general(a, b, (((1,), (1,)), ((), ())), preferred_element_type=F32)


def _dot_tn(a, b):
    return lax.dot_general(a, b, (((0,), (0,)), ((), ())), preferred_element_type=F32)


def _dot3_l(a_f32, b_exact):
    hi, mid, lo = _split3(a_f32)
    return _dot(hi, b_exact) + _dot(mid, b_exact) + _dot(lo, b_exact)


def _dot3_r(a_exact, b_f32):
    hi, mid, lo = _split3(b_f32)
    return _dot(a_exact, hi) + _dot(a_exact, mid) + _dot(a_exact, lo)


class _Stream:
    def __init__(self, groups, tm, tq, tk, ts):
        self.groups = tuple(groups)
        self.tm, self.tq, self.tk, self.ts = tm, tq, tk, ts
        self.t_g = [n * l for n, l in self.groups]
        self.off_g = [int(v) for v in np.cumsum([0] + self.t_g[:-1])]
        self.total = int(sum(self.t_g))
        self.cap_g = [CAPACITY_FACTOR * t // N_EXPERTS for t in self.t_g]
        self.nkb_g = [t // tk for t in self.t_g]
        for (n, l), t, c in zip(self.groups, self.t_g, self.cap_g):
            assert l % CHUNK == 0 and l % tq == 0 and t % tk == 0 and c % ts == 0 and tk % LANES == 0
        assert self.total % tm == 0 and self.total % tq == 0 and ts >= WIN
        self.region_g = [c + ts for c in self.cap_g]
        self.rows_per_expert = int(sum(self.region_g))
        self.tiles_per_expert = self.rows_per_expert // ts
        self.slot_rows = N_EXPERTS * self.rows_per_expert
        self.region_off_g = [int(v) for v in np.cumsum([0] + self.region_g[:-1])]

    def block_tables(self):
        rs, gfirst, cap = [], [], []
        for g, nkb in enumerate(self.nkb_g):
            for kb in range(nkb):
                gfirst.append(1 if kb == 0 else 0)
                cap.append(self.cap_g[g])
                rs.extend(e * self.rows_per_expert + self.region_off_g[g] for e in range(N_EXPERTS))
        return tuple(jnp.asarray(np.asarray(v, np.int32)) for v in (rs, gfirst, cap))

    def tile_tables(self):
        pad, src = [], []
        for e in range(N_EXPERTS):
            for g, c in enumerate(self.cap_g):
                n = c // self.ts
                base = len(pad)
                pad.extend([0] * n + [1])
                src.extend(list(range(base, base + n)) + [base + n - 1])
        return jnp.asarray(np.asarray(pad, np.int32)), jnp.asarray(np.asarray(src, np.int32))

    def seq_flags(self, tile):
        n_tiles = self.total // tile
        first = np.zeros((n_tiles,), np.int32)
        last = np.zeros((n_tiles,), np.int32)
        for (n, l), off in zip(self.groups, self.off_g):
            for s in range(n):
                first[(off + s * l) // tile] = 1
                last[(off + (s + 1) * l) // tile - 1] = 1
        return jnp.asarray(first), jnp.asarray(last)


def _inproj_body(x_ref, nw_ref, w_ref, wdt_ref, proj_ref, dt_ref, h_scr):
    @pl.when(pl.program_id(1) == 0)
    def _():
        x = x_ref[...]
        ms = jnp.mean(x * x, axis=-1, keepdims=True)
        h = (x * lax.rsqrt(ms + EPS) * nw_ref[...]).astype(BF16)
        h_scr[...] = h
        dt_ref[...] = _dot(h, wdt_ref[...])

    proj_ref[...] = _dot(h_scr[...], w_ref[...]).astype(BF16)


def _inproj(x, nw, w_main, w_dt, layer, st):
    t, tm = st.total, st.tm
    tn = INNER
    return pl.pallas_call(
        _inproj_body,
        grid=(t // tm, N_MAIN // tn),
        in_specs=[
            pl.BlockSpec((tm, D_MODEL), lambda i, n: (i, 0)),
            pl.BlockSpec((None, 1, D_MODEL), lambda i, n: (layer, 0, 0)),
            pl.BlockSpec((None, D_MODEL, tn), lambda i, n: (layer, 0, n)),
            pl.BlockSpec((None, D_MODEL, LANES), lambda i, n: (layer, 0, 0)),
        ],
        out_specs=[
            pl.BlockSpec((tm, tn), lambda i, n: (i, n)),
            pl.BlockSpec((tm, LANES), lambda i, n: (i, 0)),
        ],
        out_shape=[jax.ShapeDtypeStruct((t, N_MAIN), BF16), jax.ShapeDtypeStruct((t, LANES), F32)],
        scratch_shapes=[pltpu.VMEM((tm, D_MODEL), BF16)],
        compiler_params=_cparams(("arbitrary", "arbitrary")),
        name="inproj",
    )(x, nw, w_main, w_dt)


def _conv_body(topz, botz, main_ref, prev_ref, next_ref, w_ref, b_ref, o_ref, ext):
    i = pl.program_id(0)
    tq = main_ref.shape[0]
    halo = SUBLANES
    pad = SSD_CONV // 2
    ext[0:halo, :] = jnp.where(topz[i] == 1, 0.0, prev_ref[...].astype(F32))
    ext[halo:halo + tq, :] = main_ref[...].astype(F32)
    ext[halo + tq:2 * halo + tq, :] = jnp.where(botz[i] == 1, 0.0, next_ref[...].astype(F32))
    acc = b_ref[...] + ext[halo - pad:halo - pad + tq, :] * w_ref[0:1, :]
    for k in range(1, SSD_CONV):
        acc = acc + ext[halo - pad + k:halo - pad + k + tq, :] * w_ref[k:k + 1, :]
    o_ref[...] = (acc * _sigmoid(acc)).astype(BF16)


def _conv_xbc(proj, conv_w, conv_b, layer, st, flags):
    t, tq = st.total, st.tq
    ncb = (COL_GATE - COL_X) // CB
    cb0 = COL_X // CB
    rpt = tq // SUBLANES
    last_rb = t // SUBLANES - 1
    grid_spec = pltpu.PrefetchScalarGridSpec(
        num_scalar_prefetch=2,
        grid=(t // tq, ncb),
        in_specs=[
            pl.BlockSpec((tq, CB), lambda i, j, a, b: (i, cb0 + j)),
            pl.BlockSpec((SUBLANES, CB), lambda i, j, a, b: (jnp.maximum(i * rpt - 1, 0), cb0 + j)),
            pl.BlockSpec((SUBLANES, CB), lambda i, j, a, b: (jnp.minimum((i + 1) * rpt, last_rb), cb0 + j)),
            pl.BlockSpec((None, SSD_CONV, CB), lambda i, j, a, b: (layer, 0, j)),
            pl.BlockSpec((None, 1, CB), lambda i, j, a, b: (layer, 0, j)),
        ],
        out_specs=pl.BlockSpec((tq, CB), lambda i, j, a, b: (i, j)),
        scratch_shapes=[pltpu.VMEM((tq + 2 * SUBLANES, CB), F32)],
    )
    return pl.pallas_call(
        _conv_body,
        grid_spec=grid_spec,
        out_shape=jax.ShapeDtypeStruct((t, COL_GATE - COL_X), BF16),
        compiler_params=_cparams(("arbitrary", "arbitrary")),
        name="conv_xbc",
    )(flags[0], flags[1], proj, proj, proj, conv_w, conv_b)


def _ssd_scalars(dt_ref, dtb_ref, alog_ref):
    v = dt_ref[...] + dtb_ref[...]
    dt = jnp.maximum(v, 0.0) + jnp.log1p(jnp.exp(-jnp.abs(v)))
    da = dt * (-jnp.exp(alog_ref[...]))
    r = lax.broadcasted_iota(I32, (CHUNK, CHUNK), 0)
    c = lax.broadcasted_iota(I32, (CHUNK, CHUNK), 1)
    tri = jnp.where(c <= r, 1.0, 0.0).astype(BF16)
    upper = jnp.where(r <= c, 1.0, 0.0).astype(BF16)
    cs = _dot3_r(tri, da)
    da_t = da.T
    dt_t = dt.T
    cs_t = _dot3_l(da_t, upper)
    return dt, cs, cs - da, dt_t, cs_t, cs_t - da_t, r, c


def _expand(v, e_ref):
    hi, mid, lo = _split3(v)
    e = e_ref[...]
    return _dot(hi, e) + _dot(mid, e) + _dot(lo, e)


def _state_step(s_scr, xs_f32, w_tok, dec_col, b_ref, et_ref):
    xw = (xs_f32 * w_tok).astype(BF16)
    dec = jnp.broadcast_to(dec_col, (CHUNK, STATE))
    dec_rows = _dot3_r(et_ref[...], dec)
    for g in range(GROUPS):
        sl = slice(g * GROUP_W, (g + 1) * GROUP_W)
        sc = _dot_tn(xw[:, sl], b_ref[:, g * STATE:(g + 1) * STATE])
        s_scr[sl, :] = s_scr[sl, :] * dec_rows[sl, :] + sc


def _ssd_fwd_body(first, xs_ref, b_ref, c_ref, dt_ref, dtb_ref, alog_ref, ef_ref, et_ref, y_ref, s_scr):
    ci = pl.program_id(0)

    @pl.when(first[ci] == 1)
    def _():
        s_scr[...] = jnp.zeros_like(s_scr)

    dt, cs, ex, dt_t, cs_t, ex_t, r, c = _ssd_scalars(dt_ref, dtb_ref, alog_ref)
    w_out = _expand(jnp.exp(cs), ef_ref)
    for g in range(GROUPS):
        sl = slice(g * GROUP_W, (g + 1) * GROUP_W)
        yo = _dot_nt(c_ref[:, g * STATE:(g + 1) * STATE], s_scr[sl, :].astype(BF16))
        y_ref[:, sl] = yo * w_out[:, sl]
    lane = lax.broadcasted_iota(I32, (CHUNK, 2 * HEAD_DIM), 1)
    lower = r >= c
    pairs_per_group = HEADS // GROUPS // 2
    for pi in range(HEADS // 2):
        g = pi // pairs_per_group
        if pi % pairs_per_group == 0:
            cb = _dot_nt(c_ref[:, g * STATE:(g + 1) * STATE], b_ref[:, g * STATE:(g + 1) * STATE])
        ms = []
        for h in (2 * pi, 2 * pi + 1):
            hb = HEADS + h
            arg = jnp.where(lower, cs[:, h:h + 1] - cs_t[h:h + 1, :], ex_t[hb:hb + 1, :] - ex[:, hb:hb + 1])
            d0 = dt_t[h:h + 1, :]
            d1 = dt_t[hb:hb + 1, :]
            w = jnp.where(r > c, d0, jnp.where(r < c, d1, d0 + d1))
            ms.append((jnp.exp(arg) * w * cb).astype(BF16))
        xp = xs_ref[:, pi * 2 * HEAD_DIM:(pi + 1) * 2 * HEAD_DIM]
        zero = jnp.zeros_like(xp)
        rhs = jnp.concatenate([jnp.where(lane < HEAD_DIM, xp, zero), jnp.where(lane >= HEAD_DIM, xp, zero)], axis=0)
        psl = slice(pi * 2 * HEAD_DIM, (pi + 1) * 2 * HEAD_DIM)
        y_ref[:, psl] = y_ref[:, psl] + _dot(jnp.concatenate(ms, axis=1), rhs)
    w_st = _expand(jnp.exp(cs[CHUNK - 1:CHUNK, :] - cs) * dt, ef_ref)
    _state_step(s_scr, xs_ref[...].astype(F32), w_st, jnp.exp(cs_t[:, CHUNK - 1:CHUNK]), b_ref, et_ref)


def _ssd_bwd_body(last, xs_ref, b_ref, c_ref, dt_ref, y1_ref, z_ref, dtb_ref, alog_ref, eb_ref, et_ref,
                  dskip_ref, nw_ref, wout_ref, o_ref, s_scr):
    ci = pl.program_id(0)
    nc = pl.num_programs(0)

    @pl.when(last[nc - 1 - ci] == 1)
    def _():
        s_scr[...] = jnp.zeros_like(s_scr)

    dt, cs, ex, dt_t, cs_t, ex_t, r, c = _ssd_scalars(dt_ref, dtb_ref, alog_ref)
    tot = cs[CHUNK - 1:CHUNK, :]
    w_out = _expand(jnp.exp(tot - ex), eb_ref)
    xs = xs_ref[...].astype(F32)
    z = z_ref[...].astype(F32)
    zs = z * _sigmoid(z)
    ys = []
    for g in range(GROUPS):
        sl = slice(g * GROUP_W, (g + 1) * GROUP_W)
        yo = _dot_nt(c_ref[:, g * STATE:(g + 1) * STATE], s_scr[sl, :].astype(BF16))
        y = (y1_ref[:, sl] + yo * w_out[:, sl] + xs[:, sl] * dskip_ref[:, sl]) * zs[:, sl]
        y = y * lax.rsqrt(jnp.mean(y * y, axis=-1, keepdims=True) + EPS)
        ys.append((y * nw_ref[:, sl]).astype(BF16))
    o_ref[...] = _dot(jnp.concatenate(ys, axis=1), wout_ref[...])
    w_st = _expand(jnp.exp(ex) * dt, eb_ref)
    _state_step(s_scr, xs, w_st, jnp.exp(cs_t[:, CHUNK - 1:CHUNK]), b_ref, et_ref)


def _expansion_matrices():
    rows = np.arange(LANES)[:, None]
    cols = np.arange(INNER)[None, :] // HEAD_DIM
    ef = (rows == cols).astype(np.float32)
    eb = (rows == cols + HEADS).astype(np.float32)
    return jnp.asarray(ef, BF16), jnp.asarray(eb, BF16), jnp.asarray(ef.T, BF16), jnp.asarray(eb.T, BF16)


def _ssd_fwd(xbc, dt_raw, dtb, alog, ef, eft, layer, st, first):
    t = st.total
    nb = GROUPS * STATE // CB
    grid_spec = pltpu.PrefetchScalarGridSpec(
        num_scalar_prefetch=1,
        grid=(t // CHUNK,),
        in_specs=[
            pl.BlockSpec((CHUNK, INNER), lambda i, f: (i, 0)),
            pl.BlockSpec((CHUNK, CB), lambda i, f: (i, INNER // CB)),
            pl.BlockSpec((CHUNK, CB), lambda i, f: (i, INNER // CB + nb)),
            pl.BlockSpec((CHUNK, LANES), lambda i, f: (i, 0)),
            pl.BlockSpec((None, 1, LANES), lambda i, f: (layer, 0, 0)),
            pl.BlockSpec((None, 1, LANES), lambda i, f: (layer, 0, 0)),
            pl.BlockSpec((LANES, INNER), lambda i, f: (0, 0)),
            pl.BlockSpec((INNER, LANES), lambda i, f: (0, 0)),
        ],
        out_specs=pl.BlockSpec((CHUNK, INNER), lambda i, f: (i, 0)),
        scratch_shapes=[pltpu.VMEM((INNER, STATE), F32)],
    )
    return pl.pallas_call(
        _ssd_fwd_body,
        grid_spec=grid_spec,
        out_shape=jax.ShapeDtypeStruct((t, INNER), F32),
        compiler_params=_cparams(("arbitrary",)),
        name="ssd_fwd",
    )(first, xbc, xbc, xbc, dt_raw, dtb, alog, ef, eft)


def _ssd_bwd(xbc, dt_raw, y1, proj, dtb, alog, eb, ebt, dskip, norm_w, w_out, layer, st, last):
    t = st.total
    nc = t // CHUNK
    nb = GROUPS * STATE // CB
    rev = lambda i: nc - 1 - i
    grid_spec = pltpu.PrefetchScalarGridSpec(
        num_scalar_prefetch=1,
        grid=(nc,),
        in_specs=[
            pl.BlockSpec((CHUNK, INNER), lambda i, f: (rev(i), 0)),
            pl.BlockSpec((CHUNK, CB), lambda i, f: (rev(i), INNER // CB)),
            pl.BlockSpec((CHUNK, CB), lambda i, f: (rev(i), INNER // CB + nb)),
            pl.BlockSpec((CHUNK, LANES), lambda i, f: (rev(i), 0)),
            pl.BlockSpec((CHUNK, INNER), lambda i, f: (rev(i), 0)),
            pl.BlockSpec((CHUNK, INNER), lambda i, f: (rev(i), 0)),
            pl.BlockSpec((None, 1, LANES), lambda i, f: (layer, 0, 0)),
            pl.BlockSpec((None, 1, LANES), lambda i, f: (layer, 0, 0)),
            pl.BlockSpec((LANES, INNER), lambda i, f: (0, 0)),
            pl.BlockSpec((INNER, LANES), lambda i, f: (0, 0)),
            pl.BlockSpec((None, 1, INNER), lambda i, f: (layer, 0, 0)),
            pl.BlockSpec((None, 1, INNER), lambda i, f: (layer, 0, 0)),
            pl.BlockSpec((None, INNER, D_MODEL), lambda i, f: (layer, 0, 0)),
        ],
        out_specs=pl.BlockSpec((CHUNK, D_MODEL), lambda i, f: (rev(i), 0)),
        scratch_shapes=[pltpu.VMEM((INNER, STATE), F32)],
    )
    return pl.pallas_call(
        _ssd_bwd_body,
        grid_spec=grid_spec,
        out_shape=jax.ShapeDtypeStruct((t, D_MODEL), F32),
        compiler_params=_cparams(("arbitrary",)),
        name="ssd_bwd",
    )(last, xbc, xbc, xbc, dt_raw, y1, proj, dtb, alog, eb, ebt, dskip, norm_w, w_out)


def _merge_body(topz, botz, x_ref, ossd_ref, gate_ref, scb_ref, scc_ref, scx_ref, ccp_ref, cxp_ref, ccn_ref, cxn_ref,
                cw_ref, wsc_ref, gb_ref, wo_ref, nw_ref, rw_ref, x1_ref, h2_ref, pt_ref, ext):
    i = pl.program_id(0)
    tm = x_ref.shape[0]
    halo = SUBLANES
    pad = SC_CONV // 2
    ext[0:halo, :] = jnp.where(topz[i] == 1, 0.0, ccp_ref[...].astype(F32) * cxp_ref[...].astype(F32))
    ext[halo:halo + tm, :] = scc_ref[...].astype(F32) * scx_ref[...].astype(F32)
    ext[halo + tm:2 * halo + tm, :] = jnp.where(botz[i] == 1, 0.0, ccn_ref[...].astype(F32) * cxn_ref[...].astype(F32))
    v = ext[halo - pad:halo - pad + tm, :] * cw_ref[0:1, :]
    for k in range(1, SC_CONV):
        v = v + ext[halo - pad + k:halo - pad + k + tm, :] * cw_ref[k:k + 1, :]
    o_sc = _dot((scb_ref[...].astype(F32) * v).astype(BF16), wsc_ref[...])
    g = _sigmoid(gate_ref[...].astype(F32) + gb_ref[...])
    mix = g[:, :D_MODEL] * ossd_ref[...] + g[:, D_MODEL:] * o_sc
    x1 = x_ref[...] + _dot(mix.astype(BF16), wo_ref[...])
    x1_ref[...] = x1
    h2 = x1 * lax.rsqrt(jnp.mean(x1 * x1, axis=-1, keepdims=True) + EPS) * nw_ref[...]
    h2_ref[:, :D_MODEL] = h2.astype(BF16)
    hh, hm, _ = _split3(h2)
    rw = rw_ref[...]
    rh, rm, _ = _split3(rw)
    logits = _dot_nt(rh, hh) + _dot_nt(rh, hm) + _dot_nt(rm, hh)
    mx = jnp.max(logits, axis=0, keepdims=True)
    ex = jnp.exp(logits - mx)
    probs = ex / jnp.sum(ex, axis=0, keepdims=True)
    pt_ref[...] = probs
    pk = jnp.concatenate([probs, jnp.zeros((LANES - N_EXPERTS, tm), F32)], axis=0).T
    hi = pk.astype(BF16).astype(F32)
    mid = (pk - hi).astype(BF16).astype(F32)
    lo = pk - hi - mid
    lane = lax.broadcasted_iota(I32, (tm, LANES), 1)
    terms = jnp.where(lane < GATE_STRIDE, hi,
                      jnp.where(lane < 2 * GATE_STRIDE, pltpu.roll(mid, GATE_STRIDE, 1),
                                jnp.where(lane < 3 * GATE_STRIDE, pltpu.roll(lo, 2 * GATE_STRIDE, 1), 0.0)))
    h2_ref[:, D_MODEL:] = terms.astype(BF16)


def _merge(x, o_ssd, proj, sc_w, w_sc, gate_b, w_o, norm2, router_t, layer, st, flags):
    t, tm = st.total, st.tq
    rpt = tm // SUBLANES
    last_rb = t // SUBLANES - 1
    cscb, cscc, cscx = COL_SCB // CB, COL_SCB // CB + 1, COL_SCB // CB + 2
    prev = lambda i: jnp.maximum(i * rpt - 1, 0)
    nxt = lambda i: jnp.minimum((i + 1) * rpt, last_rb)
    grid_spec = pltpu.PrefetchScalarGridSpec(
        num_scalar_prefetch=2,
        grid=(t // tm,),
        in_specs=[
            pl.BlockSpec((tm, D_MODEL), lambda i, a, b: (i, 0)),
            pl.BlockSpec((tm, D_MODEL), lambda i, a, b: (i, 0)),
            pl.BlockSpec((tm, 2 * D_MODEL), lambda i, a, b: (i, COL_GATE // (2 * D_MODEL))),
            pl.BlockSpec((tm, CB), lambda i, a, b: (i, cscb)),
            pl.BlockSpec((tm, CB), lambda i, a, b: (i, cscc)),
            pl.BlockSpec((tm, CB), lambda i, a, b: (i, cscx)),
            pl.BlockSpec((SUBLANES, CB), lambda i, a, b: (prev(i), cscc)),
            pl.BlockSpec((SUBLANES, CB), lambda i, a, b: (prev(i), cscx)),
            pl.BlockSpec((SUBLANES, CB), lambda i, a, b: (nxt(i), cscc)),
            pl.BlockSpec((SUBLANES, CB), lambda i, a, b: (nxt(i), cscx)),
            pl.BlockSpec((None, SC_CONV, SC_WIDTH), lambda i, a, b: (layer, 0, 0)),
            pl.BlockSpec((None, SC_WIDTH, D_MODEL), lambda i, a, b: (layer, 0, 0)),
            pl.BlockSpec((None, 1, 2 * D_MODEL), lambda i, a, b: (layer, 0, 0)),
            pl.BlockSpec((None, D_MODEL, D_MODEL), lambda i, a, b: (layer, 0, 0)),
            pl.BlockSpec((None, 1, D_MODEL), lambda i, a, b: (layer, 0, 0)),
            pl.BlockSpec((None, N_EXPERTS, D_MODEL), lambda i, a, b: (layer, 0, 0)),
        ],
        out_specs=[
            pl.BlockSpec((tm, D_MODEL), lambda i, a, b: (i, 0)),
            pl.BlockSpec((tm, XW), lambda i, a, b: (i, 0)),
            pl.BlockSpec((N_EXPERTS, tm), lambda i, a, b: (0, i)),
        ],
        scratch_shapes=[pltpu.VMEM((tm + 2 * SUBLANES, SC_WIDTH), F32)],
    )
    return pl.pallas_call(
        _merge_body,
        grid_spec=grid_spec,
        out_shape=[jax.ShapeDtypeStruct((t, D_MODEL), F32), jax.ShapeDtypeStruct((t, XW), BF16),
                   jax.ShapeDtypeStruct((N_EXPERTS, t), F32)],
        compiler_params=_cparams(("arbitrary",)),
        name="merge",
    )(flags[0], flags[1], x, o_ssd, proj, proj, proj, proj, proj, proj, proj, proj,
      sc_w, w_sc, gate_b, w_o, norm2, router_t)


def _route_body(p_ref, dest_ref, posx_ref, *, cap):
    rows = p_ref.shape[1]
    r = lax.broadcasted_iota(I32, (LANES, LANES), 0)
    c = lax.broadcasted_iota(I32, (LANES, LANES), 1)
    upper = jnp.where(r <= c, 1.0, 0.0).astype(BF16)
    last_col = jnp.where(r == LANES - 1, 1.0, 0.0).astype(BF16)
    rr = lax.broadcasted_iota(I32, (rows, rows), 0)
    rc = lax.broadcasted_iota(I32, (rows, rows), 1)
    below = jnp.where(rc < rr, 1.0, 0.0).astype(BF16)

    def count(mask):
        s = jnp.sum(jnp.where(mask, 1.0, 0.0), axis=0, keepdims=True)
        return jnp.sum(s, axis=1, keepdims=True)

    def excl_prefix(mask):
        x = jnp.where(mask, 1.0, 0.0)
        incl = _dot(x.astype(BF16), upper)
        row_tot = _dot(incl.astype(BF16), last_col)
        row_off = _dot(below, row_tot.astype(BF16))
        return incl + row_off - x

    def search(i, prefix):
        bit = jnp.left_shift(jnp.int32(1), 30 - i)
        out = []
        for e in range(N_EXPERTS):
            keys = pltpu.bitcast(p_ref[e], I32)
            cand = prefix[e] | bit
            out.append(jnp.where(count(keys >= cand) >= cap, cand, prefix[e]))
        return tuple(out)

    thr = lax.fori_loop(0, 31, search, tuple(jnp.zeros((1, 1), I32) for _ in range(N_EXPERTS)))
    for e in range(N_EXPERTS):
        keys = pltpu.bitcast(p_ref[e], I32)
        gt = keys > thr[e]
        eq = keys == thr[e]
        need = cap - count(gt)
        sel = gt | (eq & (excl_prefix(eq) < need))
        pos = excl_prefix(sel).astype(I32)
        posx_ref[e] = pos
        dest_ref[e] = jnp.where(sel, pos, -1)


def _route(probs3, cap):
    e, rows, _ = probs3.shape
    return pl.pallas_call(
        functools.partial(_route_body, cap=cap),
        out_shape=[jax.ShapeDtypeStruct((e, rows, LANES), I32), jax.ShapeDtypeStruct((e, rows, LANES), I32)],
        compiler_params=pltpu.CompilerParams(vmem_limit_bytes=VMEM_LIMIT),
        name="route",
    )(probs3)


def _block_counts(posx_list, st):
    rb = st.tk // LANES
    c0s, c1s = [], []
    for g, posx in enumerate(posx_list):
        c0 = posx[:, ::rb, 0]
        c1 = jnp.concatenate([c0[:, 1:], jnp.full((N_EXPERTS, 1), st.cap_g[g], I32)], axis=1)
        c0s.append(c0.T.reshape(-1))
        c1s.append(c1.T.reshape(-1))
    return jnp.concatenate(c0s), jnp.concatenate(c1s)


def _pass_window(c0_s, c1_s, rs_s, kb, e, k):
    c0 = c0_s[kb * N_EXPERTS + e]
    c1 = c1_s[kb * N_EXPERTS + e]
    lo = jnp.minimum(c0 + k * PASS, c1)
    base = (lo // BF16_ROWS) * BF16_ROWS
    row = pl.multiple_of(rs_s[kb * N_EXPERTS + e] + base, BF16_ROWS)
    return lo, base, row, jnp.minimum(lo + PASS, c1)


def _num_passes(c0_s, c1_s, kb):
    most = c1_s[kb * N_EXPERTS] - c0_s[kb * N_EXPERTS]
    for e in range(1, N_EXPERTS):
        most = jnp.maximum(most, c1_s[kb * N_EXPERTS + e] - c0_s[kb * N_EXPERTS + e])
    return (most + PASS - 1) // PASS


def _window_one_hot(c0_s, c1_s, rs_s, dest_ref, kb, k):
    tk = dest_ref.shape[1]
    slot = lax.broadcasted_iota(I32, (WIN, tk), 0)
    parts = []
    for e in range(N_EXPERTS):
        lo, base, _, _ = _pass_window(c0_s, c1_s, rs_s, kb, e, k)
        d = dest_ref[e:e + 1, :]
        rel = jnp.where(d >= lo, jnp.where(d < lo + PASS, d - base, -1), -1)
        parts.append(jnp.where(rel == slot, 1.0, 0.0).astype(BF16))
    return jnp.concatenate(parts, axis=0)


def _dispatch_body(c0_s, c1_s, rs_s, gfirst_s, cap_s, dest_ref, h_ref, xe_hbm, stage, carry, zeros, sem, cnt):
    kb = pl.program_id(0)

    def window_copy(slot, e, row):
        return pltpu.make_async_copy(stage.at[slot, pl.ds(e * WIN, WIN), :], xe_hbm.at[pl.ds(row, WIN), :], sem.at[0])

    def wait_all():
        for e in range(N_EXPERTS):
            window_copy(0, e, 0).wait()

    def pad_copy(e):
        row = pl.multiple_of(rs_s[kb * N_EXPERTS + e] + cap_s[kb], BF16_ROWS)
        return pltpu.make_async_copy(zeros, xe_hbm.at[pl.ds(row, zeros.shape[0]), :], sem.at[1])

    @pl.when(kb == 0)
    def _():
        cnt[0] = 0

    @pl.when(gfirst_s[kb] == 1)
    def _():
        carry[...] = jnp.zeros_like(carry)
        zeros[...] = jnp.zeros_like(zeros)
        for e in range(N_EXPERTS):
            pad_copy(e).start()
        for e in range(N_EXPERTS):
            pad_copy(e).wait()

    def one_pass(k, _):
        slot = cnt[0] % 2
        packed = _dot(_window_one_hot(c0_s, c1_s, rs_s, dest_ref, kb, k), h_ref[...])
        rows = []
        for e in range(N_EXPERTS):
            lo, base, row, nxt = _pass_window(c0_s, c1_s, rs_s, kb, e, k)
            head = packed[e * WIN:e * WIN + BF16_ROWS, :] + carry[e].astype(F32)
            stage[slot, e * WIN:e * WIN + BF16_ROWS, :] = head.astype(BF16)
            stage[slot, e * WIN + BF16_ROWS:(e + 1) * WIN, :] = packed[e * WIN + BF16_ROWS:(e + 1) * WIN, :].astype(BF16)
            off = pl.multiple_of(e * WIN + (nxt // BF16_ROWS) * BF16_ROWS - base, BF16_ROWS)
            carry[e] = stage[slot, pl.ds(off, BF16_ROWS), :]
            rows.append(row)

        @pl.when(cnt[0] > 0)
        def _():
            wait_all()

        for e in range(N_EXPERTS):
            window_copy(slot, e, rows[e]).start()
        cnt[0] = cnt[0] + 1
        return 0

    lax.fori_loop(0, _num_passes(c0_s, c1_s, kb), one_pass, 0)

    @pl.when((kb == pl.num_programs(0) - 1) & (cnt[0] > 0))
    def _():
        wait_all()


def _dispatch(c0, c1, rs, gfirst, cap, dest, h2a, st):
    tk = st.tk
    grid_spec = pltpu.PrefetchScalarGridSpec(
        num_scalar_prefetch=5,
        grid=(st.total // tk,),
        in_specs=[
            pl.BlockSpec((N_EXPERTS, tk), lambda i, *_: (0, i)),
            pl.BlockSpec((tk, XW), lambda i, *_: (i, 0)),
        ],
        out_specs=pl.BlockSpec(memory_space=pl.ANY),
        scratch_shapes=[
            pltpu.VMEM((2, N_EXPERTS * WIN, XW), BF16),
            pltpu.VMEM((N_EXPERTS, BF16_ROWS, XW), BF16),
            pltpu.VMEM((st.ts, XW), BF16),
            pltpu.SemaphoreType.DMA((2,)),
            pltpu.SMEM((1,), I32),
        ],
    )
    return pl.pallas_call(
        _dispatch_body,
        grid_spec=grid_spec,
        out_shape=jax.ShapeDtypeStruct((st.slot_rows, XW), BF16),
        compiler_params=_cparams(("arbitrary",)),
        name="moe_dispatch",
    )(c0, c1, rs, gfirst, cap, dest, h2a)


def _moe_ffn_body(pad_s, src_s, xe_ref, wg_ref, wu_ref, wd_ref, o_ref, *, tiles_per_expert):
    i = pl.program_id(0)

    @pl.when(pad_s[i] == 1)
    def _():
        o_ref[...] = jnp.zeros_like(o_ref)

    @pl.when(pad_s[i] == 0)
    def _():
        e = i // tiles_per_expert
        xe = xe_ref[:, :D_MODEL]
        terms = xe_ref[:, D_MODEL:].astype(F32)
        lane = lax.broadcasted_iota(I32, terms.shape, 1)
        mine = (lane < 3 * GATE_STRIDE) & (lane % GATE_STRIDE == e)
        gate = jnp.sum(jnp.where(mine, terms, 0.0), axis=1, keepdims=True)
        a = _dot(xe, wg_ref[...])
        u = _dot(xe, wu_ref[...])
        act = (a * _sigmoid(a) * u).astype(BF16)
        o_ref[...] = (_dot(act, wd_ref[...]) * gate).astype(BF16)


def _moe_ffn(xe, wg, wu, wd, layer, st):
    ts = st.ts
    tpe = st.tiles_per_expert
    pad, src = st.tile_tables()
    grid_spec = pltpu.PrefetchScalarGridSpec(
        num_scalar_prefetch=2,
        grid=(N_EXPERTS * tpe,),
        in_specs=[
            pl.BlockSpec((ts, XW), lambda i, p, s: (s[i], 0)),
            pl.BlockSpec((None, None, D_MODEL, EXPERT_FF), lambda i, p, s: (layer, i // tpe, 0, 0)),
            pl.BlockSpec((None, None, D_MODEL, EXPERT_FF), lambda i, p, s: (layer, i // tpe, 0, 0)),
            pl.BlockSpec((None, None, EXPERT_FF, D_MODEL), lambda i, p, s: (layer, i // tpe, 0, 0)),
        ],
        out_specs=pl.BlockSpec((ts, D_MODEL), lambda i, p, s: (i, 0)),
    )
    return pl.pallas_call(
        functools.partial(_moe_ffn_body, tiles_per_expert=tpe),
        grid_spec=grid_spec,
        out_shape=jax.ShapeDtypeStruct((st.slot_rows, D_MODEL), BF16),
        compiler_params=_cparams(("arbitrary",)),
        name="moe_ffn",
    )(pad, src, xe, wg, wu, wd)


OVERFLOW_SLOT = 2


def _combine_body(c0_s, c1_s, rs_s, dest_ref, x_ref, ye_hbm, o_ref, buf, sem):
    kb = pl.program_id(0)
    nkb = pl.num_programs(0)

    def window_copy(slot, e, row):
        return pltpu.make_async_copy(ye_hbm.at[pl.ds(row, WIN), :], buf.at[slot, pl.ds(e * WIN, WIN), :], sem.at[slot])

    def fetch(block, k, slot):
        for e in range(N_EXPERTS):
            window_copy(slot, e, _pass_window(c0_s, c1_s, rs_s, block, e, k)[2]).start()

    def wait_all(slot):
        for e in range(N_EXPERTS):
            window_copy(slot, e, 0).wait()

    @pl.when(kb == 0)
    def _():
        fetch(0, 0, 0)

    slot = kb % 2
    wait_all(slot)

    @pl.when(kb + 1 < nkb)
    def _():
        fetch(kb + 1, 0, 1 - slot)

    o_ref[...] = x_ref[...] + _dot_tn(_window_one_hot(c0_s, c1_s, rs_s, dest_ref, kb, 0), buf[slot])

    def extra_pass(k, _):
        fetch(kb, k, OVERFLOW_SLOT)
        wait_all(OVERFLOW_SLOT)
        o_ref[...] += _dot_tn(_window_one_hot(c0_s, c1_s, rs_s, dest_ref, kb, k), buf[OVERFLOW_SLOT])
        return 0

    lax.fori_loop(1, _num_passes(c0_s, c1_s, kb), extra_pass, 0)


def _combine(c0, c1, rs, dest, x1, ye, st):
    tk = st.tk
    grid_spec = pltpu.PrefetchScalarGridSpec(
        num_scalar_prefetch=3,
        grid=(st.total // tk,),
        in_specs=[
            pl.BlockSpec((N_EXPERTS, tk), lambda i, *_: (0, i)),
            pl.BlockSpec((tk, D_MODEL), lambda i, *_: (i, 0)),
            pl.BlockSpec(memory_space=pl.ANY),
        ],
        out_specs=pl.BlockSpec((tk, D_MODEL), lambda i, *_: (i, 0)),
        scratch_shapes=[
            pltpu.VMEM((3, N_EXPERTS * WIN, D_MODEL), BF16),
            pltpu.SemaphoreType.DMA((3,)),
        ],
    )
    return pl.pallas_call(
        _combine_body,
        grid_spec=grid_spec,
        out_shape=jax.ShapeDtypeStruct((st.total, D_MODEL), F32),
        compiler_params=_cparams(("arbitrary",)),
        name="moe_combine",
    )(c0, c1, rs, dest, x1, ye)


def _final_norm_body(x_ref, w_ref, o_ref):
    x = x_ref[...]
    o_ref[...] = x * lax.rsqrt(jnp.mean(x * x, axis=-1, keepdims=True) + EPS) * w_ref[...]


def _final_norm(x, w, st):
    t, tm = st.total, st.tm
    return pl.pallas_call(
        _final_norm_body,
        grid=(t // tm,),
        in_specs=[pl.BlockSpec((tm, D_MODEL), lambda i: (i, 0)), pl.BlockSpec((1, D_MODEL), lambda i: (0, 0))],
        out_specs=pl.BlockSpec((tm, D_MODEL), lambda i: (i, 0)),
        out_shape=jax.ShapeDtypeStruct((t, D_MODEL), F32),
        compiler_params=_cparams(("arbitrary",)),
        name="final_norm",
    )(x, w)


def _pad_lanes(v, width=LANES):
    return jnp.pad(v, [(0, 0)] * (v.ndim - 1) + [(0, width - v.shape[-1])])


def _trunk_stream(xs, params, tm, tq, tk, ts):
    (norm1_w, w_in, ssd_conv_w, ssd_conv_b, ssd_dt_bias, ssd_a_log, ssd_d, ssd_norm_w, w_ssd_out, sc_conv_w,
     w_sc_out, gate_b, w_o, norm2_w, router_w, w_gate, w_up, w_down, final_norm_w) = params
    depth = w_in.shape[0]
    st = _Stream([(x.shape[0], x.shape[1]) for x in xs], tm, tq, tk, ts)
    x = jnp.concatenate([v.reshape(-1, D_MODEL) for v in xs], axis=0)

    w_main = jnp.concatenate([w_in[:, :, :_OFF_DT], w_in[:, :, _OFF_GATE:], w_in[:, :, _OFF_SCB:_OFF_GATE]],
                             axis=2).astype(BF16)
    w_dt = _pad_lanes(w_in[:, :, _OFF_DT:_OFF_SCB]).astype(BF16)
    dtb = _pad_lanes(ssd_dt_bias.reshape(depth, 1, 2 * HEADS))
    alog = _pad_lanes(ssd_a_log.reshape(depth, 1, 2 * HEADS))
    dskip = jnp.repeat(ssd_d, HEAD_DIM, axis=-1).reshape(depth, 1, INNER)
    ef, eb, eft, ebt = _expansion_matrices()
    router_t = jnp.swapaxes(router_w, 1, 2)
    wg, wu, wd = w_gate.astype(BF16), w_up.astype(BF16), w_down.astype(BF16)
    row = lambda v: v.reshape(depth, 1, v.shape[-1])

    chunk_first, chunk_last = st.seq_flags(CHUNK)
    tile_flags = st.seq_flags(tq)
    region_start, group_first, group_cap = st.block_tables()

    for l in range(depth):
        proj, dt_raw = _inproj(x, row(norm1_w), w_main, w_dt, l, st)
        xbc = _conv_xbc(proj, ssd_conv_w, row(ssd_conv_b), l, st, tile_flags)
        y1 = _ssd_fwd(xbc, dt_raw, dtb, alog, ef, eft, l, st, chunk_first)
        o_ssd = _ssd_bwd(xbc, dt_raw, y1, proj, dtb, alog, eb, ebt, dskip, row(ssd_norm_w),
                         w_ssd_out.astype(BF16), l, st, chunk_last)
        x1, h2, probs_t = _merge(x, o_ssd, proj, sc_conv_w, w_sc_out.astype(BF16), row(gate_b), w_o.astype(BF16),
                                 row(norm2_w), router_t, l, st, tile_flags)
        dests, posxs = [], []
        for g in range(len(xs)):
            pg = probs_t[:, st.off_g[g]:st.off_g[g] + st.t_g[g]].reshape(N_EXPERTS, st.t_g[g] // LANES, LANES)
            dest, posx = _route(pg, st.cap_g[g])
            dests.append(dest.reshape(N_EXPERTS, st.t_g[g]))
            posxs.append(posx)
        dest = jnp.concatenate(dests, axis=1)
        c0, c1 = _block_counts(posxs, st)
        xe = _dispatch(c0, c1, region_start, group_first, group_cap, dest, h2, st)
        ye = _moe_ffn(xe, wg, wu, wd, l, st)
        x = _combine(c0, c1, region_start, dest, x1, ye, st)

    y = _final_norm(x, final_norm_w.reshape(1, D_MODEL), st)
    return [y[o:o + t].reshape(v.shape) for o, t, v in zip(st.off_g, st.t_g, xs)]


def kernel(x_prompt, x_sample, norm1_w, w_in, ssd_conv_w, ssd_conv_b, ssd_dt_bias, ssd_a_log, ssd_d, ssd_norm_w, w_ssd_out, sc_conv_w, w_sc_out, gate_b, w_o, norm2_w, router_w, w_gate, w_up, w_down, final_norm_w):
    params = (norm1_w, w_in, ssd_conv_w, ssd_conv_b, ssd_dt_bias, ssd_a_log, ssd_d, ssd_norm_w, w_ssd_out, sc_conv_w,
              w_sc_out, gate_b, w_o, norm2_w, router_w, w_gate, w_up, w_down, final_norm_w)
    y_prompt, y_sample = _trunk_stream([x_prompt, x_sample], params, tm=1024, tq=512, tk=256, ts=256)
    return (y_prompt, y_sample)
```

```python
import functools

import numpy as np
import jax
import jax.numpy as jnp
from jax import lax
from jax.experimental import pallas as pl
from jax.experimental.pallas import tpu as pltpu

F32 = jnp.float32
BF16 = jnp.bfloat16
I32 = jnp.int32

D_MODEL = 1024
HEADS = 24
HEAD_DIM = 64
INNER = HEADS * HEAD_DIM
GROUPS = 4
GROUP_W = INNER // GROUPS
STATE = 128
CHUNK = 128
SSD_CONV = 5
SC_WIDTH = 512
SC_CONV = 3
N_EXPERTS = 16
EXPERT_FF = 2048
CAPACITY_FACTOR = 2
EPS = 1e-6
LANES = 128
SUBLANES = 8

_OFF_XBC = INNER
_OFF_DT = _OFF_XBC + INNER + 2 * GROUPS * STATE
_OFF_SCB = _OFF_DT + 2 * HEADS
_OFF_GATE = _OFF_SCB + 3 * SC_WIDTH
_N_IN = _OFF_GATE + 2 * D_MODEL
COL_X = INNER
COL_B = 2 * INNER
COL_C = COL_B + GROUPS * STATE
COL_GATE = COL_C + GROUPS * STATE
COL_SCB = COL_GATE + 2 * D_MODEL
N_MAIN = COL_SCB + 3 * SC_WIDTH
CB = 512

BF16_ROWS = 2 * SUBLANES
PASS = 64
WIN = PASS + BF16_ROWS
XW = D_MODEL + LANES
GATE_STRIDE = N_EXPERTS

VMEM_LIMIT = 50 * 1024 * 1024


def _cparams(sem):
    return pltpu.CompilerParams(dimension_semantics=sem, vmem_limit_bytes=VMEM_LIMIT)


def _sigmoid(v):
    return 1.0 / (1.0 + jnp.exp(-v))


def _split3(v):
    hi = v.astype(BF16)
    r = v - hi.astype(F32)
    mid = r.astype(BF16)
    lo = (r - mid.astype(F32)).astype(BF16)
    return hi, mid, lo


def _dot(a, b):
    return jnp.dot(a, b, preferred_element_type=F32)


def _dot_nt(a, b):
    return lax.dot_general(a, b, (((1,), (1,)), ((), ())), preferred_element_type=F32)


def _dot_tn(a, b):
    return lax.dot_general(a, b, (((0,), (0,)), ((), ())), preferred_element_type=F32)


def _dot3_l(a_f32, b_exact):
    hi, mid, lo = _split3(a_f32)
    return _dot(hi, b_exact) + _dot(mid, b_exact) + _dot(lo, b_exact)


def _dot3_r(a_exact, b_f32):
    hi, mid, lo = _split3(b_f32)
    return _dot(a_exact, hi) + _dot(a_exact, mid) + _dot(a_exact, lo)


class _Stream:
    def __init__(self, groups, tm, tq, tk, ts, tc, cps):
        self.groups = tuple(groups)
        self.tm, self.tq, self.tk, self.ts, self.tc, self.cps = tm, tq, tk, ts, tc, cps
        self.t_g = [n * l for n, l in self.groups]
        self.off_g = [int(v) for v in np.cumsum([0] + self.t_g[:-1])]
        self.total = int(sum(self.t_g))
        self.cap_g = [CAPACITY_FACTOR * t // N_EXPERTS for t in self.t_g]
        self.nkb_g = [t // tk for t in self.t_g]
        for (n, l), t, c in zip(self.groups, self.t_g, self.cap_g):
            assert l % (CHUNK * cps) == 0 and l % tq == 0 and l % tc == 0 and t % tk == 0 and c % ts == 0
            assert t % tm == 0 and tk % LANES == 0 and tc % CONV_ROWS == 0
        assert ts >= WIN
        self.region_g = [c + ts for c in self.cap_g]
        self.rows_per_expert = int(sum(self.region_g))
        self.tiles_per_expert = self.rows_per_expert // ts
        self.slot_rows = N_EXPERTS * self.rows_per_expert
        self.region_off_g = [int(v) for v in np.cumsum([0] + self.region_g[:-1])]

    def block_tables(self):
        rs, gfirst, cap = [], [], []
        for g, nkb in enumerate(self.nkb_g):
            for kb in range(nkb):
                gfirst.append(1 if kb == 0 else 0)
                cap.append(self.cap_g[g])
                rs.extend(e * self.rows_per_expert + self.region_off_g[g] for e in range(N_EXPERTS))
        return tuple(jnp.asarray(np.asarray(v, np.int32)) for v in (rs, gfirst, cap))

    def tile_tables(self):
        pad, src = [], []
        for e in range(N_EXPERTS):
            for g, c in enumerate(self.cap_g):
                n = c // self.ts
                base = len(pad)
                pad.extend([0] * n + [1])
                src.extend(list(range(base, base + n)) + [base + n - 1])
        return jnp.asarray(np.asarray(pad, np.int32)), jnp.asarray(np.asarray(src, np.int32))

    def seq_flags(self, tile):
        n_tiles = self.total // tile
        first = np.zeros((n_tiles,), np.int32)
        last = np.zeros((n_tiles,), np.int32)
        for (n, l), off in zip(self.groups, self.off_g):
            for s in range(n):
                first[(off + s * l) // tile] = 1
                last[(off + (s + 1) * l) // tile - 1] = 1
        return jnp.asarray(first), jnp.asarray(last)


def _inproj_body(x_ref, nw_ref, w_ref, wdt_ref, proj_ref, dt_ref, h_scr):
    @pl.when(pl.program_id(1) == 0)
    def _():
        x = x_ref[...]
        ms = jnp.mean(x * x, axis=-1, keepdims=True)
        h = (x * lax.rsqrt(ms + EPS) * nw_ref[...]).astype(BF16)
        h_scr[...] = h
        dt_ref[...] = _dot(h, wdt_ref[...])

    proj_ref[...] = _dot(h_scr[...], w_ref[...]).astype(BF16)


def _inproj(x, nw, w_main, w_dt, layer, st):
    t, tm = st.total, st.tm
    tn = N_MAIN // 3
    return pl.pallas_call(
        _inproj_body,
        grid=(t // tm, N_MAIN // tn),
        in_specs=[
            pl.BlockSpec((tm, D_MODEL), lambda i, n: (i, 0)),
            pl.BlockSpec((None, 1, D_MODEL), lambda i, n: (layer, 0, 0)),
            pl.BlockSpec((None, D_MODEL, tn), lambda i, n: (layer, 0, n)),
            pl.BlockSpec((None, D_MODEL, LANES), lambda i, n: (layer, 0, 0)),
        ],
        out_specs=[
            pl.BlockSpec((tm, tn), lambda i, n: (i, n)),
            pl.BlockSpec((tm, LANES), lambda i, n: (i, 0)),
        ],
        out_shape=[jax.ShapeDtypeStruct((t, N_MAIN), BF16), jax.ShapeDtypeStruct((t, LANES), F32)],
        scratch_shapes=[pltpu.VMEM((tm, D_MODEL), BF16)],
        compiler_params=_cparams(("arbitrary", "arbitrary")),
        name="inproj",
    )(x, nw, w_main, w_dt)


CONV_ROWS = 128


def _conv_body(topz, botz, main_ref, prev_ref, next_ref, w_ref, b_ref, o_ref, ext):
    i = pl.program_id(0)
    tq = main_ref.shape[0]
    halo = BF16_ROWS
    pad = SSD_CONV // 2
    zero = jnp.zeros_like(prev_ref[...])
    ext[0:halo, :] = jnp.where(topz[i] == 1, zero, prev_ref[...])
    ext[halo:halo + tq, :] = main_ref[...]
    ext[halo + tq:2 * halo + tq, :] = jnp.where(botz[i] == 1, zero, next_ref[...])
    win = CONV_ROWS + 2 * halo
    taps = [k for k in range(SSD_CONV) if k != pad]
    r = lax.broadcasted_iota(I32, (CONV_ROWS, win), 0)
    c = lax.broadcasted_iota(I32, (CONV_ROWS, win), 1)
    shift = jnp.concatenate([jnp.where(c == r + halo + k - pad, 1.0, 0.0).astype(BF16) for k in taps], axis=0)
    for sb in range(tq // CONV_ROWS):
        r0 = sb * CONV_ROWS
        moved = _dot(shift, ext[r0:r0 + win, :])
        acc = b_ref[...] + ext[halo + r0:halo + r0 + CONV_ROWS, :].astype(F32) * w_ref[pad:pad + 1, :]
        for n, k in enumerate(taps):
            acc = acc + moved[n * CONV_ROWS:(n + 1) * CONV_ROWS, :] * w_ref[k:k + 1, :]
        o_ref[r0:r0 + CONV_ROWS, :] = (acc * _sigmoid(acc)).astype(BF16)


def _conv_xbc(proj, conv_w, conv_b, layer, st, flags):
    t, tq = st.total, st.tc
    ncb = (COL_GATE - COL_X) // CB
    cb0 = COL_X // CB
    rpt = tq // BF16_ROWS
    last_rb = t // BF16_ROWS - 1
    grid_spec = pltpu.PrefetchScalarGridSpec(
        num_scalar_prefetch=2,
        grid=(t // tq, ncb),
        in_specs=[
            pl.BlockSpec((tq, CB), lambda i, j, a, b: (i, cb0 + j)),
            pl.BlockSpec((BF16_ROWS, CB), lambda i, j, a, b: (jnp.maximum(i * rpt - 1, 0), cb0 + j)),
            pl.BlockSpec((BF16_ROWS, CB), lambda i, j, a, b: (jnp.minimum((i + 1) * rpt, last_rb), cb0 + j)),
            pl.BlockSpec((None, SSD_CONV, CB), lambda i, j, a, b: (layer, 0, j)),
            pl.BlockSpec((None, 1, CB), lambda i, j, a, b: (layer, 0, j)),
        ],
        out_specs=pl.BlockSpec((tq, CB), lambda i, j, a, b: (i, j)),
        scratch_shapes=[pltpu.VMEM((tq + 2 * BF16_ROWS, CB), BF16)],
    )
    return pl.pallas_call(
        _conv_body,
        grid_spec=grid_spec,
        out_shape=jax.ShapeDtypeStruct((t, COL_GATE - COL_X), BF16),
        compiler_params=_cparams(("arbitrary", "arbitrary")),
        name="conv_xbc",
    )(flags[0], flags[1], proj, proj, proj, conv_w, conv_b)


def _ssd_scalars(dt_ref, rows, dtb_ref, alog_ref):
    v = dt_ref[rows, :] + dtb_ref[...]
    u = jnp.exp(-jnp.abs(v))
    w = 1.0 + u
    dt = jnp.maximum(v, 0.0) + jnp.where(w == 1.0, u, jnp.log(w) * (u / (w - 1.0)))
    da = dt * (-jnp.exp(alog_ref[...]))
    r = lax.broadcasted_iota(I32, (CHUNK, CHUNK), 0)
    c = lax.broadcasted_iota(I32, (CHUNK, CHUNK), 1)
    tri = jnp.where(c <= r, 1.0, 0.0).astype(BF16)
    upper = jnp.where(r <= c, 1.0, 0.0).astype(BF16)
    cs = _dot3_r(tri, da)
    da_t = da.T
    dt_t = dt.T
    cs_t = _dot3_l(da_t, upper)
    return dt, cs, cs - da, dt_t, cs_t, cs_t - da_t, r, c


def _expand(v, e_ref):
    hi, mid, lo = _split3(v)
    e = e_ref[...]
    return _dot(hi, e) + _dot(mid, e) + _dot(lo, e)


def _state_step(s_scr, xs_f32, w_tok, dec_col, b_ref, rows, et_ref):
    xw = (xs_f32 * w_tok).astype(BF16)
    dec = jnp.broadcast_to(dec_col, (CHUNK, STATE))
    dec_rows = _dot3_r(et_ref[...], dec)
    for g in range(GROUPS):
        sl = slice(g * GROUP_W, (g + 1) * GROUP_W)
        sc = _dot_tn(xw[:, sl], b_ref[rows, g * STATE:(g + 1) * STATE])
        s_scr[sl, :] = s_scr[sl, :] * dec_rows[sl, :] + sc


def _ssd_fwd_body(first, xs_ref, b_ref, c_ref, dt_ref, dtb_ref, alog_ref, ef_ref, et_ref, y_ref, s_scr):
    cps = xs_ref.shape[0] // CHUNK

    @pl.when(pl.program_id(0) == 0)
    def _():
        s_scr[...] = jnp.zeros_like(s_scr)

    for sub in range(cps):
        rows = slice(sub * CHUNK, (sub + 1) * CHUNK)
        keep = jnp.where(first[pl.program_id(0) * cps + sub] == 1, 0.0, 1.0)
        dt, cs, ex, dt_t, cs_t, ex_t, r, c = _ssd_scalars(dt_ref, rows, dtb_ref, alog_ref)
        w_out = _expand(jnp.exp(cs) * keep, ef_ref)
        for g in range(GROUPS):
            sl = slice(g * GROUP_W, (g + 1) * GROUP_W)
            yo = _dot_nt(c_ref[rows, g * STATE:(g + 1) * STATE], s_scr[sl, :].astype(BF16))
            y_ref[rows, sl] = yo * w_out[:, sl]
        lane = lax.broadcasted_iota(I32, (CHUNK, 2 * HEAD_DIM), 1)
        lower = r >= c
        pairs_per_group = HEADS // GROUPS // 2
        for pi in range(HEADS // 2):
            g = pi // pairs_per_group
            if pi % pairs_per_group == 0:
                cb = _dot_nt(c_ref[rows, g * STATE:(g + 1) * STATE], b_ref[rows, g * STATE:(g + 1) * STATE])
            ms = []
            for h in (2 * pi, 2 * pi + 1):
                hb = HEADS + h
                arg = jnp.where(lower, cs[:, h:h + 1] - cs_t[h:h + 1, :], ex_t[hb:hb + 1, :] - ex[:, hb:hb + 1])
                d0 = dt_t[h:h + 1, :]
                d1 = dt_t[hb:hb + 1, :]
                w = jnp.where(r > c, d0, jnp.where(r < c, d1, d0 + d1))
                ms.append((jnp.exp(arg) * w * cb).astype(BF16))
            psl = slice(pi * 2 * HEAD_DIM, (pi + 1) * 2 * HEAD_DIM)
            xp = xs_ref[rows, psl]
            zero = jnp.zeros_like(xp)
            rhs = jnp.concatenate([jnp.where(lane < HEAD_DIM, xp, zero), jnp.where(lane >= HEAD_DIM, xp, zero)],
                                  axis=0)
            y_ref[rows, psl] = y_ref[rows, psl] + _dot(jnp.concatenate(ms, axis=1), rhs)
        w_st = _expand(jnp.exp(cs[CHUNK - 1:CHUNK, :] - cs) * dt, ef_ref)
        _state_step(s_scr, xs_ref[rows, :].astype(F32), w_st, jnp.exp(cs_t[:, CHUNK - 1:CHUNK]) * keep, b_ref, rows,
                    et_ref)


def _ssd_bwd_body(last, xs_ref, b_ref, c_ref, dt_ref, y1_ref, z_ref, dtb_ref, alog_ref, eb_ref, et_ref,
                  dskip_ref, nw_ref, wout_ref, o_ref, s_scr, yn_scr):
    cps = xs_ref.shape[0] // CHUNK
    step = pl.num_programs(0) - 1 - pl.program_id(0)

    @pl.when(pl.program_id(0) == 0)
    def _():
        s_scr[...] = jnp.zeros_like(s_scr)

    for sub in reversed(range(cps)):
        rows = slice(sub * CHUNK, (sub + 1) * CHUNK)
        keep = jnp.where(last[step * cps + sub] == 1, 0.0, 1.0)
        dt, cs, ex, dt_t, cs_t, ex_t, r, c = _ssd_scalars(dt_ref, rows, dtb_ref, alog_ref)
        tot = cs[CHUNK - 1:CHUNK, :]
        w_out = _expand(jnp.exp(tot - ex) * keep, eb_ref)
        xs = xs_ref[rows, :].astype(F32)
        z = z_ref[rows, :].astype(F32)
        zs = z * _sigmoid(z)
        for g in range(GROUPS):
            sl = slice(g * GROUP_W, (g + 1) * GROUP_W)
            yo = _dot_nt(c_ref[rows, g * STATE:(g + 1) * STATE], s_scr[sl, :].astype(BF16))
            y = (y1_ref[rows, sl] + yo * w_out[:, sl] + xs[:, sl] * dskip_ref[:, sl]) * zs[:, sl]
            y = y * lax.rsqrt(jnp.mean(y * y, axis=-1, keepdims=True) + EPS)
            yn_scr[rows, sl] = (y * nw_ref[:, sl]).astype(BF16)
        w_st = _expand(jnp.exp(ex) * dt, eb_ref)
        _state_step(s_scr, xs, w_st, jnp.exp(cs_t[:, CHUNK - 1:CHUNK]) * keep, b_ref, rows, et_ref)
    o_ref[...] = _dot(yn_scr[...], wout_ref[...])


def _expansion_matrices():
    rows = np.arange(LANES)[:, None]
    cols = np.arange(INNER)[None, :] // HEAD_DIM
    ef = (rows == cols).astype(np.float32)
    eb = (rows == cols + HEADS).astype(np.float32)
    return jnp.asarray(ef, BF16), jnp.asarray(eb, BF16), jnp.asarray(ef.T, BF16), jnp.asarray(eb.T, BF16)


def _ssd_fwd(xbc, dt_raw, dtb, alog, ef, eft, layer, st, first):
    t = st.total
    nb = GROUPS * STATE // CB
    rows = st.cps * CHUNK
    grid_spec = pltpu.PrefetchScalarGridSpec(
        num_scalar_prefetch=1,
        grid=(t // rows,),
        in_specs=[
            pl.BlockSpec((rows, INNER), lambda i, f: (i, 0)),
            pl.BlockSpec((rows, CB), lambda i, f: (i, INNER // CB)),
            pl.BlockSpec((rows, CB), lambda i, f: (i, INNER // CB + nb)),
            pl.BlockSpec((rows, LANES), lambda i, f: (i, 0)),
            pl.BlockSpec((None, 1, LANES), lambda i, f: (layer, 0, 0)),
            pl.BlockSpec((None, 1, LANES), lambda i, f: (layer, 0, 0)),
            pl.BlockSpec((LANES, INNER), lambda i, f: (0, 0)),
            pl.BlockSpec((INNER, LANES), lambda i, f: (0, 0)),
        ],
        out_specs=pl.BlockSpec((rows, INNER), lambda i, f: (i, 0)),
        scratch_shapes=[pltpu.VMEM((INNER, STATE), F32)],
    )
    return pl.pallas_call(
        _ssd_fwd_body,
        grid_spec=grid_spec,
        out_shape=jax.ShapeDtypeStruct((t, INNER), F32),
        compiler_params=_cparams(("arbitrary",)),
        name="ssd_fwd",
    )(first, xbc, xbc, xbc, dt_raw, dtb, alog, ef, eft)


def _ssd_bwd(xbc, dt_raw, y1, proj, dtb, alog, eb, ebt, dskip, norm_w, w_out, layer, st, last):
    t = st.total
    rows = st.cps * CHUNK
    nc = t // rows
    nb = GROUPS * STATE // CB
    rev = lambda i: nc - 1 - i
    grid_spec = pltpu.PrefetchScalarGridSpec(
        num_scalar_prefetch=1,
        grid=(nc,),
        in_specs=[
            pl.BlockSpec((rows, INNER), lambda i, f: (rev(i), 0)),
            pl.BlockSpec((rows, CB), lambda i, f: (rev(i), INNER // CB)),
            pl.BlockSpec((rows, CB), lambda i, f: (rev(i), INNER // CB + nb)),
            pl.BlockSpec((rows, LANES), lambda i, f: (rev(i), 0)),
            pl.BlockSpec((rows, INNER), lambda i, f: (rev(i), 0)),
            pl.BlockSpec((rows, INNER), lambda i, f: (rev(i), 0)),
            pl.BlockSpec((None, 1, LANES), lambda i, f: (layer, 0, 0)),
            pl.BlockSpec((None, 1, LANES), lambda i, f: (layer, 0, 0)),
            pl.BlockSpec((LANES, INNER), lambda i, f: (0, 0)),
            pl.BlockSpec((INNER, LANES), lambda i, f: (0, 0)),
            pl.BlockSpec((None, 1, INNER), lambda i, f: (layer, 0, 0)),
            pl.BlockSpec((None, 1, INNER), lambda i, f: (layer, 0, 0)),
            pl.BlockSpec((None, INNER, D_MODEL), lambda i, f: (layer, 0, 0)),
        ],
        out_specs=pl.BlockSpec((rows, D_MODEL), lambda i, f: (rev(i), 0)),
        scratch_shapes=[pltpu.VMEM((INNER, STATE), F32), pltpu.VMEM((rows, INNER), BF16)],
    )
    return pl.pallas_call(
        _ssd_bwd_body,
        grid_spec=grid_spec,
        out_shape=jax.ShapeDtypeStruct((t, D_MODEL), F32),
        compiler_params=_cparams(("arbitrary",)),
        name="ssd_bwd",
    )(last, xbc, xbc, xbc, dt_raw, y1, proj, dtb, alog, eb, ebt, dskip, norm_w, w_out)


def _merge_body(topz, botz, x_ref, ossd_ref, gate_ref, scb_ref, scc_ref, scx_ref, ccp_ref, cxp_ref, ccn_ref, cxn_ref,
                cw_ref, wsc_ref, gb_ref, wo_ref, nw_ref, rw_ref, x1_ref, h2_ref, pt_ref, ext):
    i = pl.program_id(0)
    tm = x_ref.shape[0]
    halo = SUBLANES
    pad = SC_CONV // 2
    ext[0:halo, :] = jnp.where(topz[i] == 1, 0.0, ccp_ref[...].astype(F32) * cxp_ref[...].astype(F32))
    ext[halo:halo + tm, :] = scc_ref[...].astype(F32) * scx_ref[...].astype(F32)
    ext[halo + tm:2 * halo + tm, :] = jnp.where(botz[i] == 1, 0.0, ccn_ref[...].astype(F32) * cxn_ref[...].astype(F32))
    v = ext[halo - pad:halo - pad + tm, :] * cw_ref[0:1, :]
    for k in range(1, SC_CONV):
        v = v + ext[halo - pad + k:halo - pad + k + tm, :] * cw_ref[k:k + 1, :]
    o_sc = _dot((scb_ref[...].astype(F32) * v).astype(BF16), wsc_ref[...])
    g = _sigmoid(gate_ref[...].astype(F32) + gb_ref[...])
    mix = g[:, :D_MODEL] * ossd_ref[...] + g[:, D_MODEL:] * o_sc
    x1 = x_ref[...] + _dot(mix.astype(BF16), wo_ref[...])
    x1_ref[...] = x1
    h2 = x1 * lax.rsqrt(jnp.mean(x1 * x1, axis=-1, keepdims=True) + EPS) * nw_ref[...]
    h2_ref[:, :D_MODEL] = h2.astype(BF16)
    hh, hm, _ = _split3(h2)
    rw = rw_ref[...]
    rh, rm, _ = _split3(rw)
    logits = _dot_nt(rh, hh) + _dot_nt(rh, hm) + _dot_nt(rm, hh)
    mx = jnp.max(logits, axis=0, keepdims=True)
    ex = jnp.exp(logits - mx)
    probs = ex / jnp.sum(ex, axis=0, keepdims=True)
    pt_ref[...] = probs
    pk = jnp.concatenate([probs, jnp.zeros((LANES - N_EXPERTS, tm), F32)], axis=0).T
    hi = pk.astype(BF16).astype(F32)
    mid = (pk - hi).astype(BF16).astype(F32)
    lo = pk - hi - mid
    lane = lax.broadcasted_iota(I32, (tm, LANES), 1)
    terms = jnp.where(lane < GATE_STRIDE, hi,
                      jnp.where(lane < 2 * GATE_STRIDE, pltpu.roll(mid, GATE_STRIDE, 1),
                                jnp.where(lane < 3 * GATE_STRIDE, pltpu.roll(lo, 2 * GATE_STRIDE, 1), 0.0)))
    h2_ref[:, D_MODEL:] = terms.astype(BF16)


def _merge(x, o_ssd, proj, sc_w, w_sc, gate_b, w_o, norm2, router_t, layer, st, flags):
    t, tm = st.total, st.tq
    rpt = tm // SUBLANES
    last_rb = t // SUBLANES - 1
    cscb, cscc, cscx = COL_SCB // CB, COL_SCB // CB + 1, COL_SCB // CB + 2
    prev = lambda i: jnp.maximum(i * rpt - 1, 0)
    nxt = lambda i: jnp.minimum((i + 1) * rpt, last_rb)
    grid_spec = pltpu.PrefetchScalarGridSpec(
        num_scalar_prefetch=2,
        grid=(t // tm,),
        in_specs=[
            pl.BlockSpec((tm, D_MODEL), lambda i, a, b: (i, 0)),
            pl.BlockSpec((tm, D_MODEL), lambda i, a, b: (i, 0)),
            pl.BlockSpec((tm, 2 * D_MODEL), lambda i, a, b: (i, COL_GATE // (2 * D_MODEL))),
            pl.BlockSpec((tm, CB), lambda i, a, b: (i, cscb)),
            pl.BlockSpec((tm, CB), lambda i, a, b: (i, cscc)),
            pl.BlockSpec((tm, CB), lambda i, a, b: (i, cscx)),
            pl.BlockSpec((SUBLANES, CB), lambda i, a, b: (prev(i), cscc)),
            pl.BlockSpec((SUBLANES, CB), lambda i, a, b: (prev(i), cscx)),
            pl.BlockSpec((SUBLANES, CB), lambda i, a, b: (nxt(i), cscc)),
            pl.BlockSpec((SUBLANES, CB), lambda i, a, b: (nxt(i), cscx)),
            pl.BlockSpec((None, SC_CONV, SC_WIDTH), lambda i, a, b: (layer, 0, 0)),
            pl.BlockSpec((None, SC_WIDTH, D_MODEL), lambda i, a, b: (layer, 0, 0)),
            pl.BlockSpec((None, 1, 2 * D_MODEL), lambda i, a, b: (layer, 0, 0)),
            pl.BlockSpec((None, D_MODEL, D_MODEL), lambda i, a, b: (layer, 0, 0)),
            pl.BlockSpec((None, 1, D_MODEL), lambda i, a, b: (layer, 0, 0)),
            pl.BlockSpec((None, N_EXPERTS, D_MODEL), lambda i, a, b: (layer, 0, 0)),
        ],
        out_specs=[
            pl.BlockSpec((tm, D_MODEL), lambda i, a, b: (i, 0)),
            pl.BlockSpec((tm, XW), lambda i, a, b: (i, 0)),
            pl.BlockSpec((N_EXPERTS, tm), lambda i, a, b: (0, i)),
        ],
        scratch_shapes=[pltpu.VMEM((tm + 2 * SUBLANES, SC_WIDTH), F32)],
    )
    return pl.pallas_call(
        _merge_body,
        grid_spec=grid_spec,
        out_shape=[jax.ShapeDtypeStruct((t, D_MODEL), F32), jax.ShapeDtypeStruct((t, XW), BF16),
                   jax.ShapeDtypeStruct((N_EXPERTS, t), F32)],
        compiler_params=_cparams(("arbitrary",)),
        name="merge",
    )(flags[0], flags[1], x, o_ssd, proj, proj, proj, proj, proj, proj, proj, proj,
      sc_w, w_sc, gate_b, w_o, norm2, router_t)


def _route_body(p_ref, dest_ref, posx_ref, *, cap):
    rows = p_ref.shape[1]
    r = lax.broadcasted_iota(I32, (LANES, LANES), 0)
    c = lax.broadcasted_iota(I32, (LANES, LANES), 1)
    upper = jnp.where(r <= c, 1.0, 0.0).astype(BF16)
    last_col = jnp.where(r == LANES - 1, 1.0, 0.0).astype(BF16)
    rr = lax.broadcasted_iota(I32, (rows, rows), 0)
    rc = lax.broadcasted_iota(I32, (rows, rows), 1)
    below = jnp.where(rc < rr, 1.0, 0.0).astype(BF16)

    def count(mask):
        s = jnp.sum(jnp.where(mask, 1.0, 0.0), axis=0, keepdims=True)
        return jnp.sum(s, axis=1, keepdims=True)

    def excl_prefix(mask):
        x = jnp.where(mask, 1.0, 0.0)
        incl = _dot(x.astype(BF16), upper)
        row_tot = _dot(incl.astype(BF16), last_col)
        row_off = _dot(below, row_tot.astype(BF16))
        return incl + row_off - x

    def search(i, prefix):
        bit = jnp.left_shift(jnp.int32(1), 30 - i)
        out = []
        for e in range(N_EXPERTS):
            keys = pltpu.bitcast(p_ref[e], I32)
            cand = prefix[e] | bit
            out.append(jnp.where(count(keys >= cand) >= cap, cand, prefix[e]))
        return tuple(out)

    thr = lax.fori_loop(0, 31, search, tuple(jnp.zeros((1, 1), I32) for _ in range(N_EXPERTS)))
    for e in range(N_EXPERTS):
        keys = pltpu.bitcast(p_ref[e], I32)
        gt = keys > thr[e]
        eq = keys == thr[e]
        need = cap - count(gt)
        sel = gt | (eq & (excl_prefix(eq) < need))
        pos = excl_prefix(sel).astype(I32)
        posx_ref[e] = pos
        dest_ref[e] = jnp.where(sel, pos, -1)


def _route(probs3, cap):
    e, rows, _ = probs3.shape
    return pl.pallas_call(
        functools.partial(_route_body, cap=cap),
        out_shape=[jax.ShapeDtypeStruct((e, rows, LANES), I32), jax.ShapeDtypeStruct((e, rows, LANES), I32)],
        compiler_params=pltpu.CompilerParams(vmem_limit_bytes=VMEM_LIMIT),
        name="route",
    )(probs3)


def _block_counts(posx_list, st):
    rb = st.tk // LANES
    c0s, c1s = [], []
    for g, posx in enumerate(posx_list):
        c0 = posx[:, ::rb, 0]
        c1 = jnp.concatenate([c0[:, 1:], jnp.full((N_EXPERTS, 1), st.cap_g[g], I32)], axis=1)
        c0s.append(c0.T.reshape(-1))
        c1s.append(c1.T.reshape(-1))
    return jnp.concatenate(c0s), jnp.concatenate(c1s)


def _pass_window(c0_s, c1_s, rs_s, kb, e, k):
    c0 = c0_s[kb * N_EXPERTS + e]
    c1 = c1_s[kb * N_EXPERTS + e]
    lo = jnp.minimum(c0 + k * PASS, c1)
    base = (lo // BF16_ROWS) * BF16_ROWS
    row = pl.multiple_of(rs_s[kb * N_EXPERTS + e] + base, BF16_ROWS)
    return lo, base, row, jnp.minimum(lo + PASS, c1)


def _num_passes(c0_s, c1_s, kb):
    most = c1_s[kb * N_EXPERTS] - c0_s[kb * N_EXPERTS]
    for e in range(1, N_EXPERTS):
        most = jnp.maximum(most, c1_s[kb * N_EXPERTS + e] - c0_s[kb * N_EXPERTS + e])
    return (most + PASS - 1) // PASS


def _window_one_hot(c0_s, c1_s, rs_s, dest_ref, kb, k):
    tk = dest_ref.shape[1]
    slot = lax.broadcasted_iota(I32, (WIN, tk), 0)
    parts = []
    for e in range(N_EXPERTS):
        lo, base, _, _ = _pass_window(c0_s, c1_s, rs_s, kb, e, k)
        d = dest_ref[e:e + 1, :]
        rel = jnp.where(d >= lo, jnp.where(d < lo + PASS, d - base, -1), -1)
        parts.append(jnp.where(rel == slot, 1.0, 0.0).astype(BF16))
    return jnp.concatenate(parts, axis=0)


def _dispatch_body(c0_s, c1_s, rs_s, gfirst_s, cap_s, dest_ref, h_ref, xe_hbm, stage, carry, zeros, sem, cnt):
    kb = pl.program_id(0)

    def window_copy(slot, e, row):
        return pltpu.make_async_copy(stage.at[slot, pl.ds(e * WIN, WIN), :], xe_hbm.at[pl.ds(row, WIN), :], sem.at[0])

    def wait_all():
        for e in range(N_EXPERTS):
            window_copy(0, e, 0).wait()

    def pad_copy(e):
        row = pl.multiple_of(rs_s[kb * N_EXPERTS + e] + cap_s[kb], BF16_ROWS)
        return pltpu.make_async_copy(zeros, xe_hbm.at[pl.ds(row, zeros.shape[0]), :], sem.at[1])

    @pl.when(kb == 0)
    def _():
        cnt[0] = 0

    @pl.when(gfirst_s[kb] == 1)
    def _():
        carry[...] = jnp.zeros_like(carry)
        zeros[...] = jnp.zeros_like(zeros)
        for e in range(N_EXPERTS):
            pad_copy(e).start()
        for e in range(N_EXPERTS):
            pad_copy(e).wait()

    def one_pass(k, _):
        slot = cnt[0] % 2
        packed = _dot(_window_one_hot(c0_s, c1_s, rs_s, dest_ref, kb, k), h_ref[...])
        rows = []
        for e in range(N_EXPERTS):
            lo, base, row, nxt = _pass_window(c0_s, c1_s, rs_s, kb, e, k)
            head = packed[e * WIN:e * WIN + BF16_ROWS, :] + carry[e].astype(F32)
            stage[slot, e * WIN:e * WIN + BF16_ROWS, :] = head.astype(BF16)
            stage[slot, e * WIN + BF16_ROWS:(e + 1) * WIN, :] = packed[e * WIN + BF16_ROWS:(e + 1) * WIN, :].astype(BF16)
            off = pl.multiple_of(e * WIN + (nxt // BF16_ROWS) * BF16_ROWS - base, BF16_ROWS)
            carry[e] = stage[slot, pl.ds(off, BF16_ROWS), :]
            rows.append(row)

        @pl.when(cnt[0] > 0)
        def _():
            wait_all()

        for e in range(N_EXPERTS):
            window_copy(slot, e, rows[e]).start()
        cnt[0] = cnt[0] + 1
        return 0

    lax.fori_loop(0, _num_passes(c0_s, c1_s, kb), one_pass, 0)

    @pl.when((kb == pl.num_programs(0) - 1) & (cnt[0] > 0))
    def _():
        wait_all()


def _dispatch(c0, c1, rs, gfirst, cap, dest, h2a, st):
    tk = st.tk
    grid_spec = pltpu.PrefetchScalarGridSpec(
        num_scalar_prefetch=5,
        grid=(st.total // tk,),
        in_specs=[
            pl.BlockSpec((N_EXPERTS, tk), lambda i, *_: (0, i)),
            pl.BlockSpec((tk, XW), lambda i, *_: (i, 0)),
        ],
        out_specs=pl.BlockSpec(memory_space=pl.ANY),
        scratch_shapes=[
            pltpu.VMEM((2, N_EXPERTS * WIN, XW), BF16),
            pltpu.VMEM((N_EXPERTS, BF16_ROWS, XW), BF16),
            pltpu.VMEM((st.ts, XW), BF16),
            pltpu.SemaphoreType.DMA((2,)),
            pltpu.SMEM((1,), I32),
        ],
    )
    return pl.pallas_call(
        _dispatch_body,
        grid_spec=grid_spec,
        out_shape=jax.ShapeDtypeStruct((st.slot_rows, XW), BF16),
        compiler_params=_cparams(("arbitrary",)),
        name="moe_dispatch",
    )(c0, c1, rs, gfirst, cap, dest, h2a)


def _moe_ffn_body(pad_s, src_s, xe_ref, wg_ref, wu_ref, wd_ref, o_ref, *, tiles_per_expert):
    i = pl.program_id(0)

    @pl.when(pad_s[i] == 1)
    def _():
        o_ref[...] = jnp.zeros_like(o_ref)

    @pl.when(pad_s[i] == 0)
    def _():
        e = i // tiles_per_expert
        xe = xe_ref[:, :D_MODEL]
        terms = xe_ref[:, D_MODEL:].astype(F32)
        lane = lax.broadcasted_iota(I32, terms.shape, 1)
        mine = (lane < 3 * GATE_STRIDE) & (lane % GATE_STRIDE == e)
        gate = jnp.sum(jnp.where(mine, terms, 0.0), axis=1, keepdims=True)
        a = _dot(xe, wg_ref[...])
        u = _dot(xe, wu_ref[...])
        act = (a * _sigmoid(a) * u).astype(BF16)
        o_ref[...] = (_dot(act, wd_ref[...]) * gate).astype(BF16)


def _moe_ffn(xe, wg, wu, wd, layer, st):
    ts = st.ts
    tpe = st.tiles_per_expert
    pad, src = st.tile_tables()
    grid_spec = pltpu.PrefetchScalarGridSpec(
        num_scalar_prefetch=2,
        grid=(N_EXPERTS * tpe,),
        in_specs=[
            pl.BlockSpec((ts, XW), lambda i, p, s: (s[i], 0)),
            pl.BlockSpec((None, None, D_MODEL, EXPERT_FF), lambda i, p, s: (layer, i // tpe, 0, 0)),
            pl.BlockSpec((None, None, D_MODEL, EXPERT_FF), lambda i, p, s: (layer, i // tpe, 0, 0)),
            pl.BlockSpec((None, None, EXPERT_FF, D_MODEL), lambda i, p, s: (layer, i // tpe, 0, 0)),
        ],
        out_specs=pl.BlockSpec((ts, D_MODEL), lambda i, p, s: (i, 0)),
    )
    return pl.pallas_call(
        functools.partial(_moe_ffn_body, tiles_per_expert=tpe),
        grid_spec=grid_spec,
        out_shape=jax.ShapeDtypeStruct((st.slot_rows, D_MODEL), BF16),
        compiler_params=_cparams(("arbitrary",)),
        name="moe_ffn",
    )(pad, src, xe, wg, wu, wd)


OVERFLOW_SLOT = 2


def _combine_body(c0_s, c1_s, rs_s, dest_ref, x_ref, ye_hbm, o_ref, buf, sem):
    kb = pl.program_id(0)
    nkb = pl.num_programs(0)

    def window_copy(slot, e, row):
        return pltpu.make_async_copy(ye_hbm.at[pl.ds(row, WIN), :], buf.at[slot, pl.ds(e * WIN, WIN), :], sem.at[slot])

    def fetch(block, k, slot):
        for e in range(N_EXPERTS):
            window_copy(slot, e, _pass_window(c0_s, c1_s, rs_s, block, e, k)[2]).start()

    def wait_all(slot):
        for e in range(N_EXPERTS):
            window_copy(slot, e, 0).wait()

    @pl.when(kb == 0)
    def _():
        fetch(0, 0, 0)

    slot = kb % 2
    wait_all(slot)

    @pl.when(kb + 1 < nkb)
    def _():
        fetch(kb + 1, 0, 1 - slot)

    o_ref[...] = x_ref[...] + _dot_tn(_window_one_hot(c0_s, c1_s, rs_s, dest_ref, kb, 0), buf[slot])

    def extra_pass(k, _):
        fetch(kb, k, OVERFLOW_SLOT)
        wait_all(OVERFLOW_SLOT)
        o_ref[...] += _dot_tn(_window_one_hot(c0_s, c1_s, rs_s, dest_ref, kb, k), buf[OVERFLOW_SLOT])
        return 0

    lax.fori_loop(1, _num_passes(c0_s, c1_s, kb), extra_pass, 0)


def _combine(c0, c1, rs, dest, x1, ye, st):
    tk = st.tk
    grid_spec = pltpu.PrefetchScalarGridSpec(
        num_scalar_prefetch=3,
        grid=(st.total // tk,),
        in_specs=[
            pl.BlockSpec((N_EXPERTS, tk), lambda i, *_: (0, i)),
            pl.BlockSpec((tk, D_MODEL), lambda i, *_: (i, 0)),
            pl.BlockSpec(memory_space=pl.ANY),
        ],
        out_specs=pl.BlockSpec((tk, D_MODEL), lambda i, *_: (i, 0)),
        scratch_shapes=[
            pltpu.VMEM((3, N_EXPERTS * WIN, D_MODEL), BF16),
            pltpu.SemaphoreType.DMA((3,)),
        ],
    )
    return pl.pallas_call(
        _combine_body,
        grid_spec=grid_spec,
        out_shape=jax.ShapeDtypeStruct((st.total, D_MODEL), F32),
        compiler_params=_cparams(("arbitrary",)),
        name="moe_combine",
    )(c0, c1, rs, dest, x1, ye)


def _final_norm_body(x_ref, w_ref, o_ref):
    x = x_ref[...]
    o_ref[...] = x * lax.rsqrt(jnp.mean(x * x, axis=-1, keepdims=True) + EPS) * w_ref[...]


def _final_norm(x, w, st, group):
    t, tm = st.t_g[group], st.tm
    first = st.off_g[group] // tm
    return pl.pallas_call(
        _final_norm_body,
        grid=(t // tm,),
        in_specs=[pl.BlockSpec((tm, D_MODEL), lambda i: (first + i, 0)), pl.BlockSpec((1, D_MODEL), lambda i: (0, 0))],
        out_specs=pl.BlockSpec((tm, D_MODEL), lambda i: (i, 0)),
        out_shape=jax.ShapeDtypeStruct((t, D_MODEL), F32),
        compiler_params=_cparams(("arbitrary",)),
        name="final_norm",
    )(x, w)


def _pad_lanes(v, width=LANES):
    return jnp.pad(v, [(0, 0)] * (v.ndim - 1) + [(0, width - v.shape[-1])])


def _trunk_stream(xs, params, tm, tq, tk, ts, tc, cps):
    (norm1_w, w_in, ssd_conv_w, ssd_conv_b, ssd_dt_bias, ssd_a_log, ssd_d, ssd_norm_w, w_ssd_out, sc_conv_w,
     w_sc_out, gate_b, w_o, norm2_w, router_w, w_gate, w_up, w_down, final_norm_w) = params
    depth = w_in.shape[0]
    st = _Stream([(x.shape[0], x.shape[1]) for x in xs], tm, tq, tk, ts, tc, cps)
    x = jnp.concatenate([v.reshape(-1, D_MODEL) for v in xs], axis=0)

    w_main = jnp.concatenate([w_in[:, :, :_OFF_DT], w_in[:, :, _OFF_GATE:], w_in[:, :, _OFF_SCB:_OFF_GATE]],
                             axis=2).astype(BF16)
    w_dt = _pad_lanes(w_in[:, :, _OFF_DT:_OFF_SCB]).astype(BF16)
    dtb = _pad_lanes(ssd_dt_bias.reshape(depth, 1, 2 * HEADS))
    alog = _pad_lanes(ssd_a_log.reshape(depth, 1, 2 * HEADS))
    dskip = jnp.repeat(ssd_d, HEAD_DIM, axis=-1).reshape(depth, 1, INNER)
    ef, eb, eft, ebt = _expansion_matrices()
    router_t = jnp.swapaxes(router_w, 1, 2)
    wg, wu, wd = w_gate.astype(BF16), w_up.astype(BF16), w_down.astype(BF16)
    row = lambda v: v.reshape(depth, 1, v.shape[-1])

    chunk_first, chunk_last = st.seq_flags(CHUNK)
    tile_flags = st.seq_flags(tq)
    conv_flags = st.seq_flags(tc)
    region_start, group_first, group_cap = st.block_tables()

    for l in range(depth):
        proj, dt_raw = _inproj(x, row(norm1_w), w_main, w_dt, l, st)
        xbc = _conv_xbc(proj, ssd_conv_w, row(ssd_conv_b), l, st, conv_flags)
        y1 = _ssd_fwd(xbc, dt_raw, dtb, alog, ef, eft, l, st, chunk_first)
        o_ssd = _ssd_bwd(xbc, dt_raw, y1, proj, dtb, alog, eb, ebt, dskip, row(ssd_norm_w),
                         w_ssd_out.astype(BF16), l, st, chunk_last)
        x1, h2, probs_t = _merge(x, o_ssd, proj, sc_conv_w, w_sc_out.astype(BF16), row(gate_b), w_o.astype(BF16),
                                 row(norm2_w), router_t, l, st, tile_flags)
        dests, posxs = [], []
        for g in range(len(xs)):
            pg = probs_t[:, st.off_g[g]:st.off_g[g] + st.t_g[g]].reshape(N_EXPERTS, st.t_g[g] // LANES, LANES)
            dest, posx = _route(pg, st.cap_g[g])
            dests.append(dest.reshape(N_EXPERTS, st.t_g[g]))
            posxs.append(posx)
        dest = jnp.concatenate(dests, axis=1)
        c0, c1 = _block_counts(posxs, st)
        xe = _dispatch(c0, c1, region_start, group_first, group_cap, dest, h2, st)
        ye = _moe_ffn(xe, wg, wu, wd, l, st)
        x = _combine(c0, c1, region_start, dest, x1, ye, st)

    fw = final_norm_w.reshape(1, D_MODEL)
    return [_final_norm(x, fw, st, g).reshape(v.shape) for g, v in enumerate(xs)]


def kernel(x_prompt, x_sample, norm1_w, w_in, ssd_conv_w, ssd_conv_b, ssd_dt_bias, ssd_a_log, ssd_d, ssd_norm_w, w_ssd_out, sc_conv_w, w_sc_out, gate_b, w_o, norm2_w, router_w, w_gate, w_up, w_down, final_norm_w):
    params = (norm1_w, w_in, ssd_conv_w, ssd_conv_b, ssd_dt_bias, ssd_a_log, ssd_d, ssd_norm_w, w_ssd_out, sc_conv_w,
              w_sc_out, gate_b, w_o, norm2_w, router_w, w_gate, w_up, w_down, final_norm_w)
    y_prompt, y_sample = _trunk_stream([x_prompt, x_sample], params, tm=1024, tq=512, tk=256, ts=512, tc=1024, cps=4)
    return (y_prompt, y_sample)
```

```python
import functools

import numpy as np
import jax
import jax.numpy as jnp
from jax import lax
from jax.experimental import pallas as pl
from jax.experimental.pallas import tpu as pltpu

F32 = jnp.float32
BF16 = jnp.bfloat16
I32 = jnp.int32

D_MODEL = 1024
HEADS = 24
HEAD_DIM = 64
INNER = HEADS * HEAD_DIM
GROUPS = 4
GROUP_W = INNER // GROUPS
STATE = 128
CHUNK = 128
SSD_CONV = 5
SC_WIDTH = 512
SC_CONV = 3
N_EXPERTS = 16
EXPERT_FF = 2048
CAPACITY_FACTOR = 2
EPS = 1e-6
LANES = 128
SUBLANES = 8

_OFF_XBC = INNER
_OFF_DT = _OFF_XBC + INNER + 2 * GROUPS * STATE
_OFF_SCB = _OFF_DT + 2 * HEADS
_OFF_GATE = _OFF_SCB + 3 * SC_WIDTH
_N_IN = _OFF_GATE + 2 * D_MODEL
COL_X = INNER
COL_B = 2 * INNER
COL_C = COL_B + GROUPS * STATE
COL_GATE = COL_C + GROUPS * STATE
COL_SCB = COL_GATE + 2 * D_MODEL
N_MAIN = COL_SCB + 3 * SC_WIDTH
CB = 512

BF16_ROWS = 2 * SUBLANES
PASS = 64
WIN = PASS + BF16_ROWS
XW = D_MODEL + LANES
GATE_STRIDE = N_EXPERTS

VMEM_LIMIT = 50 * 1024 * 1024


def _cparams(sem):
    return pltpu.CompilerParams(dimension_semantics=sem, vmem_limit_bytes=VMEM_LIMIT)


def _sigmoid(v):
    return 1.0 / (1.0 + jnp.exp(-v))


def _split3(v):
    hi = v.astype(BF16)
    r = v - hi.astype(F32)
    mid = r.astype(BF16)
    lo = (r - mid.astype(F32)).astype(BF16)
    return hi, mid, lo


def _dot(a, b):
    return jnp.dot(a, b, preferred_element_type=F32)


def _dot_nt(a, b):
    return lax.dot_general(a, b, (((1,), (1,)), ((), ())), preferred_element_type=F32)


def _dot_tn(a, b):
    return lax.dot_general(a, b, (((0,), (0,)), ((), ())), preferred_element_type=F32)


def _dot3_l(a_f32, b_exact):
    hi, mid, lo = _split3(a_f32)
    return _dot(hi, b_exact) + _dot(mid, b_exact) + _dot(lo, b_exact)


def _dot3_r(a_exact, b_f32):
    hi, mid, lo = _split3(b_f32)
    return _dot(a_exact, hi) + _dot(a_exact, mid) + _dot(a_exact, lo)


class _Stream:
    def __init__(self, groups, tm, tq, tk, ts, tc, cps):
        self.groups = tuple(groups)
        self.tm, self.tq, self.tk, self.ts, self.tc, self.cps = tm, tq, tk, ts, tc, cps
        self.t_g = [n * l for n, l in self.groups]
        self.off_g = [int(v) for v in np.cumsum([0] + self.t_g[:-1])]
        self.total = int(sum(self.t_g))
        self.cap_g = [CAPACITY_FACTOR * t // N_EXPERTS for t in self.t_g]
        self.nkb_g = [t // tk for t in self.t_g]
        for (n, l), t, c in zip(self.groups, self.t_g, self.cap_g):
            assert l % (CHUNK * cps) == 0 and l % tq == 0 and l % tc == 0 and t % tk == 0 and c % ts == 0
            assert t % tm == 0 and tk % LANES == 0 and tc % CONV_ROWS == 0
        assert ts >= WIN
        self.region_g = [c + ts for c in self.cap_g]
        self.rows_per_expert = int(sum(self.region_g))
        self.tiles_per_expert = self.rows_per_expert // ts
        self.slot_rows = N_EXPERTS * self.rows_per_expert
        self.region_off_g = [int(v) for v in np.cumsum([0] + self.region_g[:-1])]

    def block_tables(self):
        rs, gfirst, cap = [], [], []
        for g, nkb in enumerate(self.nkb_g):
            for kb in range(nkb):
                gfirst.append(1 if kb == 0 else 0)
                cap.append(self.cap_g[g])
                rs.extend(e * self.rows_per_expert + self.region_off_g[g] for e in range(N_EXPERTS))
        return tuple(jnp.asarray(np.asarray(v, np.int32)) for v in (rs, gfirst, cap))

    def tile_tables(self):
        pad, src = [], []
        for e in range(N_EXPERTS):
            for g, c in enumerate(self.cap_g):
                n = c // self.ts
                base = len(pad)
                pad.extend([0] * n + [1])
                src.extend(list(range(base, base + n)) + [base + n - 1])
        return jnp.asarray(np.asarray(pad, np.int32)), jnp.asarray(np.asarray(src, np.int32))

    def seq_flags(self, tile):
        n_tiles = self.total // tile
        first = np.zeros((n_tiles,), np.int32)
        last = np.zeros((n_tiles,), np.int32)
        for (n, l), off in zip(self.groups, self.off_g):
            for s in range(n):
                first[(off + s * l) // tile] = 1
                last[(off + (s + 1) * l) // tile - 1] = 1
        return jnp.asarray(first), jnp.asarray(last)


def _inproj_body(x_ref, nw_ref, w_ref, wdt_ref, proj_ref, dt_ref, h_scr):
    @pl.when(pl.program_id(1) == 0)
    def _():
        x = x_ref[...]
        ms = jnp.mean(x * x, axis=-1, keepdims=True)
        h = (x * lax.rsqrt(ms + EPS) * nw_ref[...]).astype(BF16)
        h_scr[...] = h
        dt_ref[...] = _dot(h, wdt_ref[...])

    proj_ref[...] = _dot(h_scr[...], w_ref[...]).astype(BF16)


def _inproj(x, nw, w_main, w_dt, layer, st):
    t, tm = st.total, st.tm
    tn = N_MAIN // 3
    return pl.pallas_call(
        _inproj_body,
        grid=(t // tm, N_MAIN // tn),
        in_specs=[
            pl.BlockSpec((tm, D_MODEL), lambda i, n: (i, 0)),
            pl.BlockSpec((None, 1, D_MODEL), lambda i, n: (layer, 0, 0)),
            pl.BlockSpec((None, D_MODEL, tn), lambda i, n: (layer, 0, n)),
            pl.BlockSpec((None, D_MODEL, LANES), lambda i, n: (layer, 0, 0)),
        ],
        out_specs=[
            pl.BlockSpec((tm, tn), lambda i, n: (i, n)),
            pl.BlockSpec((tm, LANES), lambda i, n: (i, 0)),
        ],
        out_shape=[jax.ShapeDtypeStruct((t, N_MAIN), BF16), jax.ShapeDtypeStruct((t, LANES), F32)],
        scratch_shapes=[pltpu.VMEM((tm, D_MODEL), BF16)],
        compiler_params=_cparams(("arbitrary", "arbitrary")),
        name="inproj",
    )(x, nw, w_main, w_dt)


CONV_ROWS = 128


def _conv_body(topz, botz, main_ref, prev_ref, next_ref, w_ref, b_ref, o_ref, ext):
    i = pl.program_id(0)
    tq = main_ref.shape[0]
    halo = BF16_ROWS
    pad = SSD_CONV // 2
    zero = jnp.zeros_like(prev_ref[...])
    ext[0:halo, :] = jnp.where(topz[i] == 1, zero, prev_ref[...])
    ext[halo:halo + tq, :] = main_ref[...]
    ext[halo + tq:2 * halo + tq, :] = jnp.where(botz[i] == 1, zero, next_ref[...])
    win = CONV_ROWS + 2 * halo
    taps = [k for k in range(SSD_CONV) if k != pad]
    r = lax.broadcasted_iota(I32, (CONV_ROWS, win), 0)
    c = lax.broadcasted_iota(I32, (CONV_ROWS, win), 1)
    shift = jnp.concatenate([jnp.where(c == r + halo + k - pad, 1.0, 0.0).astype(BF16) for k in taps], axis=0)
    for sb in range(tq // CONV_ROWS):
        r0 = sb * CONV_ROWS
        moved = _dot(shift, ext[r0:r0 + win, :])
        acc = b_ref[...] + ext[halo + r0:halo + r0 + CONV_ROWS, :].astype(F32) * w_ref[pad:pad + 1, :]
        for n, k in enumerate(taps):
            acc = acc + moved[n * CONV_ROWS:(n + 1) * CONV_ROWS, :] * w_ref[k:k + 1, :]
        o_ref[r0:r0 + CONV_ROWS, :] = (acc * _sigmoid(acc)).astype(BF16)


def _conv_xbc(proj, conv_w, conv_b, layer, st, flags):
    t, tq = st.total, st.tc
    ncb = (COL_GATE - COL_X) // CB
    cb0 = COL_X // CB
    rpt = tq // BF16_ROWS
    last_rb = t // BF16_ROWS - 1
    grid_spec = pltpu.PrefetchScalarGridSpec(
        num_scalar_prefetch=2,
        grid=(t // tq, ncb),
        in_specs=[
            pl.BlockSpec((tq, CB), lambda i, j, a, b: (i, cb0 + j)),
            pl.BlockSpec((BF16_ROWS, CB), lambda i, j, a, b: (jnp.maximum(i * rpt - 1, 0), cb0 + j)),
            pl.BlockSpec((BF16_ROWS, CB), lambda i, j, a, b: (jnp.minimum((i + 1) * rpt, last_rb), cb0 + j)),
            pl.BlockSpec((None, SSD_CONV, CB), lambda i, j, a, b: (layer, 0, j)),
            pl.BlockSpec((None, 1, CB), lambda i, j, a, b: (layer, 0, j)),
        ],
        out_specs=pl.BlockSpec((tq, CB), lambda i, j, a, b: (i, j)),
        scratch_shapes=[pltpu.VMEM((tq + 2 * BF16_ROWS, CB), BF16)],
    )
    return pl.pallas_call(
        _conv_body,
        grid_spec=grid_spec,
        out_shape=jax.ShapeDtypeStruct((t, COL_GATE - COL_X), BF16),
        compiler_params=_cparams(("arbitrary", "arbitrary")),
        name="conv_xbc",
    )(flags[0], flags[1], proj, proj, proj, conv_w, conv_b)


def _ssd_scalars(dt_ref, rows, dtb_ref, alog_ref):
    v = dt_ref[rows, :] + dtb_ref[...]
    u = jnp.exp(-jnp.abs(v))
    w = 1.0 + u
    dt = jnp.maximum(v, 0.0) + jnp.where(w == 1.0, u, jnp.log(w) * (u / (w - 1.0)))
    da = dt * (-jnp.exp(alog_ref[...]))
    r = lax.broadcasted_iota(I32, (CHUNK, CHUNK), 0)
    c = lax.broadcasted_iota(I32, (CHUNK, CHUNK), 1)
    tri = jnp.where(c <= r, 1.0, 0.0).astype(BF16)
    upper = jnp.where(r <= c, 1.0, 0.0).astype(BF16)
    cs = _dot3_r(tri, da)
    da_t = da.T
    dt_t = dt.T
    cs_t = _dot3_l(da_t, upper)
    return dt, cs, cs - da, dt_t, cs_t, cs_t - da_t, r, c


def _expand(v, e_ref):
    hi = v.astype(BF16)
    mid = (v - hi.astype(F32)).astype(BF16)
    e = e_ref[...]
    return _dot(hi, e) + _dot(mid, e)


def _expand_pair(a, b, e_ref):
    out = _expand(jnp.concatenate([a, b], axis=0), e_ref)
    return out[:CHUNK, :], out[CHUNK:, :]


def _state_step(s_scr, xs_f32, w_tok, dec_lanes, b_ref, rows):
    xw = (xs_f32 * w_tok).astype(BF16)
    for g in range(GROUPS):
        sl = slice(g * GROUP_W, (g + 1) * GROUP_W)
        sc = _dot_tn(b_ref[rows, g * STATE:(g + 1) * STATE], xw[:, sl])
        s_scr[:, sl] = s_scr[:, sl] * dec_lanes[:, sl] + sc


def _ssd_fwd_body(first, xs_ref, b_ref, c_ref, dt_ref, dtb_ref, alog_ref, ef_ref, y_ref, s_scr):
    cps = xs_ref.shape[0] // CHUNK

    @pl.when(pl.program_id(0) == 0)
    def _():
        s_scr[...] = jnp.zeros_like(s_scr)

    for sub in range(cps):
        rows = slice(sub * CHUNK, (sub + 1) * CHUNK)
        keep = jnp.where(first[pl.program_id(0) * cps + sub] == 1, 0.0, 1.0)
        dt, cs, ex, dt_t, cs_t, ex_t, r, c = _ssd_scalars(dt_ref, rows, dtb_ref, alog_ref)
        w_out, w_st = _expand_pair(jnp.exp(cs) * keep, jnp.exp(cs[CHUNK - 1:CHUNK, :] - cs) * dt, ef_ref)
        for g in range(GROUPS):
            sl = slice(g * GROUP_W, (g + 1) * GROUP_W)
            yo = _dot(c_ref[rows, g * STATE:(g + 1) * STATE], s_scr[:, sl].astype(BF16))
            y_ref[rows, sl] = yo * w_out[:, sl]
        lane = lax.broadcasted_iota(I32, (CHUNK, 2 * HEAD_DIM), 1)
        lower = r >= c
        pairs_per_group = HEADS // GROUPS // 2
        for pi in range(HEADS // 2):
            g = pi // pairs_per_group
            if pi % pairs_per_group == 0:
                cb = _dot_nt(c_ref[rows, g * STATE:(g + 1) * STATE], b_ref[rows, g * STATE:(g + 1) * STATE])
            ms = []
            for h in (2 * pi, 2 * pi + 1):
                hb = HEADS + h
                arg = jnp.where(lower, cs[:, h:h + 1] - cs_t[h:h + 1, :], ex_t[hb:hb + 1, :] - ex[:, hb:hb + 1])
                d0 = dt_t[h:h + 1, :]
                d1 = dt_t[hb:hb + 1, :]
                w = jnp.where(r > c, d0, jnp.where(r < c, d1, d0 + d1))
                ms.append((jnp.exp(arg) * w * cb).astype(BF16))
            psl = slice(pi * 2 * HEAD_DIM, (pi + 1) * 2 * HEAD_DIM)
            xp = xs_ref[rows, psl]
            zero = jnp.zeros_like(xp)
            rhs = jnp.concatenate([jnp.where(lane < HEAD_DIM, xp, zero), jnp.where(lane >= HEAD_DIM, xp, zero)],
                                  axis=0)
            y_ref[rows, psl] = y_ref[rows, psl] + _dot(jnp.concatenate(ms, axis=1), rhs)
        _state_step(s_scr, xs_ref[rows, :].astype(F32), w_st, w_out[CHUNK - 1:CHUNK, :], b_ref, rows)


def _ssd_bwd_body(last, xs_ref, b_ref, c_ref, dt_ref, y1_ref, z_ref, dtb_ref, alog_ref, eb_ref,
                  dskip_ref, nw_ref, wout_ref, o_ref, s_scr, yn_scr):
    cps = xs_ref.shape[0] // CHUNK
    step = pl.num_programs(0) - 1 - pl.program_id(0)

    @pl.when(pl.program_id(0) == 0)
    def _():
        s_scr[...] = jnp.zeros_like(s_scr)

    for sub in reversed(range(cps)):
        rows = slice(sub * CHUNK, (sub + 1) * CHUNK)
        keep = jnp.where(last[step * cps + sub] == 1, 0.0, 1.0)
        dt, cs, ex, dt_t, cs_t, ex_t, r, c = _ssd_scalars(dt_ref, rows, dtb_ref, alog_ref)
        tot = cs[CHUNK - 1:CHUNK, :]
        w_out, w_st = _expand_pair(jnp.exp(tot - ex) * keep, jnp.exp(ex) * dt, eb_ref)
        xs = xs_ref[rows, :].astype(F32)
        z = z_ref[rows, :].astype(F32)
        zs = z * _sigmoid(z)
        for g in range(GROUPS):
            sl = slice(g * GROUP_W, (g + 1) * GROUP_W)
            yo = _dot(c_ref[rows, g * STATE:(g + 1) * STATE], s_scr[:, sl].astype(BF16))
            y = (y1_ref[rows, sl] + yo * w_out[:, sl] + xs[:, sl] * dskip_ref[:, sl]) * zs[:, sl]
            y = y * lax.rsqrt(jnp.mean(y * y, axis=-1, keepdims=True) + EPS)
            yn_scr[rows, sl] = (y * nw_ref[:, sl]).astype(BF16)
        _state_step(s_scr, xs, w_st, w_out[0:1, :], b_ref, rows)
    o_ref[...] = _dot(yn_scr[...], wout_ref[...])


def _expansion_matrices():
    rows = np.arange(LANES)[:, None]
    cols = np.arange(INNER)[None, :] // HEAD_DIM
    ef = (rows == cols).astype(np.float32)
    eb = (rows == cols + HEADS).astype(np.float32)
    return jnp.asarray(ef, BF16), jnp.asarray(eb, BF16)


def _ssd_fwd(xbc, dt_raw, dtb, alog, ef, layer, st, first):
    t = st.total
    nb = GROUPS * STATE // CB
    rows = st.cps * CHUNK
    grid_spec = pltpu.PrefetchScalarGridSpec(
        num_scalar_prefetch=1,
        grid=(t // rows,),
        in_specs=[
            pl.BlockSpec((rows, INNER), lambda i, f: (i, 0)),
            pl.BlockSpec((rows, CB), lambda i, f: (i, INNER // CB)),
            pl.BlockSpec((rows, CB), lambda i, f: (i, INNER // CB + nb)),
            pl.BlockSpec((rows, LANES), lambda i, f: (i, 0)),
            pl.BlockSpec((None, 1, LANES), lambda i, f: (layer, 0, 0)),
            pl.BlockSpec((None, 1, LANES), lambda i, f: (layer, 0, 0)),
            pl.BlockSpec((LANES, INNER), lambda i, f: (0, 0)),
        ],
        out_specs=pl.BlockSpec((rows, INNER), lambda i, f: (i, 0)),
        scratch_shapes=[pltpu.VMEM((STATE, INNER), F32)],
    )
    return pl.pallas_call(
        _ssd_fwd_body,
        grid_spec=grid_spec,
        out_shape=jax.ShapeDtypeStruct((t, INNER), F32),
        compiler_params=_cparams(("arbitrary",)),
        name="ssd_fwd",
    )(first, xbc, xbc, xbc, dt_raw, dtb, alog, ef)


def _ssd_bwd(xbc, dt_raw, y1, proj, dtb, alog, eb, dskip, norm_w, w_out, layer, st, last):
    t = st.total
    rows = st.cps * CHUNK
    nc = t // rows
    nb = GROUPS * STATE // CB
    rev = lambda i: nc - 1 - i
    grid_spec = pltpu.PrefetchScalarGridSpec(
        num_scalar_prefetch=1,
        grid=(nc,),
        in_specs=[
            pl.BlockSpec((rows, INNER), lambda i, f: (rev(i), 0)),
            pl.BlockSpec((rows, CB), lambda i, f: (rev(i), INNER // CB)),
            pl.BlockSpec((rows, CB), lambda i, f: (rev(i), INNER // CB + nb)),
            pl.BlockSpec((rows, LANES), lambda i, f: (rev(i), 0)),
            pl.BlockSpec((rows, INNER), lambda i, f: (rev(i), 0)),
            pl.BlockSpec((rows, INNER), lambda i, f: (rev(i), 0)),
            pl.BlockSpec((None, 1, LANES), lambda i, f: (layer, 0, 0)),
            pl.BlockSpec((None, 1, LANES), lambda i, f: (layer, 0, 0)),
            pl.BlockSpec((LANES, INNER), lambda i, f: (0, 0)),
            pl.BlockSpec((None, 1, INNER), lambda i, f: (layer, 0, 0)),
            pl.BlockSpec((None, 1, INNER), lambda i, f: (layer, 0, 0)),
            pl.BlockSpec((None, INNER, D_MODEL), lambda i, f: (layer, 0, 0)),
        ],
        out_specs=pl.BlockSpec((rows, D_MODEL), lambda i, f: (rev(i), 0)),
        scratch_shapes=[pltpu.VMEM((STATE, INNER), F32), pltpu.VMEM((rows, INNER), BF16)],
    )
    return pl.pallas_call(
        _ssd_bwd_body,
        grid_spec=grid_spec,
        out_shape=jax.ShapeDtypeStruct((t, D_MODEL), F32),
        compiler_params=_cparams(("arbitrary",)),
        name="ssd_bwd",
    )(last, xbc, xbc, xbc, dt_raw, y1, proj, dtb, alog, eb, dskip, norm_w, w_out)


def _merge_body(topz, botz, x_ref, ossd_ref, gate_ref, scb_ref, scc_ref, scx_ref, ccp_ref, cxp_ref, ccn_ref, cxn_ref,
                cw_ref, wsc_ref, gb_ref, wo_ref, nw_ref, rw_ref, x1_ref, h2_ref, pt_ref, ext):
    i = pl.program_id(0)
    tm = x_ref.shape[0]
    halo = SUBLANES
    pad = SC_CONV // 2
    ext[0:halo, :] = jnp.where(topz[i] == 1, 0.0, ccp_ref[...].astype(F32) * cxp_ref[...].astype(F32))
    ext[halo:halo + tm, :] = scc_ref[...].astype(F32) * scx_ref[...].astype(F32)
    ext[halo + tm:2 * halo + tm, :] = jnp.where(botz[i] == 1, 0.0, ccn_ref[...].astype(F32) * cxn_ref[...].astype(F32))
    v = ext[halo - pad:halo - pad + tm, :] * cw_ref[0:1, :]
    for k in range(1, SC_CONV):
        v = v + ext[halo - pad + k:halo - pad + k + tm, :] * cw_ref[k:k + 1, :]
    o_sc = _dot((scb_ref[...].astype(F32) * v).astype(BF16), wsc_ref[...])
    g = _sigmoid(gate_ref[...].astype(F32) + gb_ref[...])
    mix = g[:, :D_MODEL] * ossd_ref[...] + g[:, D_MODEL:] * o_sc
    x1 = x_ref[...] + _dot(mix.astype(BF16), wo_ref[...])
    x1_ref[...] = x1
    h2 = x1 * lax.rsqrt(jnp.mean(x1 * x1, axis=-1, keepdims=True) + EPS) * nw_ref[...]
    h2_ref[:, :D_MODEL] = h2.astype(BF16)
    hh, hm, _ = _split3(h2)
    rw = rw_ref[...]
    rh, rm, _ = _split3(rw)
    logits = _dot_nt(rh, hh) + _dot_nt(rh, hm) + _dot_nt(rm, hh)
    mx = jnp.max(logits, axis=0, keepdims=True)
    ex = jnp.exp(logits - mx)
    probs = ex / jnp.sum(ex, axis=0, keepdims=True)
    pt_ref[...] = probs
    pk = jnp.concatenate([probs, jnp.zeros((LANES - N_EXPERTS, tm), F32)], axis=0).T
    hi = pk.astype(BF16).astype(F32)
    mid = (pk - hi).astype(BF16).astype(F32)
    lo = pk - hi - mid
    lane = lax.broadcasted_iota(I32, (tm, LANES), 1)
    terms = jnp.where(lane < GATE_STRIDE, hi,
                      jnp.where(lane < 2 * GATE_STRIDE, pltpu.roll(mid, GATE_STRIDE, 1),
                                jnp.where(lane < 3 * GATE_STRIDE, pltpu.roll(lo, 2 * GATE_STRIDE, 1), 0.0)))
    h2_ref[:, D_MODEL:] = terms.astype(BF16)


def _merge(x, o_ssd, proj, sc_w, w_sc, gate_b, w_o, norm2, router_t, layer, st, flags):
    t, tm = st.total, st.tq
    rpt = tm // SUBLANES
    last_rb = t // SUBLANES - 1
    cscb, cscc, cscx = COL_SCB // CB, COL_SCB // CB + 1, COL_SCB // CB + 2
    prev = lambda i: jnp.maximum(i * rpt - 1, 0)
    nxt = lambda i: jnp.minimum((i + 1) * rpt, last_rb)
    grid_spec = pltpu.PrefetchScalarGridSpec(
        num_scalar_prefetch=2,
        grid=(t // tm,),
        in_specs=[
            pl.BlockSpec((tm, D_MODEL), lambda i, a, b: (i, 0)),
            pl.BlockSpec((tm, D_MODEL), lambda i, a, b: (i, 0)),
            pl.BlockSpec((tm, 2 * D_MODEL), lambda i, a, b: (i, COL_GATE // (2 * D_MODEL))),
            pl.BlockSpec((tm, CB), lambda i, a, b: (i, cscb)),
            pl.BlockSpec((tm, CB), lambda i, a, b: (i, cscc)),
            pl.BlockSpec((tm, CB), lambda i, a, b: (i, cscx)),
            pl.BlockSpec((SUBLANES, CB), lambda i, a, b: (prev(i), cscc)),
            pl.BlockSpec((SUBLANES, CB), lambda i, a, b: (prev(i), cscx)),
            pl.BlockSpec((SUBLANES, CB), lambda i, a, b: (nxt(i), cscc)),
            pl.BlockSpec((SUBLANES, CB), lambda i, a, b: (nxt(i), cscx)),
            pl.BlockSpec((None, SC_CONV, SC_WIDTH), lambda i, a, b: (layer, 0, 0)),
            pl.BlockSpec((None, SC_WIDTH, D_MODEL), lambda i, a, b: (layer, 0, 0)),
            pl.BlockSpec((None, 1, 2 * D_MODEL), lambda i, a, b: (layer, 0, 0)),
            pl.BlockSpec((None, D_MODEL, D_MODEL), lambda i, a, b: (layer, 0, 0)),
            pl.BlockSpec((None, 1, D_MODEL), lambda i, a, b: (layer, 0, 0)),
            pl.BlockSpec((None, N_EXPERTS, D_MODEL), lambda i, a, b: (layer, 0, 0)),
        ],
        out_specs=[
            pl.BlockSpec((tm, D_MODEL), lambda i, a, b: (i, 0)),
            pl.BlockSpec((tm, XW), lambda i, a, b: (i, 0)),
            pl.BlockSpec((N_EXPERTS, tm), lambda i, a, b: (0, i)),
        ],
        scratch_shapes=[pltpu.VMEM((tm + 2 * SUBLANES, SC_WIDTH), F32)],
    )
    return pl.pallas_call(
        _merge_body,
        grid_spec=grid_spec,
        out_shape=[jax.ShapeDtypeStruct((t, D_MODEL), F32), jax.ShapeDtypeStruct((t, XW), BF16),
                   jax.ShapeDtypeStruct((N_EXPERTS, t), F32)],
        compiler_params=_cparams(("arbitrary",)),
        name="merge",
    )(flags[0], flags[1], x, o_ssd, proj, proj, proj, proj, proj, proj, proj, proj,
      sc_w, w_sc, gate_b, w_o, norm2, router_t)


def _route_body(p_ref, dest_ref, posx_ref, *, cap):
    rows = p_ref.shape[1]
    r = lax.broadcasted_iota(I32, (LANES, LANES), 0)
    c = lax.broadcasted_iota(I32, (LANES, LANES), 1)
    upper = jnp.where(r <= c, 1.0, 0.0).astype(BF16)
    last_col = jnp.where(r == LANES - 1, 1.0, 0.0).astype(BF16)
    rr = lax.broadcasted_iota(I32, (rows, rows), 0)
    rc = lax.broadcasted_iota(I32, (rows, rows), 1)
    below = jnp.where(rc < rr, 1.0, 0.0).astype(BF16)

    def count(mask):
        s = jnp.sum(jnp.where(mask, 1.0, 0.0), axis=0, keepdims=True)
        return jnp.sum(s, axis=1, keepdims=True)

    def excl_prefix(mask):
        x = jnp.where(mask, 1.0, 0.0)
        incl = _dot(x.astype(BF16), upper)
        row_tot = _dot(incl.astype(BF16), last_col)
        row_off = _dot(below, row_tot.astype(BF16))
        return incl + row_off - x

    def search(i, prefix):
        bit = jnp.left_shift(jnp.int32(1), 30 - i)
        out = []
        for e in range(N_EXPERTS):
            keys = pltpu.bitcast(p_ref[e], I32)
            cand = prefix[e] | bit
            out.append(jnp.where(count(keys >= cand) >= cap, cand, prefix[e]))
        return tuple(out)

    thr = lax.fori_loop(0, 31, search, tuple(jnp.zeros((1, 1), I32) for _ in range(N_EXPERTS)))
    for e in range(N_EXPERTS):
        keys = pltpu.bitcast(p_ref[e], I32)
        gt = keys > thr[e]
        eq = keys == thr[e]
        need = cap - count(gt)
        sel = gt | (eq & (excl_prefix(eq) < need))
        pos = excl_prefix(sel).astype(I32)
        posx_ref[e] = pos
        dest_ref[e] = jnp.where(sel, pos, -1)


def _route(probs3, cap):
    e, rows, _ = probs3.shape
    return pl.pallas_call(
        functools.partial(_route_body, cap=cap),
        out_shape=[jax.ShapeDtypeStruct((e, rows, LANES), I32), jax.ShapeDtypeStruct((e, rows, LANES), I32)],
        compiler_params=pltpu.CompilerParams(vmem_limit_bytes=VMEM_LIMIT),
        name="route",
    )(probs3)


def _block_counts(posx_list, st):
    rb = st.tk // LANES
    c0s, c1s = [], []
    for g, posx in enumerate(posx_list):
        c0 = posx[:, ::rb, 0]
        c1 = jnp.concatenate([c0[:, 1:], jnp.full((N_EXPERTS, 1), st.cap_g[g], I32)], axis=1)
        c0s.append(c0.T.reshape(-1))
        c1s.append(c1.T.reshape(-1))
    return jnp.concatenate(c0s), jnp.concatenate(c1s)


def _pass_window(c0_s, c1_s, rs_s, kb, e, k):
    c0 = c0_s[kb * N_EXPERTS + e]
    c1 = c1_s[kb * N_EXPERTS + e]
    lo = jnp.minimum(c0 + k * PASS, c1)
    base = (lo // BF16_ROWS) * BF16_ROWS
    row = pl.multiple_of(rs_s[kb * N_EXPERTS + e] + base, BF16_ROWS)
    return lo, base, row, jnp.minimum(lo + PASS, c1)


def _num_passes(c0_s, c1_s, kb):
    most = c1_s[kb * N_EXPERTS] - c0_s[kb * N_EXPERTS]
    for e in range(1, N_EXPERTS):
        most = jnp.maximum(most, c1_s[kb * N_EXPERTS + e] - c0_s[kb * N_EXPERTS + e])
    return (most + PASS - 1) // PASS


def _window_one_hot(c0_s, c1_s, rs_s, dest_ref, kb, k):
    tk = dest_ref.shape[1]
    slot = lax.broadcasted_iota(I32, (WIN, tk), 0)
    parts = []
    for e in range(N_EXPERTS):
        lo, base, _, _ = _pass_window(c0_s, c1_s, rs_s, kb, e, k)
        d = dest_ref[e:e + 1, :]
        rel = jnp.where(d >= lo, jnp.where(d < lo + PASS, d - base, -1), -1)
        parts.append(jnp.where(rel == slot, 1.0, 0.0).astype(BF16))
    return jnp.concatenate(parts, axis=0)


def _dispatch_body(c0_s, c1_s, rs_s, gfirst_s, cap_s, dest_ref, h_ref, xe_hbm, stage, carry, zeros, sem, cnt):
    kb = pl.program_id(0)

    def window_copy(slot, e, row):
        return pltpu.make_async_copy(stage.at[slot, pl.ds(e * WIN, WIN), :], xe_hbm.at[pl.ds(row, WIN), :], sem.at[0])

    def wait_all():
        for e in range(N_EXPERTS):
            window_copy(0, e, 0).wait()

    def pad_copy(e):
        row = pl.multiple_of(rs_s[kb * N_EXPERTS + e] + cap_s[kb], BF16_ROWS)
        return pltpu.make_async_copy(zeros, xe_hbm.at[pl.ds(row, zeros.shape[0]), :], sem.at[1])

    @pl.when(kb == 0)
    def _():
        cnt[0] = 0

    @pl.when(gfirst_s[kb] == 1)
    def _():
        carry[...] = jnp.zeros_like(carry)
        zeros[...] = jnp.zeros_like(zeros)
        for e in range(N_EXPERTS):
            pad_copy(e).start()
        for e in range(N_EXPERTS):
            pad_copy(e).wait()

    def one_pass(k, _):
        slot = cnt[0] % 2
        packed = _dot(_window_one_hot(c0_s, c1_s, rs_s, dest_ref, kb, k), h_ref[...])
        rows = []
        for e in range(N_EXPERTS):
            lo, base, row, nxt = _pass_window(c0_s, c1_s, rs_s, kb, e, k)
            head = packed[e * WIN:e * WIN + BF16_ROWS, :] + carry[e].astype(F32)
            stage[slot, e * WIN:e * WIN + BF16_ROWS, :] = head.astype(BF16)
            stage[slot, e * WIN + BF16_ROWS:(e + 1) * WIN, :] = packed[e * WIN + BF16_ROWS:(e + 1) * WIN, :].astype(BF16)
            off = pl.multiple_of(e * WIN + (nxt // BF16_ROWS) * BF16_ROWS - base, BF16_ROWS)
            carry[e] = stage[slot, pl.ds(off, BF16_ROWS), :]
            rows.append(row)

        @pl.when(cnt[0] > 0)
        def _():
            wait_all()

        for e in range(N_EXPERTS):
            window_copy(slot, e, rows[e]).start()
        cnt[0] = cnt[0] + 1
        return 0

    lax.fori_loop(0, _num_passes(c0_s, c1_s, kb), one_pass, 0)

    @pl.when((kb == pl.num_programs(0) - 1) & (cnt[0] > 0))
    def _():
        wait_all()


def _dispatch(c0, c1, rs, gfirst, cap, dest, h2a, st):
    tk = st.tk
    grid_spec = pltpu.PrefetchScalarGridSpec(
        num_scalar_prefetch=5,
        grid=(st.total // tk,),
        in_specs=[
            pl.BlockSpec((N_EXPERTS, tk), lambda i, *_: (0, i)),
            pl.BlockSpec((tk, XW), lambda i, *_: (i, 0)),
        ],
        out_specs=pl.BlockSpec(memory_space=pl.ANY),
        scratch_shapes=[
            pltpu.VMEM((2, N_EXPERTS * WIN, XW), BF16),
            pltpu.VMEM((N_EXPERTS, BF16_ROWS, XW), BF16),
            pltpu.VMEM((st.ts, XW), BF16),
            pltpu.SemaphoreType.DMA((2,)),
            pltpu.SMEM((1,), I32),
        ],
    )
    return pl.pallas_call(
        _dispatch_body,
        grid_spec=grid_spec,
        out_shape=jax.ShapeDtypeStruct((st.slot_rows, XW), BF16),
        compiler_params=_cparams(("arbitrary",)),
        name="moe_dispatch",
    )(c0, c1, rs, gfirst, cap, dest, h2a)


def _moe_ffn_body(pad_s, src_s, xe_ref, wg_ref, wu_ref, wd_ref, o_ref, *, tiles_per_expert):
    i = pl.program_id(0)

    @pl.when(pad_s[i] == 1)
    def _():
        o_ref[...] = jnp.zeros_like(o_ref)

    @pl.when(pad_s[i] == 0)
    def _():
        e = i // tiles_per_expert
        xe = xe_ref[:, :D_MODEL]
        terms = xe_ref[:, D_MODEL:].astype(F32)
        lane = lax.broadcasted_iota(I32, terms.shape, 1)
        mine = (lane < 3 * GATE_STRIDE) & (lane % GATE_STRIDE == e)
        gate = jnp.sum(jnp.where(mine, terms, 0.0), axis=1, keepdims=True)
        a = _dot(xe, wg_ref[...])
        u = _dot(xe, wu_ref[...])
        act = (a * _sigmoid(a) * u).astype(BF16)
        o_ref[...] = (_dot(act, wd_ref[...]) * gate).astype(BF16)


def _moe_ffn(xe, wg, wu, wd, layer, st):
    ts = st.ts
    tpe = st.tiles_per_expert
    pad, src = st.tile_tables()
    grid_spec = pltpu.PrefetchScalarGridSpec(
        num_scalar_prefetch=2,
        grid=(N_EXPERTS * tpe,),
        in_specs=[
            pl.BlockSpec((ts, XW), lambda i, p, s: (s[i], 0)),
            pl.BlockSpec((None, None, D_MODEL, EXPERT_FF), lambda i, p, s: (layer, i // tpe, 0, 0)),
            pl.BlockSpec((None, None, D_MODEL, EXPERT_FF), lambda i, p, s: (layer, i // tpe, 0, 0)),
            pl.BlockSpec((None, None, EXPERT_FF, D_MODEL), lambda i, p, s: (layer, i // tpe, 0, 0)),
        ],
        out_specs=pl.BlockSpec((ts, D_MODEL), lambda i, p, s: (i, 0)),
    )
    return pl.pallas_call(
        functools.partial(_moe_ffn_body, tiles_per_expert=tpe),
        grid_spec=grid_spec,
        out_shape=jax.ShapeDtypeStruct((st.slot_rows, D_MODEL), BF16),
        compiler_params=_cparams(("arbitrary",)),
        name="moe_ffn",
    )(pad, src, xe, wg, wu, wd)


OVERFLOW_SLOT = 2


def _combine_body(c0_s, c1_s, rs_s, dest_ref, x_ref, ye_hbm, o_ref, buf, sem):
    kb = pl.program_id(0)
    nkb = pl.num_programs(0)

    def window_copy(slot, e, row):
        return pltpu.make_async_copy(ye_hbm.at[pl.ds(row, WIN), :], buf.at[slot, pl.ds(e * WIN, WIN), :], sem.at[slot])

    def fetch(block, k, slot):
        for e in range(N_EXPERTS):
            window_copy(slot, e, _pass_window(c0_s, c1_s, rs_s, block, e, k)[2]).start()

    def wait_all(slot):
        for e in range(N_EXPERTS):
            window_copy(slot, e, 0).wait()

    @pl.when(kb == 0)
    def _():
        fetch(0, 0, 0)

    slot = kb % 2
    wait_all(slot)

    @pl.when(kb + 1 < nkb)
    def _():
        fetch(kb + 1, 0, 1 - slot)

    o_ref[...] = x_ref[...] + _dot_tn(_window_one_hot(c0_s, c1_s, rs_s, dest_ref, kb, 0), buf[slot])

    def extra_pass(k, _):
        fetch(kb, k, OVERFLOW_SLOT)
        wait_all(OVERFLOW_SLOT)
        o_ref[...] += _dot_tn(_window_one_hot(c0_s, c1_s, rs_s, dest_ref, kb, k), buf[OVERFLOW_SLOT])
        return 0

    lax.fori_loop(1, _num_passes(c0_s, c1_s, kb), extra_pass, 0)


def _combine(c0, c1, rs, dest, x1, ye, st):
    tk = st.tk
    grid_spec = pltpu.PrefetchScalarGridSpec(
        num_scalar_prefetch=3,
        grid=(st.total // tk,),
        in_specs=[
            pl.BlockSpec((N_EXPERTS, tk), lambda i, *_: (0, i)),
            pl.BlockSpec((tk, D_MODEL), lambda i, *_: (i, 0)),
            pl.BlockSpec(memory_space=pl.ANY),
        ],
        out_specs=pl.BlockSpec((tk, D_MODEL), lambda i, *_: (i, 0)),
        scratch_shapes=[
            pltpu.VMEM((3, N_EXPERTS * WIN, D_MODEL), BF16),
            pltpu.SemaphoreType.DMA((3,)),
        ],
    )
    return pl.pallas_call(
        _combine_body,
        grid_spec=grid_spec,
        out_shape=jax.ShapeDtypeStruct((st.total, D_MODEL), F32),
        compiler_params=_cparams(("arbitrary",)),
        name="moe_combine",
    )(c0, c1, rs, dest, x1, ye)


def _final_norm_body(x_ref, w_ref, o_ref):
    x = x_ref[...]
    o_ref[...] = x * lax.rsqrt(jnp.mean(x * x, axis=-1, keepdims=True) + EPS) * w_ref[...]


def _final_norm(x, w, st, group):
    t, tm = st.t_g[group], st.tm
    first = st.off_g[group] // tm
    return pl.pallas_call(
        _final_norm_body,
        grid=(t // tm,),
        in_specs=[pl.BlockSpec((tm, D_MODEL), lambda i: (first + i, 0)), pl.BlockSpec((1, D_MODEL), lambda i: (0, 0))],
        out_specs=pl.BlockSpec((tm, D_MODEL), lambda i: (i, 0)),
        out_shape=jax.ShapeDtypeStruct((t, D_MODEL), F32),
        compiler_params=_cparams(("arbitrary",)),
        name="final_norm",
    )(x, w)


def _pad_lanes(v, width=LANES):
    return jnp.pad(v, [(0, 0)] * (v.ndim - 1) + [(0, width - v.shape[-1])])


def _trunk_stream(xs, params, tm, tq, tk, ts, tc, cps):
    (norm1_w, w_in, ssd_conv_w, ssd_conv_b, ssd_dt_bias, ssd_a_log, ssd_d, ssd_norm_w, w_ssd_out, sc_conv_w,
     w_sc_out, gate_b, w_o, norm2_w, router_w, w_gate, w_up, w_down, final_norm_w) = params
    depth = w_in.shape[0]
    st = _Stream([(x.shape[0], x.shape[1]) for x in xs], tm, tq, tk, ts, tc, cps)
    x = jnp.concatenate([v.reshape(-1, D_MODEL) for v in xs], axis=0)

    w_main = jnp.concatenate([w_in[:, :, :_OFF_DT], w_in[:, :, _OFF_GATE:], w_in[:, :, _OFF_SCB:_OFF_GATE]],
                             axis=2).astype(BF16)
    w_dt = _pad_lanes(w_in[:, :, _OFF_DT:_OFF_SCB]).astype(BF16)
    dtb = _pad_lanes(ssd_dt_bias.reshape(depth, 1, 2 * HEADS))
    alog = _pad_lanes(ssd_a_log.reshape(depth, 1, 2 * HEADS))
    dskip = jnp.repeat(ssd_d, HEAD_DIM, axis=-1).reshape(depth, 1, INNER)
    ef, eb = _expansion_matrices()
    router_t = jnp.swapaxes(router_w, 1, 2)
    wg, wu, wd = w_gate.astype(BF16), w_up.astype(BF16), w_down.astype(BF16)
    row = lambda v: v.reshape(depth, 1, v.shape[-1])

    chunk_first, chunk_last = st.seq_flags(CHUNK)
    tile_flags = st.seq_flags(tq)
    conv_flags = st.seq_flags(tc)
    region_start, group_first, group_cap = st.block_tables()

    for l in range(depth):
        proj, dt_raw = _inproj(x, row(norm1_w), w_main, w_dt, l, st)
        xbc = _conv_xbc(proj, ssd_conv_w, row(ssd_conv_b), l, st, conv_flags)
        y1 = _ssd_fwd(xbc, dt_raw, dtb, alog, ef, l, st, chunk_first)
        o_ssd = _ssd_bwd(xbc, dt_raw, y1, proj, dtb, alog, eb, dskip, row(ssd_norm_w),
                         w_ssd_out.astype(BF16), l, st, chunk_last)
        x1, h2, probs_t = _merge(x, o_ssd, proj, sc_conv_w, w_sc_out.astype(BF16), row(gate_b), w_o.astype(BF16),
                                 row(norm2_w), router_t, l, st, tile_flags)
        dests, posxs = [], []
        for g in range(len(xs)):
            pg = probs_t[:, st.off_g[g]:st.off_g[g] + st.t_g[g]].reshape(N_EXPERTS, st.t_g[g] // LANES, LANES)
            dest, posx = _route(pg, st.cap_g[g])
            dests.append(dest.reshape(N_EXPERTS, st.t_g[g]))
            posxs.append(posx)
        dest = jnp.concatenate(dests, axis=1)
        c0, c1 = _block_counts(posxs, st)
        xe = _dispatch(c0, c1, region_start, group_first, group_cap, dest, h2, st)
        ye = _moe_ffn(xe, wg, wu, wd, l, st)
        x = _combine(c0, c1, region_start, dest, x1, ye, st)

    fw = final_norm_w.reshape(1, D_MODEL)
    return [_final_norm(x, fw, st, g).reshape(v.shape) for g, v in enumerate(xs)]


def kernel(x_prompt, x_sample, norm1_w, w_in, ssd_conv_w, ssd_conv_b, ssd_dt_bias, ssd_a_log, ssd_d, ssd_norm_w, w_ssd_out, sc_conv_w, w_sc_out, gate_b, w_o, norm2_w, router_w, w_gate, w_up, w_down, final_norm_w):
    params = (norm1_w, w_in, ssd_conv_w, ssd_conv_b, ssd_dt_bias, ssd_a_log, ssd_d, ssd_norm_w, w_ssd_out, sc_conv_w,
              w_sc_out, gate_b, w_o, norm2_w, router_w, w_gate, w_up, w_down, final_norm_w)
    y_prompt, y_sample = _trunk_stream([x_prompt, x_sample], params, tm=1024, tq=512, tk=256, ts=512, tc=1024, cps=4)
    return (y_prompt, y_sample)
```

```python
import functools

import numpy as np
import jax
import jax.numpy as jnp
from jax import lax
from jax.experimental import pallas as pl
from jax.experimental.pallas import tpu as pltpu

F32 = jnp.float32
BF16 = jnp.bfloat16
I32 = jnp.int32

D_MODEL = 1024
HEADS = 24
HEAD_DIM = 64
INNER = HEADS * HEAD_DIM
GROUPS = 4
GROUP_W = INNER // GROUPS
STATE = 128
CHUNK = 128
SSD_CONV = 5
SC_WIDTH = 512
SC_CONV = 3
N_EXPERTS = 16
EXPERT_FF = 2048
CAPACITY_FACTOR = 2
EPS = 1e-6
LANES = 128
SUBLANES = 8

_OFF_XBC = INNER
_OFF_DT = _OFF_XBC + INNER + 2 * GROUPS * STATE
_OFF_SCB = _OFF_DT + 2 * HEADS
_OFF_GATE = _OFF_SCB + 3 * SC_WIDTH
_N_IN = _OFF_GATE + 2 * D_MODEL
COL_X = INNER
COL_B = 2 * INNER
COL_C = COL_B + GROUPS * STATE
COL_GATE = COL_C + GROUPS * STATE
COL_SCB = COL_GATE + 2 * D_MODEL
N_MAIN = COL_SCB + 3 * SC_WIDTH
CB = 512

BF16_ROWS = 2 * SUBLANES
PASS = 48
WIN = PASS + BF16_ROWS
XW = D_MODEL + LANES
GATE_STRIDE = N_EXPERTS

VMEM_LIMIT = 50 * 1024 * 1024


def _cparams(sem):
    return pltpu.CompilerParams(dimension_semantics=sem, vmem_limit_bytes=VMEM_LIMIT)


def _sigmoid(v):
    return 0.5 * jnp.tanh(0.5 * v) + 0.5


def _split3(v):
    hi = v.astype(BF16)
    r = v - hi.astype(F32)
    mid = r.astype(BF16)
    lo = (r - mid.astype(F32)).astype(BF16)
    return hi, mid, lo


def _dot(a, b):
    return jnp.dot(a, b, preferred_element_type=F32)


def _dot_nt(a, b):
    return lax.dot_general(a, b, (((1,), (1,)), ((), ())), preferred_element_type=F32)


def _dot_tn(a, b):
    return lax.dot_general(a, b, (((0,), (0,)), ((), ())), preferred_element_type=F32)


def _dot3_l(a_f32, b_exact):
    hi, mid, lo = _split3(a_f32)
    return _dot(hi, b_exact) + _dot(mid, b_exact) + _dot(lo, b_exact)


def _dot3_r(a_exact, b_f32):
    hi, mid, lo = _split3(b_f32)
    return _dot(a_exact, hi) + _dot(a_exact, mid) + _dot(a_exact, lo)


class _Stream:
    def __init__(self, groups, tm, tq, tk, ts, tc, cps):
        self.groups = tuple(groups)
        self.tm, self.tq, self.tk, self.ts, self.tc, self.cps = tm, tq, tk, ts, tc, cps
        self.t_g = [n * l for n, l in self.groups]
        self.off_g = [int(v) for v in np.cumsum([0] + self.t_g[:-1])]
        self.total = int(sum(self.t_g))
        self.cap_g = [CAPACITY_FACTOR * t // N_EXPERTS for t in self.t_g]
        self.nkb_g = [t // tk for t in self.t_g]
        for (n, l), t, c in zip(self.groups, self.t_g, self.cap_g):
            assert l % (CHUNK * cps) == 0 and l % tq == 0 and l % tc == 0 and t % tk == 0 and c % ts == 0
            assert t % tm == 0 and tk % LANES == 0 and tc % CONV_ROWS == 0
        assert ts >= WIN
        self.region_g = [c + ts for c in self.cap_g]
        self.rows_per_expert = int(sum(self.region_g))
        self.tiles_per_expert = self.rows_per_expert // ts
        self.slot_rows = N_EXPERTS * self.rows_per_expert
        self.region_off_g = [int(v) for v in np.cumsum([0] + self.region_g[:-1])]

    def block_tables(self):
        rs, gfirst, cap = [], [], []
        for g, nkb in enumerate(self.nkb_g):
            for kb in range(nkb):
                gfirst.append(1 if kb == 0 else 0)
                cap.append(self.cap_g[g])
                rs.extend(e * self.rows_per_expert + self.region_off_g[g] for e in range(N_EXPERTS))
        return tuple(jnp.asarray(np.asarray(v, np.int32)) for v in (rs, gfirst, cap))

    def tile_tables(self):
        pad, src = [], []
        for e in range(N_EXPERTS):
            for g, c in enumerate(self.cap_g):
                n = c // self.ts
                base = len(pad)
                pad.extend([0] * n + [1])
                src.extend(list(range(base, base + n)) + [base + n - 1])
        return jnp.asarray(np.asarray(pad, np.int32)), jnp.asarray(np.asarray(src, np.int32))

    def seq_flags(self, tile):
        n_tiles = self.total // tile
        first = np.zeros((n_tiles,), np.int32)
        last = np.zeros((n_tiles,), np.int32)
        for (n, l), off in zip(self.groups, self.off_g):
            for s in range(n):
                first[(off + s * l) // tile] = 1
                last[(off + (s + 1) * l) // tile - 1] = 1
        return jnp.asarray(first), jnp.asarray(last)


def _inproj_body(x_ref, nw_ref, w_ref, wdt_ref, proj_ref, dt_ref, h_scr):
    @pl.when(pl.program_id(1) == 0)
    def _():
        x = x_ref[...]
        ms = jnp.mean(x * x, axis=-1, keepdims=True)
        h = (x * lax.rsqrt(ms + EPS) * nw_ref[...]).astype(BF16)
        h_scr[...] = h
        dt_ref[...] = _dot(h, wdt_ref[...])

    proj_ref[...] = _dot(h_scr[...], w_ref[...]).astype(BF16)


def _inproj(x, nw, w_main, w_dt, layer, st):
    t, tm = st.total, st.tm
    tn = N_MAIN // 3
    return pl.pallas_call(
        _inproj_body,
        grid=(t // tm, N_MAIN // tn),
        in_specs=[
            pl.BlockSpec((tm, D_MODEL), lambda i, n: (i, 0)),
            pl.BlockSpec((None, 1, D_MODEL), lambda i, n: (layer, 0, 0)),
            pl.BlockSpec((None, D_MODEL, tn), lambda i, n: (layer, 0, n)),
            pl.BlockSpec((None, D_MODEL, LANES), lambda i, n: (layer, 0, 0)),
        ],
        out_specs=[
            pl.BlockSpec((tm, tn), lambda i, n: (i, n)),
            pl.BlockSpec((tm, LANES), lambda i, n: (i, 0)),
        ],
        out_shape=[jax.ShapeDtypeStruct((t, N_MAIN), BF16), jax.ShapeDtypeStruct((t, LANES), F32)],
        scratch_shapes=[pltpu.VMEM((tm, D_MODEL), BF16)],
        compiler_params=_cparams(("arbitrary", "arbitrary")),
        name="inproj",
    )(x, nw, w_main, w_dt)


CONV_ROWS = 128


def _conv_body(topz, botz, main_ref, prev_ref, next_ref, w_ref, b_ref, o_ref, ext):
    i = pl.program_id(0)
    tq = main_ref.shape[0]
    halo = BF16_ROWS
    pad = SSD_CONV // 2
    zero = jnp.zeros_like(prev_ref[...])
    ext[0:halo, :] = jnp.where(topz[i] == 1, zero, prev_ref[...])
    ext[halo:halo + tq, :] = main_ref[...]
    ext[halo + tq:2 * halo + tq, :] = jnp.where(botz[i] == 1, zero, next_ref[...])
    win = CONV_ROWS + 2 * halo
    taps = [k for k in range(SSD_CONV) if k != pad]
    r = lax.broadcasted_iota(I32, (CONV_ROWS, win), 0)
    c = lax.broadcasted_iota(I32, (CONV_ROWS, win), 1)
    shift = jnp.concatenate([jnp.where(c == r + halo + k - pad, 1.0, 0.0).astype(BF16) for k in taps], axis=0)
    for sb in range(tq // CONV_ROWS):
        r0 = sb * CONV_ROWS
        moved = _dot(shift, ext[r0:r0 + win, :])
        acc = b_ref[...] + ext[halo + r0:halo + r0 + CONV_ROWS, :].astype(F32) * w_ref[pad:pad + 1, :]
        for n, k in enumerate(taps):
            acc = acc + moved[n * CONV_ROWS:(n + 1) * CONV_ROWS, :] * w_ref[k:k + 1, :]
        o_ref[r0:r0 + CONV_ROWS, :] = (acc * _sigmoid(acc)).astype(BF16)


def _conv_xbc(proj, conv_w, conv_b, layer, st, flags):
    t, tq = st.total, st.tc
    ncb = (COL_GATE - COL_X) // CB
    cb0 = COL_X // CB
    rpt = tq // BF16_ROWS
    last_rb = t // BF16_ROWS - 1
    grid_spec = pltpu.PrefetchScalarGridSpec(
        num_scalar_prefetch=2,
        grid=(t // tq, ncb),
        in_specs=[
            pl.BlockSpec((tq, CB), lambda i, j, a, b: (i, cb0 + j)),
            pl.BlockSpec((BF16_ROWS, CB), lambda i, j, a, b: (jnp.maximum(i * rpt - 1, 0), cb0 + j)),
            pl.BlockSpec((BF16_ROWS, CB), lambda i, j, a, b: (jnp.minimum((i + 1) * rpt, last_rb), cb0 + j)),
            pl.BlockSpec((None, SSD_CONV, CB), lambda i, j, a, b: (layer, 0, j)),
            pl.BlockSpec((None, 1, CB), lambda i, j, a, b: (layer, 0, j)),
        ],
        out_specs=pl.BlockSpec((tq, CB), lambda i, j, a, b: (i, j)),
        scratch_shapes=[pltpu.VMEM((tq + 2 * BF16_ROWS, CB), BF16)],
    )
    return pl.pallas_call(
        _conv_body,
        grid_spec=grid_spec,
        out_shape=jax.ShapeDtypeStruct((t, COL_GATE - COL_X), BF16),
        compiler_params=_cparams(("arbitrary", "arbitrary")),
        name="conv_xbc",
    )(flags[0], flags[1], proj, proj, proj, conv_w, conv_b)


def _ssd_scalars(dt_ref, rows, dtb_ref, alog_ref):
    v = dt_ref[rows, :] + dtb_ref[...]
    u = jnp.exp(-jnp.abs(v))
    w = 1.0 + u
    dt = jnp.maximum(v, 0.0) + jnp.where(w == 1.0, u, jnp.log(w) * (u / (w - 1.0)))
    da = dt * (-jnp.exp(alog_ref[...]))
    r = lax.broadcasted_iota(I32, (CHUNK, CHUNK), 0)
    c = lax.broadcasted_iota(I32, (CHUNK, CHUNK), 1)
    tri = jnp.where(c <= r, 1.0, 0.0).astype(BF16)
    upper = jnp.where(r <= c, 1.0, 0.0).astype(BF16)
    cs = _dot3_r(tri, da)
    da_t = da.T
    dt_t = dt.T
    cs_t = _dot3_l(da_t, upper)
    return dt, cs, cs - da, dt_t, cs_t, cs_t - da_t, r, c


def _expand(v, e_ref):
    hi = v.astype(BF16)
    mid = (v - hi.astype(F32)).astype(BF16)
    e = e_ref[...]
    return _dot(hi, e) + _dot(mid, e)


def _expand_pair(a, b, e_ref):
    out = _expand(jnp.concatenate([a, b], axis=0), e_ref)
    return out[:CHUNK, :], out[CHUNK:, :]


def _state_step(s_scr, xs_f32, w_tok, dec_lanes, b_ref, rows):
    xw = (xs_f32 * w_tok).astype(BF16)
    for g in range(GROUPS):
        sl = slice(g * GROUP_W, (g + 1) * GROUP_W)
        sc = _dot_tn(b_ref[rows, g * STATE:(g + 1) * STATE], xw[:, sl])
        s_scr[:, sl] = s_scr[:, sl] * dec_lanes[:, sl] + sc


def _ssd_fwd_body(first, xs_ref, b_ref, c_ref, dt_ref, dtb_ref, alog_ref, ef_ref, y_ref, s_scr):
    cps = xs_ref.shape[0] // CHUNK

    @pl.when(pl.program_id(0) == 0)
    def _():
        s_scr[...] = jnp.zeros_like(s_scr)

    for sub in range(cps):
        rows = slice(sub * CHUNK, (sub + 1) * CHUNK)
        keep = jnp.where(first[pl.program_id(0) * cps + sub] == 1, 0.0, 1.0)
        dt, cs, ex, dt_t, cs_t, ex_t, r, c = _ssd_scalars(dt_ref, rows, dtb_ref, alog_ref)
        w_out, w_st = _expand_pair(jnp.exp(cs) * keep, jnp.exp(cs[CHUNK - 1:CHUNK, :] - cs) * dt, ef_ref)
        for g in range(GROUPS):
            sl = slice(g * GROUP_W, (g + 1) * GROUP_W)
            yo = _dot(c_ref[rows, g * STATE:(g + 1) * STATE], s_scr[:, sl].astype(BF16))
            y_ref[rows, sl] = yo * w_out[:, sl]
        lane = lax.broadcasted_iota(I32, (CHUNK, 2 * HEAD_DIM), 1)
        lower = r >= c
        pairs_per_group = HEADS // GROUPS // 2
        for pi in range(HEADS // 2):
            g = pi // pairs_per_group
            if pi % pairs_per_group == 0:
                cb = _dot_nt(c_ref[rows, g * STATE:(g + 1) * STATE], b_ref[rows, g * STATE:(g + 1) * STATE])
            ms = []
            for h in (2 * pi, 2 * pi + 1):
                hb = HEADS + h
                arg = jnp.where(lower, cs[:, h:h + 1] - cs_t[h:h + 1, :], ex_t[hb:hb + 1, :] - ex[:, hb:hb + 1])
                d0 = dt_t[h:h + 1, :]
                d1 = dt_t[hb:hb + 1, :]
                w = jnp.where(r > c, d0, jnp.where(r < c, d1, d0 + d1))
                ms.append((jnp.exp(arg) * w * cb).astype(BF16))
            psl = slice(pi * 2 * HEAD_DIM, (pi + 1) * 2 * HEAD_DIM)
            xp = xs_ref[rows, psl]
            zero = jnp.zeros_like(xp)
            rhs = jnp.concatenate([jnp.where(lane < HEAD_DIM, xp, zero), jnp.where(lane >= HEAD_DIM, xp, zero)],
                                  axis=0)
            y_ref[rows, psl] = y_ref[rows, psl] + _dot(jnp.concatenate(ms, axis=1), rhs)
        _state_step(s_scr, xs_ref[rows, :].astype(F32), w_st, w_out[CHUNK - 1:CHUNK, :], b_ref, rows)


def _ssd_bwd_body(last, xs_ref, b_ref, c_ref, dt_ref, y1_ref, z_ref, dtb_ref, alog_ref, eb_ref,
                  dskip_ref, nw_ref, wout_ref, o_ref, s_scr, yn_scr):
    cps = xs_ref.shape[0] // CHUNK
    step = pl.num_programs(0) - 1 - pl.program_id(0)

    @pl.when(pl.program_id(0) == 0)
    def _():
        s_scr[...] = jnp.zeros_like(s_scr)

    for sub in reversed(range(cps)):
        rows = slice(sub * CHUNK, (sub + 1) * CHUNK)
        keep = jnp.where(last[step * cps + sub] == 1, 0.0, 1.0)
        dt, cs, ex, dt_t, cs_t, ex_t, r, c = _ssd_scalars(dt_ref, rows, dtb_ref, alog_ref)
        tot = cs[CHUNK - 1:CHUNK, :]
        w_out, w_st = _expand_pair(jnp.exp(tot - ex) * keep, jnp.exp(ex) * dt, eb_ref)
        xs = xs_ref[rows, :].astype(F32)
        z = z_ref[rows, :].astype(F32)
        zs = z * _sigmoid(z)
        for g in range(GROUPS):
            sl = slice(g * GROUP_W, (g + 1) * GROUP_W)
            yo = _dot(c_ref[rows, g * STATE:(g + 1) * STATE], s_scr[:, sl].astype(BF16))
            y = (y1_ref[rows, sl] + yo * w_out[:, sl] + xs[:, sl] * dskip_ref[:, sl]) * zs[:, sl]
            y = y * lax.rsqrt(jnp.mean(y * y, axis=-1, keepdims=True) + EPS)
            yn_scr[rows, sl] = (y * nw_ref[:, sl]).astype(BF16)
        _state_step(s_scr, xs, w_st, w_out[0:1, :], b_ref, rows)
    o_ref[...] = _dot(yn_scr[...], wout_ref[...]).astype(BF16)


def _expansion_matrices():
    rows = np.arange(LANES)[:, None]
    cols = np.arange(INNER)[None, :] // HEAD_DIM
    ef = (rows == cols).astype(np.float32)
    eb = (rows == cols + HEADS).astype(np.float32)
    return jnp.asarray(ef, BF16), jnp.asarray(eb, BF16)


def _ssd_fwd(xbc, dt_raw, dtb, alog, ef, layer, st, first):
    t = st.total
    nb = GROUPS * STATE // CB
    rows = st.cps * CHUNK
    grid_spec = pltpu.PrefetchScalarGridSpec(
        num_scalar_prefetch=1,
        grid=(t // rows,),
        in_specs=[
            pl.BlockSpec((rows, INNER), lambda i, f: (i, 0)),
            pl.BlockSpec((rows, CB), lambda i, f: (i, INNER // CB)),
            pl.BlockSpec((rows, CB), lambda i, f: (i, INNER // CB + nb)),
            pl.BlockSpec((rows, LANES), lambda i, f: (i, 0)),
            pl.BlockSpec((None, 1, LANES), lambda i, f: (layer, 0, 0)),
            pl.BlockSpec((None, 1, LANES), lambda i, f: (layer, 0, 0)),
            pl.BlockSpec((LANES, INNER), lambda i, f: (0, 0)),
        ],
        out_specs=pl.BlockSpec((rows, INNER), lambda i, f: (i, 0)),
        scratch_shapes=[pltpu.VMEM((STATE, INNER), F32)],
    )
    return pl.pallas_call(
        _ssd_fwd_body,
        grid_spec=grid_spec,
        out_shape=jax.ShapeDtypeStruct((t, INNER), F32),
        compiler_params=_cparams(("arbitrary",)),
        name="ssd_fwd",
    )(first, xbc, xbc, xbc, dt_raw, dtb, alog, ef)


def _ssd_bwd(xbc, dt_raw, y1, proj, dtb, alog, eb, dskip, norm_w, w_out, layer, st, last):
    t = st.total
    rows = st.cps * CHUNK
    nc = t // rows
    nb = GROUPS * STATE // CB
    rev = lambda i: nc - 1 - i
    grid_spec = pltpu.PrefetchScalarGridSpec(
        num_scalar_prefetch=1,
        grid=(nc,),
        in_specs=[
            pl.BlockSpec((rows, INNER), lambda i, f: (rev(i), 0)),
            pl.BlockSpec((rows, CB), lambda i, f: (rev(i), INNER // CB)),
            pl.BlockSpec((rows, CB), lambda i, f: (rev(i), INNER // CB + nb)),
            pl.BlockSpec((rows, LANES), lambda i, f: (rev(i), 0)),
            pl.BlockSpec((rows, INNER), lambda i, f: (rev(i), 0)),
            pl.BlockSpec((rows, INNER), lambda i, f: (rev(i), 0)),
            pl.BlockSpec((None, 1, LANES), lambda i, f: (layer, 0, 0)),
            pl.BlockSpec((None, 1, LANES), lambda i, f: (layer, 0, 0)),
            pl.BlockSpec((LANES, INNER), lambda i, f: (0, 0)),
            pl.BlockSpec((None, 1, INNER), lambda i, f: (layer, 0, 0)),
            pl.BlockSpec((None, 1, INNER), lambda i, f: (layer, 0, 0)),
            pl.BlockSpec((None, INNER, D_MODEL), lambda i, f: (layer, 0, 0)),
        ],
        out_specs=pl.BlockSpec((rows, D_MODEL), lambda i, f: (rev(i), 0)),
        scratch_shapes=[pltpu.VMEM((STATE, INNER), F32), pltpu.VMEM((rows, INNER), BF16)],
    )
    return pl.pallas_call(
        _ssd_bwd_body,
        grid_spec=grid_spec,
        out_shape=jax.ShapeDtypeStruct((t, D_MODEL), BF16),
        compiler_params=_cparams(("arbitrary",)),
        name="ssd_bwd",
    )(last, xbc, xbc, xbc, dt_raw, y1, proj, dtb, alog, eb, dskip, norm_w, w_out)


def _merge_body(topz, botz, x_ref, ossd_ref, gate_ref, scb_ref, scc_ref, scx_ref, ccp_ref, cxp_ref, ccn_ref, cxn_ref,
                cw_ref, wsc_ref, gb_ref, wo_ref, nw_ref, rw_ref, x1_ref, h2_ref, pt_ref, ext):
    i = pl.program_id(0)
    tm = x_ref.shape[0]
    halo = SUBLANES
    pad = SC_CONV // 2
    ext[0:halo, :] = jnp.where(topz[i] == 1, 0.0, ccp_ref[...].astype(F32) * cxp_ref[...].astype(F32))
    ext[halo:halo + tm, :] = scc_ref[...].astype(F32) * scx_ref[...].astype(F32)
    ext[halo + tm:2 * halo + tm, :] = jnp.where(botz[i] == 1, 0.0, ccn_ref[...].astype(F32) * cxn_ref[...].astype(F32))
    v = ext[halo - pad:halo - pad + tm, :] * cw_ref[0:1, :]
    for k in range(1, SC_CONV):
        v = v + ext[halo - pad + k:halo - pad + k + tm, :] * cw_ref[k:k + 1, :]
    o_sc = _dot((scb_ref[...].astype(F32) * v).astype(BF16), wsc_ref[...])
    g = _sigmoid(gate_ref[...].astype(F32) + gb_ref[...])
    mix = g[:, :D_MODEL] * ossd_ref[...].astype(F32) + g[:, D_MODEL:] * o_sc
    x1 = x_ref[...] + _dot(mix.astype(BF16), wo_ref[...])
    x1_ref[...] = x1
    h2 = x1 * lax.rsqrt(jnp.mean(x1 * x1, axis=-1, keepdims=True) + EPS) * nw_ref[...]
    h2_ref[:, :D_MODEL] = h2.astype(BF16)
    hh, hm, _ = _split3(h2)
    rw = rw_ref[...]
    rh, rm, _ = _split3(rw)
    logits = _dot_nt(rh, hh) + _dot_nt(rh, hm) + _dot_nt(rm, hh)
    mx = jnp.max(logits, axis=0, keepdims=True)
    ex = jnp.exp(logits - mx)
    probs = ex / jnp.sum(ex, axis=0, keepdims=True)
    pt_ref[...] = probs
    pk = jnp.concatenate([probs, jnp.zeros((LANES - N_EXPERTS, tm), F32)], axis=0).T
    hi = pk.astype(BF16).astype(F32)
    mid = (pk - hi).astype(BF16).astype(F32)
    lo = pk - hi - mid
    lane = lax.broadcasted_iota(I32, (tm, LANES), 1)
    terms = jnp.where(lane < GATE_STRIDE, hi,
                      jnp.where(lane < 2 * GATE_STRIDE, pltpu.roll(mid, GATE_STRIDE, 1),
                                jnp.where(lane < 3 * GATE_STRIDE, pltpu.roll(lo, 2 * GATE_STRIDE, 1), 0.0)))
    h2_ref[:, D_MODEL:] = terms.astype(BF16)


def _merge(x, o_ssd, proj, sc_w, w_sc, gate_b, w_o, norm2, router_t, layer, st, flags):
    t, tm = st.total, st.tq
    rpt = tm // SUBLANES
    last_rb = t // SUBLANES - 1
    cscb, cscc, cscx = COL_SCB // CB, COL_SCB // CB + 1, COL_SCB // CB + 2
    prev = lambda i: jnp.maximum(i * rpt - 1, 0)
    nxt = lambda i: jnp.minimum((i + 1) * rpt, last_rb)
    grid_spec = pltpu.PrefetchScalarGridSpec(
        num_scalar_prefetch=2,
        grid=(t // tm,),
        in_specs=[
            pl.BlockSpec((tm, D_MODEL), lambda i, a, b: (i, 0)),
            pl.BlockSpec((tm, D_MODEL), lambda i, a, b: (i, 0)),
            pl.BlockSpec((tm, 2 * D_MODEL), lambda i, a, b: (i, COL_GATE // (2 * D_MODEL))),
            pl.BlockSpec((tm, CB), lambda i, a, b: (i, cscb)),
            pl.BlockSpec((tm, CB), lambda i, a, b: (i, cscc)),
            pl.BlockSpec((tm, CB), lambda i, a, b: (i, cscx)),
            pl.BlockSpec((SUBLANES, CB), lambda i, a, b: (prev(i), cscc)),
            pl.BlockSpec((SUBLANES, CB), lambda i, a, b: (prev(i), cscx)),
            pl.BlockSpec((SUBLANES, CB), lambda i, a, b: (nxt(i), cscc)),
            pl.BlockSpec((SUBLANES, CB), lambda i, a, b: (nxt(i), cscx)),
            pl.BlockSpec((None, SC_CONV, SC_WIDTH), lambda i, a, b: (layer, 0, 0)),
            pl.BlockSpec((None, SC_WIDTH, D_MODEL), lambda i, a, b: (layer, 0, 0)),
            pl.BlockSpec((None, 1, 2 * D_MODEL), lambda i, a, b: (layer, 0, 0)),
            pl.BlockSpec((None, D_MODEL, D_MODEL), lambda i, a, b: (layer, 0, 0)),
            pl.BlockSpec((None, 1, D_MODEL), lambda i, a, b: (layer, 0, 0)),
            pl.BlockSpec((None, N_EXPERTS, D_MODEL), lambda i, a, b: (layer, 0, 0)),
        ],
        out_specs=[
            pl.BlockSpec((tm, D_MODEL), lambda i, a, b: (i, 0)),
            pl.BlockSpec((tm, XW), lambda i, a, b: (i, 0)),
            pl.BlockSpec((N_EXPERTS, tm), lambda i, a, b: (0, i)),
        ],
        scratch_shapes=[pltpu.VMEM((tm + 2 * SUBLANES, SC_WIDTH), F32)],
    )
    return pl.pallas_call(
        _merge_body,
        grid_spec=grid_spec,
        out_shape=[jax.ShapeDtypeStruct((t, D_MODEL), F32), jax.ShapeDtypeStruct((t, XW), BF16),
                   jax.ShapeDtypeStruct((N_EXPERTS, t), F32)],
        compiler_params=_cparams(("arbitrary",)),
        name="merge",
    )(flags[0], flags[1], x, o_ssd, proj, proj, proj, proj, proj, proj, proj, proj,
      sc_w, w_sc, gate_b, w_o, norm2, router_t)


def _route_body(p_ref, dest_ref, posx_ref, *, cap):
    rows = p_ref.shape[1]
    r = lax.broadcasted_iota(I32, (LANES, LANES), 0)
    c = lax.broadcasted_iota(I32, (LANES, LANES), 1)
    upper = jnp.where(r <= c, 1.0, 0.0).astype(BF16)
    last_col = jnp.where(r == LANES - 1, 1.0, 0.0).astype(BF16)
    rr = lax.broadcasted_iota(I32, (rows, rows), 0)
    rc = lax.broadcasted_iota(I32, (rows, rows), 1)
    below = jnp.where(rc < rr, 1.0, 0.0).astype(BF16)

    def count(mask):
        s = jnp.sum(jnp.where(mask, 1.0, 0.0), axis=0, keepdims=True)
        return jnp.sum(s, axis=1, keepdims=True)

    def excl_prefix(mask):
        x = jnp.where(mask, 1.0, 0.0)
        incl = _dot(x.astype(BF16), upper)
        row_tot = _dot(incl.astype(BF16), last_col)
        row_off = _dot(below, row_tot.astype(BF16))
        return incl + row_off - x

    def search(i, prefix):
        bit = jnp.left_shift(jnp.int32(1), 30 - i)
        out = []
        for e in range(N_EXPERTS):
            keys = pltpu.bitcast(p_ref[e], I32)
            cand = prefix[e] | bit
            out.append(jnp.where(count(keys >= cand) >= cap, cand, prefix[e]))
        return tuple(out)

    thr = lax.fori_loop(0, 31, search, tuple(jnp.zeros((1, 1), I32) for _ in range(N_EXPERTS)))
    for e in range(N_EXPERTS):
        keys = pltpu.bitcast(p_ref[e], I32)
        gt = keys > thr[e]
        eq = keys == thr[e]
        need = cap - count(gt)
        sel = gt | (eq & (excl_prefix(eq) < need))
        pos = excl_prefix(sel).astype(I32)
        posx_ref[e] = pos
        dest_ref[e] = jnp.where(sel, pos, -1)


def _route(probs3, cap):
    e, rows, _ = probs3.shape
    return pl.pallas_call(
        functools.partial(_route_body, cap=cap),
        out_shape=[jax.ShapeDtypeStruct((e, rows, LANES), I32), jax.ShapeDtypeStruct((e, rows, LANES), I32)],
        compiler_params=pltpu.CompilerParams(vmem_limit_bytes=VMEM_LIMIT),
        name="route",
    )(probs3)


def _block_counts(posx_list, st):
    rb = st.tk // LANES
    c0s, c1s = [], []
    for g, posx in enumerate(posx_list):
        c0 = posx[:, ::rb, 0]
        c1 = jnp.concatenate([c0[:, 1:], jnp.full((N_EXPERTS, 1), st.cap_g[g], I32)], axis=1)
        c0s.append(c0.T.reshape(-1))
        c1s.append(c1.T.reshape(-1))
    return jnp.concatenate(c0s), jnp.concatenate(c1s)


def _pass_window(c0_s, c1_s, rs_s, kb, e, k):
    c0 = c0_s[kb * N_EXPERTS + e]
    c1 = c1_s[kb * N_EXPERTS + e]
    lo = jnp.minimum(c0 + k * PASS, c1)
    base = (lo // BF16_ROWS) * BF16_ROWS
    row = pl.multiple_of(rs_s[kb * N_EXPERTS + e] + base, BF16_ROWS)
    return lo, base, row, jnp.minimum(lo + PASS, c1)


def _num_passes(c0_s, c1_s, kb):
    most = c1_s[kb * N_EXPERTS] - c0_s[kb * N_EXPERTS]
    for e in range(1, N_EXPERTS):
        most = jnp.maximum(most, c1_s[kb * N_EXPERTS + e] - c0_s[kb * N_EXPERTS + e])
    return (most + PASS - 1) // PASS


def _window_one_hot(c0_s, c1_s, rs_s, dest_ref, kb, k):
    tk = dest_ref.shape[1]
    slot = lax.broadcasted_iota(I32, (WIN, tk), 0)
    parts = []
    for e in range(N_EXPERTS):
        lo, base, _, _ = _pass_window(c0_s, c1_s, rs_s, kb, e, k)
        d = dest_ref[e:e + 1, :]
        rel = jnp.where(d >= lo, jnp.where(d < lo + PASS, d - base, -1), -1)
        parts.append(jnp.where(rel == slot, 1.0, 0.0).astype(BF16))
    return jnp.concatenate(parts, axis=0)


def _dispatch_body(c0_s, c1_s, rs_s, gfirst_s, cap_s, dest_ref, h_ref, xe_hbm, stage, carry, zeros, sem, cnt):
    kb = pl.program_id(0)

    def window_copy(slot, e, row):
        return pltpu.make_async_copy(stage.at[slot, pl.ds(e * WIN, WIN), :], xe_hbm.at[pl.ds(row, WIN), :], sem.at[0])

    def wait_all():
        for e in range(N_EXPERTS):
            window_copy(0, e, 0).wait()

    def pad_copy(e):
        row = pl.multiple_of(rs_s[kb * N_EXPERTS + e] + cap_s[kb], BF16_ROWS)
        return pltpu.make_async_copy(zeros, xe_hbm.at[pl.ds(row, zeros.shape[0]), :], sem.at[1])

    @pl.when(kb == 0)
    def _():
        cnt[0] = 0

    @pl.when(gfirst_s[kb] == 1)
    def _():
        carry[...] = jnp.zeros_like(carry)
        zeros[...] = jnp.zeros_like(zeros)
        for e in range(N_EXPERTS):
            pad_copy(e).start()
        for e in range(N_EXPERTS):
            pad_copy(e).wait()

    def one_pass(k, _):
        slot = cnt[0] % 2
        packed = _dot(_window_one_hot(c0_s, c1_s, rs_s, dest_ref, kb, k), h_ref[...])
        rows = []
        for e in range(N_EXPERTS):
            lo, base, row, nxt = _pass_window(c0_s, c1_s, rs_s, kb, e, k)
            head = packed[e * WIN:e * WIN + BF16_ROWS, :] + carry[e].astype(F32)
            stage[slot, e * WIN:e * WIN + BF16_ROWS, :] = head.astype(BF16)
            stage[slot, e * WIN + BF16_ROWS:(e + 1) * WIN, :] = packed[e * WIN + BF16_ROWS:(e + 1) * WIN, :].astype(BF16)
            off = pl.multiple_of(e * WIN + (nxt // BF16_ROWS) * BF16_ROWS - base, BF16_ROWS)
            carry[e] = stage[slot, pl.ds(off, BF16_ROWS), :]
            rows.append(row)

        @pl.when(cnt[0] > 0)
        def _():
            wait_all()

        for e in range(N_EXPERTS):
            window_copy(slot, e, rows[e]).start()
        cnt[0] = cnt[0] + 1
        return 0

    lax.fori_loop(0, _num_passes(c0_s, c1_s, kb), one_pass, 0)

    @pl.when((kb == pl.num_programs(0) - 1) & (cnt[0] > 0))
    def _():
        wait_all()


def _dispatch(c0, c1, rs, gfirst, cap, dest, h2a, st):
    tk = st.tk
    grid_spec = pltpu.PrefetchScalarGridSpec(
        num_scalar_prefetch=5,
        grid=(st.total // tk,),
        in_specs=[
            pl.BlockSpec((N_EXPERTS, tk), lambda i, *_: (0, i)),
            pl.BlockSpec((tk, XW), lambda i, *_: (i, 0)),
        ],
        out_specs=pl.BlockSpec(memory_space=pl.ANY),
        scratch_shapes=[
            pltpu.VMEM((2, N_EXPERTS * WIN, XW), BF16),
            pltpu.VMEM((N_EXPERTS, BF16_ROWS, XW), BF16),
            pltpu.VMEM((st.ts, XW), BF16),
            pltpu.SemaphoreType.DMA((2,)),
            pltpu.SMEM((1,), I32),
        ],
    )
    return pl.pallas_call(
        _dispatch_body,
        grid_spec=grid_spec,
        out_shape=jax.ShapeDtypeStruct((st.slot_rows, XW), BF16),
        compiler_params=_cparams(("arbitrary",)),
        name="moe_dispatch",
    )(c0, c1, rs, gfirst, cap, dest, h2a)


def _moe_ffn_body(pad_s, src_s, xe_ref, wg_ref, wu_ref, wd_ref, o_ref, *, tiles_per_expert):
    i = pl.program_id(0)

    @pl.when(pad_s[i] == 1)
    def _():
        o_ref[...] = jnp.zeros_like(o_ref)

    @pl.when(pad_s[i] == 0)
    def _():
        e = i // tiles_per_expert
        xe = xe_ref[:, :D_MODEL]
        terms = xe_ref[:, D_MODEL:].astype(F32)
        lane = lax.broadcasted_iota(I32, terms.shape, 1)
        mine = (lane < 3 * GATE_STRIDE) & (lane % GATE_STRIDE == e)
        gate = jnp.sum(jnp.where(mine, terms, 0.0), axis=1, keepdims=True)
        a = _dot(xe, wg_ref[...])
        u = _dot(xe, wu_ref[...])
        act = (a * _sigmoid(a) * u).astype(BF16)
        o_ref[...] = (_dot(act, wd_ref[...]) * gate).astype(BF16)


def _moe_ffn(xe, wg, wu, wd, layer, st):
    ts = st.ts
    tpe = st.tiles_per_expert
    pad, src = st.tile_tables()
    grid_spec = pltpu.PrefetchScalarGridSpec(
        num_scalar_prefetch=2,
        grid=(N_EXPERTS * tpe,),
        in_specs=[
            pl.BlockSpec((ts, XW), lambda i, p, s: (s[i], 0)),
            pl.BlockSpec((None, None, D_MODEL, EXPERT_FF), lambda i, p, s: (layer, i // tpe, 0, 0)),
            pl.BlockSpec((None, None, D_MODEL, EXPERT_FF), lambda i, p, s: (layer, i // tpe, 0, 0)),
            pl.BlockSpec((None, None, EXPERT_FF, D_MODEL), lambda i, p, s: (layer, i // tpe, 0, 0)),
        ],
        out_specs=pl.BlockSpec((ts, D_MODEL), lambda i, p, s: (i, 0)),
    )
    return pl.pallas_call(
        functools.partial(_moe_ffn_body, tiles_per_expert=tpe),
        grid_spec=grid_spec,
        out_shape=jax.ShapeDtypeStruct((st.slot_rows, D_MODEL), BF16),
        compiler_params=_cparams(("arbitrary",)),
        name="moe_ffn",
    )(pad, src, xe, wg, wu, wd)


OVERFLOW_SLOT = 2


def _combine_body(c0_s, c1_s, rs_s, dest_ref, x_ref, ye_hbm, o_ref, buf, sem):
    kb = pl.program_id(0)
    nkb = pl.num_programs(0)

    def window_copy(slot, e, row):
        return pltpu.make_async_copy(ye_hbm.at[pl.ds(row, WIN), :], buf.at[slot, pl.ds(e * WIN, WIN), :], sem.at[slot])

    def fetch(block, k, slot):
        for e in range(N_EXPERTS):
            window_copy(slot, e, _pass_window(c0_s, c1_s, rs_s, block, e, k)[2]).start()

    def wait_all(slot):
        for e in range(N_EXPERTS):
            window_copy(slot, e, 0).wait()

    @pl.when(kb == 0)
    def _():
        fetch(0, 0, 0)

    slot = kb % 2
    wait_all(slot)

    @pl.when(kb + 1 < nkb)
    def _():
        fetch(kb + 1, 0, 1 - slot)

    o_ref[...] = x_ref[...] + _dot_tn(_window_one_hot(c0_s, c1_s, rs_s, dest_ref, kb, 0), buf[slot])

    def extra_pass(k, _):
        fetch(kb, k, OVERFLOW_SLOT)
        wait_all(OVERFLOW_SLOT)
        o_ref[...] += _dot_tn(_window_one_hot(c0_s, c1_s, rs_s, dest_ref, kb, k), buf[OVERFLOW_SLOT])
        return 0

    lax.fori_loop(1, _num_passes(c0_s, c1_s, kb), extra_pass, 0)


def _combine(c0, c1, rs, dest, x1, ye, st):
    tk = st.tk
    grid_spec = pltpu.PrefetchScalarGridSpec(
        num_scalar_prefetch=3,
        grid=(st.total // tk,),
        in_specs=[
            pl.BlockSpec((N_EXPERTS, tk), lambda i, *_: (0, i)),
            pl.BlockSpec((tk, D_MODEL), lambda i, *_: (i, 0)),
            pl.BlockSpec(memory_space=pl.ANY),
        ],
        out_specs=pl.BlockSpec((tk, D_MODEL), lambda i, *_: (i, 0)),
        scratch_shapes=[
            pltpu.VMEM((3, N_EXPERTS * WIN, D_MODEL), BF16),
            pltpu.SemaphoreType.DMA((3,)),
        ],
    )
    return pl.pallas_call(
        _combine_body,
        grid_spec=grid_spec,
        out_shape=jax.ShapeDtypeStruct((st.total, D_MODEL), F32),
        compiler_params=_cparams(("arbitrary",)),
        name="moe_combine",
    )(c0, c1, rs, dest, x1, ye)


def _final_norm_body(x_ref, w_ref, o_ref):
    x = x_ref[...]
    o_ref[...] = x * lax.rsqrt(jnp.mean(x * x, axis=-1, keepdims=True) + EPS) * w_ref[...]


def _final_norm(x, w, st, group):
    t, tm = st.t_g[group], st.tm
    first = st.off_g[group] // tm
    return pl.pallas_call(
        _final_norm_body,
        grid=(t // tm,),
        in_specs=[pl.BlockSpec((tm, D_MODEL), lambda i: (first + i, 0)), pl.BlockSpec((1, D_MODEL), lambda i: (0, 0))],
        out_specs=pl.BlockSpec((tm, D_MODEL), lambda i: (i, 0)),
        out_shape=jax.ShapeDtypeStruct((t, D_MODEL), F32),
        compiler_params=_cparams(("arbitrary",)),
        name="final_norm",
    )(x, w)


def _pad_lanes(v, width=LANES):
    return jnp.pad(v, [(0, 0)] * (v.ndim - 1) + [(0, width - v.shape[-1])])


def _trunk_stream(xs, params, tm, tq, tk, ts, tc, cps):
    (norm1_w, w_in, ssd_conv_w, ssd_conv_b, ssd_dt_bias, ssd_a_log, ssd_d, ssd_norm_w, w_ssd_out, sc_conv_w,
     w_sc_out, gate_b, w_o, norm2_w, router_w, w_gate, w_up, w_down, final_norm_w) = params
    depth = w_in.shape[0]
    st = _Stream([(x.shape[0], x.shape[1]) for x in xs], tm, tq, tk, ts, tc, cps)
    x = jnp.concatenate([v.reshape(-1, D_MODEL) for v in xs], axis=0)

    w_main = jnp.concatenate([w_in[:, :, :_OFF_DT], w_in[:, :, _OFF_GATE:], w_in[:, :, _OFF_SCB:_OFF_GATE]],
                             axis=2).astype(BF16)
    w_dt = _pad_lanes(w_in[:, :, _OFF_DT:_OFF_SCB]).astype(BF16)
    dtb = _pad_lanes(ssd_dt_bias.reshape(depth, 1, 2 * HEADS))
    alog = _pad_lanes(ssd_a_log.reshape(depth, 1, 2 * HEADS))
    dskip = jnp.repeat(ssd_d, HEAD_DIM, axis=-1).reshape(depth, 1, INNER)
    ef, eb = _expansion_matrices()
    router_t = jnp.swapaxes(router_w, 1, 2)
    wg, wu, wd = w_gate.astype(BF16), w_up.astype(BF16), w_down.astype(BF16)
    row = lambda v: v.reshape(depth, 1, v.shape[-1])

    chunk_first, chunk_last = st.seq_flags(CHUNK)
    tile_flags = st.seq_flags(tq)
    conv_flags = st.seq_flags(tc)
    region_start, group_first, group_cap = st.block_tables()

    for l in range(depth):
        proj, dt_raw = _inproj(x, row(norm1_w), w_main, w_dt, l, st)
        xbc = _conv_xbc(proj, ssd_conv_w, row(ssd_conv_b), l, st, conv_flags)
        y1 = _ssd_fwd(xbc, dt_raw, dtb, alog, ef, l, st, chunk_first)
        o_ssd = _ssd_bwd(xbc, dt_raw, y1, proj, dtb, alog, eb, dskip, row(ssd_norm_w),
                         w_ssd_out.astype(BF16), l, st, chunk_last)
        x1, h2, probs_t = _merge(x, o_ssd, proj, sc_conv_w, w_sc_out.astype(BF16), row(gate_b), w_o.astype(BF16),
                                 row(norm2_w), router_t, l, st, tile_flags)
        dests, posxs = [], []
        for g in range(len(xs)):
            pg = probs_t[:, st.off_g[g]:st.off_g[g] + st.t_g[g]].reshape(N_EXPERTS, st.t_g[g] // LANES, LANES)
            dest, posx = _route(pg, st.cap_g[g])
            dests.append(dest.reshape(N_EXPERTS, st.t_g[g]))
            posxs.append(posx)
        dest = jnp.concatenate(dests, axis=1)
        c0, c1 = _block_counts(posxs, st)
        xe = _dispatch(c0, c1, region_start, group_first, group_cap, dest, h2, st)
        ye = _moe_ffn(xe, wg, wu, wd, l, st)
        x = _combine(c0, c1, region_start, dest, x1, ye, st)

    fw = final_norm_w.reshape(1, D_MODEL)
    return [_final_norm(x, fw, st, g).reshape(v.shape) for g, v in enumerate(xs)]


def kernel(x_prompt, x_sample, norm1_w, w_in, ssd_conv_w, ssd_conv_b, ssd_dt_bias, ssd_a_log, ssd_d, ssd_norm_w, w_ssd_out, sc_conv_w, w_sc_out, gate_b, w_o, norm2_w, router_w, w_gate, w_up, w_down, final_norm_w):
    params = (norm1_w, w_in, ssd_conv_w, ssd_conv_b, ssd_dt_bias, ssd_a_log, ssd_d, ssd_norm_w, w_ssd_out, sc_conv_w,
              w_sc_out, gate_b, w_o, norm2_w, router_w, w_gate, w_up, w_down, final_norm_w)
    y_prompt, y_sample = _trunk_stream([x_prompt, x_sample], params, tm=1024, tq=512, tk=256, ts=512, tc=1024, cps=4)
    return (y_prompt, y_sample)
```

```python
import functools

import numpy as np
import jax
import jax.numpy as jnp
from jax import lax
from jax.experimental import pallas as pl
from jax.experimental.pallas import tpu as pltpu

F32 = jnp.float32
BF16 = jnp.bfloat16
I32 = jnp.int32

D_MODEL = 1024
HEADS = 24
HEAD_DIM = 64
INNER = HEADS * HEAD_DIM
GROUPS = 4
GROUP_W = INNER // GROUPS
STATE = 128
CHUNK = 128
SSD_CONV = 5
SC_WIDTH = 512
SC_CONV = 3
N_EXPERTS = 16
EXPERT_FF = 2048
CAPACITY_FACTOR = 2
EPS = 1e-6
LANES = 128
SUBLANES = 8

_OFF_XBC = INNER
_OFF_DT = _OFF_XBC + INNER + 2 * GROUPS * STATE
_OFF_SCB = _OFF_DT + 2 * HEADS
_OFF_GATE = _OFF_SCB + 3 * SC_WIDTH
_N_IN = _OFF_GATE + 2 * D_MODEL
COL_X = INNER
COL_B = 2 * INNER
COL_C = COL_B + GROUPS * STATE
COL_GATE = COL_C + GROUPS * STATE
COL_SCB = COL_GATE + 2 * D_MODEL
N_MAIN = COL_SCB + 3 * SC_WIDTH
CB = 512

BF16_ROWS = 2 * SUBLANES
PASS = 48
WIN = PASS + BF16_ROWS
XW = D_MODEL + LANES
GATE_STRIDE = N_EXPERTS

VMEM_LIMIT = 50 * 1024 * 1024
FUSED_VMEM_LIMIT = 58 * 1024 * 1024


def _cparams(sem):
    return pltpu.CompilerParams(dimension_semantics=sem, vmem_limit_bytes=VMEM_LIMIT)


def _sigmoid(v):
    return 0.5 * jnp.tanh(0.5 * v) + 0.5


def _split3(v):
    hi = v.astype(BF16)
    r = v - hi.astype(F32)
    mid = r.astype(BF16)
    lo = (r - mid.astype(F32)).astype(BF16)
    return hi, mid, lo


def _dot(a, b):
    return jnp.dot(a, b, preferred_element_type=F32)


def _dot_nt(a, b):
    return lax.dot_general(a, b, (((1,), (1,)), ((), ())), preferred_element_type=F32)


def _dot_tn(a, b):
    return lax.dot_general(a, b, (((0,), (0,)), ((), ())), preferred_element_type=F32)


def _dot3_l(a_f32, b_exact):
    hi, mid, lo = _split3(a_f32)
    return _dot(hi, b_exact) + _dot(mid, b_exact) + _dot(lo, b_exact)


def _dot3_r(a_exact, b_f32):
    hi, mid, lo = _split3(b_f32)
    return _dot(a_exact, hi) + _dot(a_exact, mid) + _dot(a_exact, lo)


class _Stream:
    def __init__(self, groups, tm, tq, tk, ts, tc, cps):
        self.groups = tuple(groups)
        self.tm, self.tq, self.tk, self.ts, self.tc, self.cps = tm, tq, tk, ts, tc, cps
        self.t_g = [n * l for n, l in self.groups]
        self.off_g = [int(v) for v in np.cumsum([0] + self.t_g[:-1])]
        self.total = int(sum(self.t_g))
        self.cap_g = [CAPACITY_FACTOR * t // N_EXPERTS for t in self.t_g]
        self.nkb_g = [t // tk for t in self.t_g]
        for (n, l), t, c in zip(self.groups, self.t_g, self.cap_g):
            assert l % (CHUNK * cps) == 0 and l % tq == 0 and l % tc == 0 and t % tk == 0 and c % ts == 0
            assert t % tm == 0 and tk % LANES == 0 and tc % CONV_ROWS == 0
        assert ts >= WIN
        self.region_g = [c + ts for c in self.cap_g]
        self.rows_per_expert = int(sum(self.region_g))
        self.tiles_per_expert = self.rows_per_expert // ts
        self.slot_rows = N_EXPERTS * self.rows_per_expert
        self.region_off_g = [int(v) for v in np.cumsum([0] + self.region_g[:-1])]

    def block_tables(self):
        rs, gfirst, cap = [], [], []
        for g, nkb in enumerate(self.nkb_g):
            for kb in range(nkb):
                gfirst.append(1 if kb == 0 else 0)
                cap.append(self.cap_g[g])
                rs.extend(e * self.rows_per_expert + self.region_off_g[g] for e in range(N_EXPERTS))
        return tuple(jnp.asarray(np.asarray(v, np.int32)) for v in (rs, gfirst, cap))

    def tile_tables(self):
        pad, src = [], []
        for e in range(N_EXPERTS):
            for g, c in enumerate(self.cap_g):
                n = c // self.ts
                base = len(pad)
                pad.extend([0] * n + [1])
                src.extend(list(range(base, base + n)) + [base + n - 1])
        return jnp.asarray(np.asarray(pad, np.int32)), jnp.asarray(np.asarray(src, np.int32))

    def seq_flags(self, tile):
        n_tiles = self.total // tile
        first = np.zeros((n_tiles,), np.int32)
        last = np.zeros((n_tiles,), np.int32)
        for (n, l), off in zip(self.groups, self.off_g):
            for s in range(n):
                first[(off + s * l) // tile] = 1
                last[(off + (s + 1) * l) // tile - 1] = 1
        return jnp.asarray(first), jnp.asarray(last)


def _inproj_body(x_ref, nw_ref, w_ref, wdt_ref, proj_ref, dt_ref, h_scr):
    @pl.when(pl.program_id(1) == 0)
    def _():
        x = x_ref[...]
        ms = jnp.mean(x * x, axis=-1, keepdims=True)
        h = (x * lax.rsqrt(ms + EPS) * nw_ref[...]).astype(BF16)
        h_scr[...] = h
        dt_ref[...] = _dot(h, wdt_ref[...])

    proj_ref[...] = _dot(h_scr[...], w_ref[...]).astype(BF16)


def _inproj(x, nw, w_main, w_dt, layer, st):
    t, tm = st.total, st.tm
    tn = N_MAIN // 3
    return pl.pallas_call(
        _inproj_body,
        grid=(t // tm, N_MAIN // tn),
        in_specs=[
            pl.BlockSpec((tm, D_MODEL), lambda i, n: (i, 0)),
            pl.BlockSpec((None, 1, D_MODEL), lambda i, n: (layer, 0, 0)),
            pl.BlockSpec((None, D_MODEL, tn), lambda i, n: (layer, 0, n)),
            pl.BlockSpec((None, D_MODEL, LANES), lambda i, n: (layer, 0, 0)),
        ],
        out_specs=[
            pl.BlockSpec((tm, tn), lambda i, n: (i, n)),
            pl.BlockSpec((tm, LANES), lambda i, n: (i, 0)),
        ],
        out_shape=[jax.ShapeDtypeStruct((t, N_MAIN), BF16), jax.ShapeDtypeStruct((t, LANES), F32)],
        scratch_shapes=[pltpu.VMEM((tm, D_MODEL), BF16)],
        compiler_params=_cparams(("arbitrary", "arbitrary")),
        name="inproj",
    )(x, nw, w_main, w_dt)


CONV_ROWS = 128


def _conv_body(topz, botz, main_ref, prev_ref, next_ref, w_ref, b_ref, o_ref, ext):
    i = pl.program_id(0)
    tq = main_ref.shape[0]
    halo = BF16_ROWS
    pad = SSD_CONV // 2
    zero = jnp.zeros_like(prev_ref[...])
    ext[0:halo, :] = jnp.where(topz[i] == 1, zero, prev_ref[...])
    ext[halo:halo + tq, :] = main_ref[...]
    ext[halo + tq:2 * halo + tq, :] = jnp.where(botz[i] == 1, zero, next_ref[...])
    win = CONV_ROWS + 2 * halo
    taps = [k for k in range(SSD_CONV) if k != pad]
    r = lax.broadcasted_iota(I32, (CONV_ROWS, win), 0)
    c = lax.broadcasted_iota(I32, (CONV_ROWS, win), 1)
    shift = jnp.concatenate([jnp.where(c == r + halo + k - pad, 1.0, 0.0).astype(BF16) for k in taps], axis=0)
    for sb in range(tq // CONV_ROWS):
        r0 = sb * CONV_ROWS
        moved = _dot(shift, ext[r0:r0 + win, :])
        acc = b_ref[...] + ext[halo + r0:halo + r0 + CONV_ROWS, :].astype(F32) * w_ref[pad:pad + 1, :]
        for n, k in enumerate(taps):
            acc = acc + moved[n * CONV_ROWS:(n + 1) * CONV_ROWS, :] * w_ref[k:k + 1, :]
        o_ref[r0:r0 + CONV_ROWS, :] = (acc * _sigmoid(acc)).astype(BF16)


def _conv_xbc(proj, conv_w, conv_b, layer, st, flags):
    t, tq = st.total, st.tc
    ncb = (COL_GATE - COL_X) // CB
    cb0 = COL_X // CB
    rpt = tq // BF16_ROWS
    last_rb = t // BF16_ROWS - 1
    grid_spec = pltpu.PrefetchScalarGridSpec(
        num_scalar_prefetch=2,
        grid=(t // tq, ncb),
        in_specs=[
            pl.BlockSpec((tq, CB), lambda i, j, a, b: (i, cb0 + j)),
            pl.BlockSpec((BF16_ROWS, CB), lambda i, j, a, b: (jnp.maximum(i * rpt - 1, 0), cb0 + j)),
            pl.BlockSpec((BF16_ROWS, CB), lambda i, j, a, b: (jnp.minimum((i + 1) * rpt, last_rb), cb0 + j)),
            pl.BlockSpec((None, SSD_CONV, CB), lambda i, j, a, b: (layer, 0, j)),
            pl.BlockSpec((None, 1, CB), lambda i, j, a, b: (layer, 0, j)),
        ],
        out_specs=pl.BlockSpec((tq, CB), lambda i, j, a, b: (i, j)),
        scratch_shapes=[pltpu.VMEM((tq + 2 * BF16_ROWS, CB), BF16)],
    )
    return pl.pallas_call(
        _conv_body,
        grid_spec=grid_spec,
        out_shape=jax.ShapeDtypeStruct((t, COL_GATE - COL_X), BF16),
        compiler_params=_cparams(("arbitrary", "arbitrary")),
        name="conv_xbc",
    )(flags[0], flags[1], proj, proj, proj, conv_w, conv_b)


def _ssd_scalars(dt_ref, rows, dtb_ref, alog_ref):
    v = dt_ref[rows, :] + dtb_ref[...]
    u = jnp.exp(-jnp.abs(v))
    w = 1.0 + u
    dt = jnp.maximum(v, 0.0) + jnp.where(w == 1.0, u, jnp.log(w) * (u / (w - 1.0)))
    da = dt * (-jnp.exp(alog_ref[...]))
    r = lax.broadcasted_iota(I32, (CHUNK, CHUNK), 0)
    c = lax.broadcasted_iota(I32, (CHUNK, CHUNK), 1)
    tri = jnp.where(c <= r, 1.0, 0.0).astype(BF16)
    upper = jnp.where(r <= c, 1.0, 0.0).astype(BF16)
    cs = _dot3_r(tri, da)
    da_t = da.T
    dt_t = dt.T
    cs_t = _dot3_l(da_t, upper)
    return dt, cs, cs - da, dt_t, cs_t, cs_t - da_t, r, c


def _expand(v, e_ref):
    hi = v.astype(BF16)
    mid = (v - hi.astype(F32)).astype(BF16)
    e = e_ref[...]
    return _dot(hi, e) + _dot(mid, e)


def _expand_pair(a, b, e_ref):
    out = _expand(jnp.concatenate([a, b], axis=0), e_ref)
    return out[:CHUNK, :], out[CHUNK:, :]


def _state_step(s_scr, xs_f32, w_tok, dec_lanes, b_ref, rows):
    xw = (xs_f32 * w_tok).astype(BF16)
    for g in range(GROUPS):
        sl = slice(g * GROUP_W, (g + 1) * GROUP_W)
        sc = _dot_tn(b_ref[rows, g * STATE:(g + 1) * STATE], xw[:, sl])
        s_scr[:, sl] = s_scr[:, sl] * dec_lanes[:, sl] + sc


def _ssd_fwd_body(first, xs_ref, b_ref, c_ref, dt_ref, dtb_ref, alog_ref, ef_ref, y_ref, s_scr):
    cps = xs_ref.shape[0] // CHUNK

    @pl.when(pl.program_id(0) == 0)
    def _():
        s_scr[...] = jnp.zeros_like(s_scr)

    for sub in range(cps):
        rows = slice(sub * CHUNK, (sub + 1) * CHUNK)
        keep = jnp.where(first[pl.program_id(0) * cps + sub] == 1, 0.0, 1.0)
        dt, cs, ex, dt_t, cs_t, ex_t, r, c = _ssd_scalars(dt_ref, rows, dtb_ref, alog_ref)
        w_out, w_st = _expand_pair(jnp.exp(cs) * keep, jnp.exp(cs[CHUNK - 1:CHUNK, :] - cs) * dt, ef_ref)
        for g in range(GROUPS):
            sl = slice(g * GROUP_W, (g + 1) * GROUP_W)
            yo = _dot(c_ref[rows, g * STATE:(g + 1) * STATE], s_scr[:, sl].astype(BF16))
            y_ref[rows, sl] = yo * w_out[:, sl]
        lane = lax.broadcasted_iota(I32, (CHUNK, 2 * HEAD_DIM), 1)
        lower = r >= c
        pairs_per_group = HEADS // GROUPS // 2
        for pi in range(HEADS // 2):
            g = pi // pairs_per_group
            if pi % pairs_per_group == 0:
                cb = _dot_nt(c_ref[rows, g * STATE:(g + 1) * STATE], b_ref[rows, g * STATE:(g + 1) * STATE])
            ms = []
            for h in (2 * pi, 2 * pi + 1):
                hb = HEADS + h
                arg = jnp.where(lower, cs[:, h:h + 1] - cs_t[h:h + 1, :], ex_t[hb:hb + 1, :] - ex[:, hb:hb + 1])
                d0 = dt_t[h:h + 1, :]
                d1 = dt_t[hb:hb + 1, :]
                w = jnp.where(r > c, d0, jnp.where(r < c, d1, d0 + d1))
                ms.append((jnp.exp(arg) * w * cb).astype(BF16))
            psl = slice(pi * 2 * HEAD_DIM, (pi + 1) * 2 * HEAD_DIM)
            xp = xs_ref[rows, psl]
            zero = jnp.zeros_like(xp)
            rhs = jnp.concatenate([jnp.where(lane < HEAD_DIM, xp, zero), jnp.where(lane >= HEAD_DIM, xp, zero)],
                                  axis=0)
            y_ref[rows, psl] = y_ref[rows, psl] + _dot(jnp.concatenate(ms, axis=1), rhs)
        _state_step(s_scr, xs_ref[rows, :].astype(F32), w_st, w_out[CHUNK - 1:CHUNK, :], b_ref, rows)


def _ssd_bwd_rows(last, step, xs_ref, b_ref, c_ref, dt_ref, y1_ref, z_ref, dtb_ref, alog_ref, eb_ref,
                  dskip_ref, nw_ref, s_scr, yn_scr, between=()):
    cps = xs_ref.shape[0] // CHUNK
    between = list(between)
    for sub in reversed(range(cps)):
        if between:
            between.pop(0)()
        rows = slice(sub * CHUNK, (sub + 1) * CHUNK)
        keep = jnp.where(last[step * cps + sub] == 1, 0.0, 1.0)
        dt, cs, ex, dt_t, cs_t, ex_t, r, c = _ssd_scalars(dt_ref, rows, dtb_ref, alog_ref)
        tot = cs[CHUNK - 1:CHUNK, :]
        w_out, w_st = _expand_pair(jnp.exp(tot - ex) * keep, jnp.exp(ex) * dt, eb_ref)
        xs = xs_ref[rows, :].astype(F32)
        z = z_ref[rows, :].astype(F32)
        zs = z * _sigmoid(z)
        for g in range(GROUPS):
            sl = slice(g * GROUP_W, (g + 1) * GROUP_W)
            yo = _dot(c_ref[rows, g * STATE:(g + 1) * STATE], s_scr[:, sl].astype(BF16))
            y = (y1_ref[rows, sl] + yo * w_out[:, sl] + xs[:, sl] * dskip_ref[:, sl]) * zs[:, sl]
            y = y * lax.rsqrt(jnp.mean(y * y, axis=-1, keepdims=True) + EPS)
            yn_scr[rows, sl] = (y * nw_ref[:, sl]).astype(BF16)
        _state_step(s_scr, xs, w_st, w_out[0:1, :], b_ref, rows)
    for rest in between:
        rest()


def _expansion_matrices():
    rows = np.arange(LANES)[:, None]
    cols = np.arange(INNER)[None, :] // HEAD_DIM
    ef = (rows == cols).astype(np.float32)
    eb = (rows == cols + HEADS).astype(np.float32)
    return jnp.asarray(ef, BF16), jnp.asarray(eb, BF16)


def _ssd_fwd(xbc, dt_raw, dtb, alog, ef, layer, st, first):
    t = st.total
    nb = GROUPS * STATE // CB
    rows = st.cps * CHUNK
    grid_spec = pltpu.PrefetchScalarGridSpec(
        num_scalar_prefetch=1,
        grid=(t // rows,),
        in_specs=[
            pl.BlockSpec((rows, INNER), lambda i, f: (i, 0)),
            pl.BlockSpec((rows, CB), lambda i, f: (i, INNER // CB)),
            pl.BlockSpec((rows, CB), lambda i, f: (i, INNER // CB + nb)),
            pl.BlockSpec((rows, LANES), lambda i, f: (i, 0)),
            pl.BlockSpec((None, 1, LANES), lambda i, f: (layer, 0, 0)),
            pl.BlockSpec((None, 1, LANES), lambda i, f: (layer, 0, 0)),
            pl.BlockSpec((LANES, INNER), lambda i, f: (0, 0)),
        ],
        out_specs=pl.BlockSpec((rows, INNER), lambda i, f: (i, 0)),
        scratch_shapes=[pltpu.VMEM((STATE, INNER), F32)],
    )
    return pl.pallas_call(
        _ssd_fwd_body,
        grid_spec=grid_spec,
        out_shape=jax.ShapeDtypeStruct((t, INNER), F32),
        compiler_params=_cparams(("arbitrary",)),
        name="ssd_fwd",
    )(first, xbc, xbc, xbc, dt_raw, dtb, alog, ef)


def _merge_stages(i, topz, botz, x_ref, o_ssd, gate_ref, scb_ref, scc_ref, scx_ref, ccp_ref, cxp_ref, ccn_ref, cxn_ref,
                  cw_ref, wsc_ref, gb_ref, wo_ref, nw_ref, rw_ref, x1_ref, h2_ref, pt_ref, ext):
    tm = x_ref.shape[0]
    halo = SUBLANES
    pad = SC_CONV // 2
    live = {}

    def short_conv():
        ext[0:halo, :] = jnp.where(topz[i] == 1, 0.0, ccp_ref[...].astype(F32) * cxp_ref[...].astype(F32))
        ext[halo:halo + tm, :] = scc_ref[...].astype(F32) * scx_ref[...].astype(F32)
        ext[halo + tm:2 * halo + tm, :] = jnp.where(botz[i] == 1, 0.0,
                                                    ccn_ref[...].astype(F32) * cxn_ref[...].astype(F32))
        v = ext[halo - pad:halo - pad + tm, :] * cw_ref[0:1, :]
        for k in range(1, SC_CONV):
            v = v + ext[halo - pad + k:halo - pad + k + tm, :] * cw_ref[k:k + 1, :]
        live["o_sc"] = _dot((scb_ref[...].astype(F32) * v).astype(BF16), wsc_ref[...])

    def gated_mix():
        g = _sigmoid(gate_ref[...].astype(F32) + gb_ref[...])
        mix = g[:, :D_MODEL] * o_ssd.astype(F32) + g[:, D_MODEL:] * live["o_sc"]
        x1 = x_ref[...] + _dot(mix.astype(BF16), wo_ref[...])
        x1_ref[...] = x1
        live["x1"] = x1

    def norm():
        x1 = live["x1"]
        h2 = x1 * lax.rsqrt(jnp.mean(x1 * x1, axis=-1, keepdims=True) + EPS) * nw_ref[...]
        h2_ref[:, :D_MODEL] = h2.astype(BF16)
        live["h2"] = h2

    def router():
        hh, hm, _ = _split3(live["h2"])
        rh, rm, _ = _split3(rw_ref[...])
        logits = _dot_nt(rh, hh) + _dot_nt(rh, hm) + _dot_nt(rm, hh)
        mx = jnp.max(logits, axis=0, keepdims=True)
        ex = jnp.exp(logits - mx)
        probs = ex / jnp.sum(ex, axis=0, keepdims=True)
        pt_ref[...] = probs
        pk = jnp.concatenate([probs, jnp.zeros((LANES - N_EXPERTS, tm), F32)], axis=0).T
        hi = pk.astype(BF16).astype(F32)
        mid = (pk - hi).astype(BF16).astype(F32)
        lo = pk - hi - mid
        lane = lax.broadcasted_iota(I32, (tm, LANES), 1)
        terms = jnp.where(lane < GATE_STRIDE, hi,
                          jnp.where(lane < 2 * GATE_STRIDE, pltpu.roll(mid, GATE_STRIDE, 1),
                                    jnp.where(lane < 3 * GATE_STRIDE, pltpu.roll(lo, 2 * GATE_STRIDE, 1), 0.0)))
        h2_ref[:, D_MODEL:] = terms.astype(BF16)

    return [short_conv, gated_mix, norm, router]


def _bwd_merge_body(last, topz, botz,
                    xs_ref, b_ref, c_ref, dt_ref, y1_ref, z_ref, dtb_ref, alog_ref, eb_ref, dskip_ref, snw_ref, wout_ref,
                    x_ref, gate_ref, scb_ref, scc_ref, scx_ref, ccp_ref, cxp_ref, ccn_ref, cxn_ref,
                    cw_ref, wsc_ref, gb_ref, wo_ref, nw_ref, rw_ref,
                    x1_ref, h2_ref, pt_ref, s_scr, yn_scr, ossd_scr, ext):
    i = pl.program_id(0)
    n_blocks = pl.num_programs(0) - 1

    @pl.when(i == 0)
    def _():
        s_scr[...] = jnp.zeros_like(s_scr)
        ossd_scr[...] = jnp.zeros_like(ossd_scr)

    slot = i % 2
    merge_tile = n_blocks - 1 - jnp.maximum(i - 1, 0)
    stages = _merge_stages(merge_tile, topz, botz, x_ref, ossd_scr[1 - slot], gate_ref, scb_ref, scc_ref, scx_ref,
                           ccp_ref, cxp_ref, ccn_ref, cxn_ref, cw_ref, wsc_ref, gb_ref, wo_ref, nw_ref, rw_ref,
                           x1_ref, h2_ref, pt_ref, ext)
    ssd_tile = n_blocks - 1 - jnp.minimum(i, n_blocks - 1)
    _ssd_bwd_rows(last, ssd_tile, xs_ref, b_ref, c_ref, dt_ref, y1_ref, z_ref, dtb_ref, alog_ref, eb_ref,
                  dskip_ref, snw_ref, s_scr, yn_scr, between=stages)
    ossd_scr[slot] = _dot(yn_scr[...], wout_ref[...]).astype(BF16)


def _ssd_bwd_merge(x, xbc, dt_raw, y1, proj, dtb, alog, eb, dskip, ssd_norm, w_ssd_out, sc_w, w_sc, gate_b, w_o, norm2,
                   router_t, layer, st, last, flags):
    t, tm = st.total, st.tq
    assert tm == st.cps * CHUNK
    nc = t // tm
    nb = GROUPS * STATE // CB
    rpt = tm // SUBLANES
    last_rb = t // SUBLANES - 1
    cscb, cscc, cscx = COL_SCB // CB, COL_SCB // CB + 1, COL_SCB // CB + 2
    sb = lambda i: nc - 1 - jnp.minimum(i, nc - 1)
    mb = lambda i: nc - 1 - jnp.maximum(i - 1, 0)
    prev = lambda i: jnp.maximum(mb(i) * rpt - 1, 0)
    nxt = lambda i: jnp.minimum((mb(i) + 1) * rpt, last_rb)
    const = lambda *idx: (lambda i, *_: idx)
    grid_spec = pltpu.PrefetchScalarGridSpec(
        num_scalar_prefetch=3,
        grid=(nc + 1,),
        in_specs=[
            pl.BlockSpec((tm, INNER), lambda i, *_: (sb(i), 0)),
            pl.BlockSpec((tm, CB), lambda i, *_: (sb(i), INNER // CB)),
            pl.BlockSpec((tm, CB), lambda i, *_: (sb(i), INNER // CB + nb)),
            pl.BlockSpec((tm, LANES), lambda i, *_: (sb(i), 0)),
            pl.BlockSpec((tm, INNER), lambda i, *_: (sb(i), 0)),
            pl.BlockSpec((tm, INNER), lambda i, *_: (sb(i), 0)),
            pl.BlockSpec((None, 1, LANES), const(layer, 0, 0)),
            pl.BlockSpec((None, 1, LANES), const(layer, 0, 0)),
            pl.BlockSpec((LANES, INNER), const(0, 0)),
            pl.BlockSpec((None, 1, INNER), const(layer, 0, 0)),
            pl.BlockSpec((None, 1, INNER), const(layer, 0, 0)),
            pl.BlockSpec((None, INNER, D_MODEL), const(layer, 0, 0)),
            pl.BlockSpec((tm, D_MODEL), lambda i, *_: (mb(i), 0)),
            pl.BlockSpec((tm, 2 * D_MODEL), lambda i, *_: (mb(i), COL_GATE // (2 * D_MODEL))),
            pl.BlockSpec((tm, CB), lambda i, *_: (mb(i), cscb)),
            pl.BlockSpec((tm, CB), lambda i, *_: (mb(i), cscc)),
            pl.BlockSpec((tm, CB), lambda i, *_: (mb(i), cscx)),
            pl.BlockSpec((SUBLANES, CB), lambda i, *_: (prev(i), cscc)),
            pl.BlockSpec((SUBLANES, CB), lambda i, *_: (prev(i), cscx)),
            pl.BlockSpec((SUBLANES, CB), lambda i, *_: (nxt(i), cscc)),
            pl.BlockSpec((SUBLANES, CB), lambda i, *_: (nxt(i), cscx)),
            pl.BlockSpec((None, SC_CONV, SC_WIDTH), const(layer, 0, 0)),
            pl.BlockSpec((None, SC_WIDTH, D_MODEL), const(layer, 0, 0)),
            pl.BlockSpec((None, 1, 2 * D_MODEL), const(layer, 0, 0)),
            pl.BlockSpec((None, D_MODEL, D_MODEL), const(layer, 0, 0)),
            pl.BlockSpec((None, 1, D_MODEL), const(layer, 0, 0)),
            pl.BlockSpec((None, N_EXPERTS, D_MODEL), const(layer, 0, 0)),
        ],
        out_specs=[
            pl.BlockSpec((tm, D_MODEL), lambda i, *_: (mb(i), 0)),
            pl.BlockSpec((tm, XW), lambda i, *_: (mb(i), 0)),
            pl.BlockSpec((N_EXPERTS, tm), lambda i, *_: (0, mb(i))),
        ],
        scratch_shapes=[
            pltpu.VMEM((STATE, INNER), F32),
            pltpu.VMEM((tm, INNER), BF16),
            pltpu.VMEM((2, tm, D_MODEL), BF16),
            pltpu.VMEM((tm + 2 * SUBLANES, SC_WIDTH), F32),
        ],
    )
    return pl.pallas_call(
        _bwd_merge_body,
        grid_spec=grid_spec,
        out_shape=[jax.ShapeDtypeStruct((t, D_MODEL), F32), jax.ShapeDtypeStruct((t, XW), BF16),
                   jax.ShapeDtypeStruct((N_EXPERTS, t), F32)],
        compiler_params=pltpu.CompilerParams(dimension_semantics=("arbitrary",), vmem_limit_bytes=FUSED_VMEM_LIMIT),
        name="ssd_bwd_merge",
    )(last, flags[0], flags[1], xbc, xbc, xbc, dt_raw, y1, proj, dtb, alog, eb, dskip, ssd_norm, w_ssd_out,
      x, proj, proj, proj, proj, proj, proj, proj, proj, sc_w, w_sc, gate_b, w_o, norm2, router_t)


def _route_body(p_ref, dest_ref, posx_ref, *, cap):
    rows = p_ref.shape[1]
    r = lax.broadcasted_iota(I32, (LANES, LANES), 0)
    c = lax.broadcasted_iota(I32, (LANES, LANES), 1)
    upper = jnp.where(r <= c, 1.0, 0.0).astype(BF16)
    last_col = jnp.where(r == LANES - 1, 1.0, 0.0).astype(BF16)
    rr = lax.broadcasted_iota(I32, (rows, rows), 0)
    rc = lax.broadcasted_iota(I32, (rows, rows), 1)
    below = jnp.where(rc < rr, 1.0, 0.0).astype(BF16)

    def count(mask):
        s = jnp.sum(jnp.where(mask, 1.0, 0.0), axis=0, keepdims=True)
        return jnp.sum(s, axis=1, keepdims=True)

    def excl_prefix(mask):
        x = jnp.where(mask, 1.0, 0.0)
        incl = _dot(x.astype(BF16), upper)
        row_tot = _dot(incl.astype(BF16), last_col)
        row_off = _dot(below, row_tot.astype(BF16))
        return incl + row_off - x

    def search(i, prefix):
        bit = jnp.left_shift(jnp.int32(1), 30 - i)
        out = []
        for e in range(N_EXPERTS):
            keys = pltpu.bitcast(p_ref[e], I32)
            cand = prefix[e] | bit
            out.append(jnp.where(count(keys >= cand) >= cap, cand, prefix[e]))
        return tuple(out)

    thr = lax.fori_loop(0, 31, search, tuple(jnp.zeros((1, 1), I32) for _ in range(N_EXPERTS)))
    for e in range(N_EXPERTS):
        keys = pltpu.bitcast(p_ref[e], I32)
        gt = keys > thr[e]
        eq = keys == thr[e]
        need = cap - count(gt)
        sel = gt | (eq & (excl_prefix(eq) < need))
        pos = excl_prefix(sel).astype(I32)
        posx_ref[e] = pos
        dest_ref[e] = jnp.where(sel, pos, -1)


def _route(probs3, cap):
    e, rows, _ = probs3.shape
    return pl.pallas_call(
        functools.partial(_route_body, cap=cap),
        out_shape=[jax.ShapeDtypeStruct((e, rows, LANES), I32), jax.ShapeDtypeStruct((e, rows, LANES), I32)],
        compiler_params=pltpu.CompilerParams(vmem_limit_bytes=VMEM_LIMIT),
        name="route",
    )(probs3)


def _block_counts(posx_list, st):
    rb = st.tk // LANES
    c0s, c1s = [], []
    for g, posx in enumerate(posx_list):
        c0 = posx[:, ::rb, 0]
        c1 = jnp.concatenate([c0[:, 1:], jnp.full((N_EXPERTS, 1), st.cap_g[g], I32)], axis=1)
        c0s.append(c0.T.reshape(-1))
        c1s.append(c1.T.reshape(-1))
    return jnp.concatenate(c0s), jnp.concatenate(c1s)


def _pass_window(c0_s, c1_s, rs_s, kb, e, k):
    c0 = c0_s[kb * N_EXPERTS + e]
    c1 = c1_s[kb * N_EXPERTS + e]
    lo = jnp.minimum(c0 + k * PASS, c1)
    base = (lo // BF16_ROWS) * BF16_ROWS
    row = pl.multiple_of(rs_s[kb * N_EXPERTS + e] + base, BF16_ROWS)
    return lo, base, row, jnp.minimum(lo + PASS, c1)


def _num_passes(c0_s, c1_s, kb):
    most = c1_s[kb * N_EXPERTS] - c0_s[kb * N_EXPERTS]
    for e in range(1, N_EXPERTS):
        most = jnp.maximum(most, c1_s[kb * N_EXPERTS + e] - c0_s[kb * N_EXPERTS + e])
    return (most + PASS - 1) // PASS


def _window_one_hot(c0_s, c1_s, rs_s, dest_ref, kb, k):
    tk = dest_ref.shape[1]
    slot = lax.broadcasted_iota(I32, (WIN, tk), 0)
    parts = []
    for e in range(N_EXPERTS):
        lo, base, _, _ = _pass_window(c0_s, c1_s, rs_s, kb, e, k)
        d = dest_ref[e:e + 1, :]
        rel = jnp.where(d >= lo, jnp.where(d < lo + PASS, d - base, -1), -1)
        parts.append(jnp.where(rel == slot, 1.0, 0.0).astype(BF16))
    return jnp.concatenate(parts, axis=0)


def _dispatch_body(c0_s, c1_s, rs_s, gfirst_s, cap_s, dest_ref, h_ref, xe_hbm, stage, carry, zeros, sem, cnt):
    kb = pl.program_id(0)

    def window_copy(slot, e, row):
        return pltpu.make_async_copy(stage.at[slot, pl.ds(e * WIN, WIN), :], xe_hbm.at[pl.ds(row, WIN), :], sem.at[0])

    def wait_all():
        for e in range(N_EXPERTS):
            window_copy(0, e, 0).wait()

    def pad_copy(e):
        row = pl.multiple_of(rs_s[kb * N_EXPERTS + e] + cap_s[kb], BF16_ROWS)
        return pltpu.make_async_copy(zeros, xe_hbm.at[pl.ds(row, zeros.shape[0]), :], sem.at[1])

    @pl.when(kb == 0)
    def _():
        cnt[0] = 0

    @pl.when(gfirst_s[kb] == 1)
    def _():
        carry[...] = jnp.zeros_like(carry)
        zeros[...] = jnp.zeros_like(zeros)
        for e in range(N_EXPERTS):
            pad_copy(e).start()
        for e in range(N_EXPERTS):
            pad_copy(e).wait()

    def one_pass(k, _):
        slot = cnt[0] % 2
        packed = _dot(_window_one_hot(c0_s, c1_s, rs_s, dest_ref, kb, k), h_ref[...])
        rows = []
        for e in range(N_EXPERTS):
            lo, base, row, nxt = _pass_window(c0_s, c1_s, rs_s, kb, e, k)
            head = packed[e * WIN:e * WIN + BF16_ROWS, :] + carry[e].astype(F32)
            stage[slot, e * WIN:e * WIN + BF16_ROWS, :] = head.astype(BF16)
            stage[slot, e * WIN + BF16_ROWS:(e + 1) * WIN, :] = packed[e * WIN + BF16_ROWS:(e + 1) * WIN, :].astype(BF16)
            off = pl.multiple_of(e * WIN + (nxt // BF16_ROWS) * BF16_ROWS - base, BF16_ROWS)
            carry[e] = stage[slot, pl.ds(off, BF16_ROWS), :]
            rows.append(row)

        @pl.when(cnt[0] > 0)
        def _():
            wait_all()

        for e in range(N_EXPERTS):
            window_copy(slot, e, rows[e]).start()
        cnt[0] = cnt[0] + 1
        return 0

    lax.fori_loop(0, _num_passes(c0_s, c1_s, kb), one_pass, 0)

    @pl.when((kb == pl.num_programs(0) - 1) & (cnt[0] > 0))
    def _():
        wait_all()


def _dispatch(c0, c1, rs, gfirst, cap, dest, h2a, st):
    tk = st.tk
    grid_spec = pltpu.PrefetchScalarGridSpec(
        num_scalar_prefetch=5,
        grid=(st.total // tk,),
        in_specs=[
            pl.BlockSpec((N_EXPERTS, tk), lambda i, *_: (0, i)),
            pl.BlockSpec((tk, XW), lambda i, *_: (i, 0)),
        ],
        out_specs=pl.BlockSpec(memory_space=pl.ANY),
        scratch_shapes=[
            pltpu.VMEM((2, N_EXPERTS * WIN, XW), BF16),
            pltpu.VMEM((N_EXPERTS, BF16_ROWS, XW), BF16),
            pltpu.VMEM((st.ts, XW), BF16),
            pltpu.SemaphoreType.DMA((2,)),
            pltpu.SMEM((1,), I32),
        ],
    )
    return pl.pallas_call(
        _dispatch_body,
        grid_spec=grid_spec,
        out_shape=jax.ShapeDtypeStruct((st.slot_rows, XW), BF16),
        compiler_params=_cparams(("arbitrary",)),
        name="moe_dispatch",
    )(c0, c1, rs, gfirst, cap, dest, h2a)


def _moe_ffn_body(pad_s, src_s, xe_ref, wg_ref, wu_ref, wd_ref, o_ref, *, tiles_per_expert):
    i = pl.program_id(0)

    @pl.when(pad_s[i] == 1)
    def _():
        o_ref[...] = jnp.zeros_like(o_ref)

    @pl.when(pad_s[i] == 0)
    def _():
        e = i // tiles_per_expert
        xe = xe_ref[:, :D_MODEL]
        terms = xe_ref[:, D_MODEL:].astype(F32)
        lane = lax.broadcasted_iota(I32, terms.shape, 1)
        mine = (lane < 3 * GATE_STRIDE) & (lane % GATE_STRIDE == e)
        gate = jnp.sum(jnp.where(mine, terms, 0.0), axis=1, keepdims=True)
        a = _dot(xe, wg_ref[...])
        u = _dot(xe, wu_ref[...])
        act = (a * _sigmoid(a) * u).astype(BF16)
        o_ref[...] = (_dot(act, wd_ref[...]) * gate).astype(BF16)


def _moe_ffn(xe, wg, wu, wd, layer, st):
    ts = st.ts
    tpe = st.tiles_per_expert
    pad, src = st.tile_tables()
    grid_spec = pltpu.PrefetchScalarGridSpec(
        num_scalar_prefetch=2,
        grid=(N_EXPERTS * tpe,),
        in_specs=[
            pl.BlockSpec((ts, XW), lambda i, p, s: (s[i], 0)),
            pl.BlockSpec((None, None, D_MODEL, EXPERT_FF), lambda i, p, s: (layer, i // tpe, 0, 0)),
            pl.BlockSpec((None, None, D_MODEL, EXPERT_FF), lambda i, p, s: (layer, i // tpe, 0, 0)),
            pl.BlockSpec((None, None, EXPERT_FF, D_MODEL), lambda i, p, s: (layer, i // tpe, 0, 0)),
        ],
        out_specs=pl.BlockSpec((ts, D_MODEL), lambda i, p, s: (i, 0)),
    )
    return pl.pallas_call(
        functools.partial(_moe_ffn_body, tiles_per_expert=tpe),
        grid_spec=grid_spec,
        out_shape=jax.ShapeDtypeStruct((st.slot_rows, D_MODEL), BF16),
        compiler_params=_cparams(("arbitrary",)),
        name="moe_ffn",
    )(pad, src, xe, wg, wu, wd)


OVERFLOW_SLOT = 2


def _combine_body(c0_s, c1_s, rs_s, dest_ref, x_ref, ye_hbm, o_ref, buf, sem):
    kb = pl.program_id(0)
    nkb = pl.num_programs(0)

    def window_copy(slot, e, row):
        return pltpu.make_async_copy(ye_hbm.at[pl.ds(row, WIN), :], buf.at[slot, pl.ds(e * WIN, WIN), :], sem.at[slot])

    def fetch(block, k, slot):
        for e in range(N_EXPERTS):
            window_copy(slot, e, _pass_window(c0_s, c1_s, rs_s, block, e, k)[2]).start()

    def wait_all(slot):
        for e in range(N_EXPERTS):
            window_copy(slot, e, 0).wait()

    @pl.when(kb == 0)
    def _():
        fetch(0, 0, 0)

    slot = kb % 2
    wait_all(slot)

    @pl.when(kb + 1 < nkb)
    def _():
        fetch(kb + 1, 0, 1 - slot)

    o_ref[...] = x_ref[...] + _dot_tn(_window_one_hot(c0_s, c1_s, rs_s, dest_ref, kb, 0), buf[slot])

    def extra_pass(k, _):
        fetch(kb, k, OVERFLOW_SLOT)
        wait_all(OVERFLOW_SLOT)
        o_ref[...] += _dot_tn(_window_one_hot(c0_s, c1_s, rs_s, dest_ref, kb, k), buf[OVERFLOW_SLOT])
        return 0

    lax.fori_loop(1, _num_passes(c0_s, c1_s, kb), extra_pass, 0)


def _combine(c0, c1, rs, dest, x1, ye, st):
    tk = st.tk
    grid_spec = pltpu.PrefetchScalarGridSpec(
        num_scalar_prefetch=3,
        grid=(st.total // tk,),
        in_specs=[
            pl.BlockSpec((N_EXPERTS, tk), lambda i, *_: (0, i)),
            pl.BlockSpec((tk, D_MODEL), lambda i, *_: (i, 0)),
            pl.BlockSpec(memory_space=pl.ANY),
        ],
        out_specs=pl.BlockSpec((tk, D_MODEL), lambda i, *_: (i, 0)),
        scratch_shapes=[
            pltpu.VMEM((3, N_EXPERTS * WIN, D_MODEL), BF16),
            pltpu.SemaphoreType.DMA((3,)),
        ],
    )
    return pl.pallas_call(
        _combine_body,
        grid_spec=grid_spec,
        out_shape=jax.ShapeDtypeStruct((st.total, D_MODEL), F32),
        compiler_params=_cparams(("arbitrary",)),
        name="moe_combine",
    )(c0, c1, rs, dest, x1, ye)


def _final_norm_body(x_ref, w_ref, o_ref):
    x = x_ref[...]
    o_ref[...] = x * lax.rsqrt(jnp.mean(x * x, axis=-1, keepdims=True) + EPS) * w_ref[...]


def _final_norm(x, w, st, group):
    t, tm = st.t_g[group], st.tm
    first = st.off_g[group] // tm
    return pl.pallas_call(
        _final_norm_body,
        grid=(t // tm,),
        in_specs=[pl.BlockSpec((tm, D_MODEL), lambda i: (first + i, 0)), pl.BlockSpec((1, D_MODEL), lambda i: (0, 0))],
        out_specs=pl.BlockSpec((tm, D_MODEL), lambda i: (i, 0)),
        out_shape=jax.ShapeDtypeStruct((t, D_MODEL), F32),
        compiler_params=_cparams(("arbitrary",)),
        name="final_norm",
    )(x, w)


def _pad_lanes(v, width=LANES):
    return jnp.pad(v, [(0, 0)] * (v.ndim - 1) + [(0, width - v.shape[-1])])


def _trunk_stream(xs, params, tm, tq, tk, ts, tc, cps):
    (norm1_w, w_in, ssd_conv_w, ssd_conv_b, ssd_dt_bias, ssd_a_log, ssd_d, ssd_norm_w, w_ssd_out, sc_conv_w,
     w_sc_out, gate_b, w_o, norm2_w, router_w, w_gate, w_up, w_down, final_norm_w) = params
    depth = w_in.shape[0]
    st = _Stream([(x.shape[0], x.shape[1]) for x in xs], tm, tq, tk, ts, tc, cps)
    x = jnp.concatenate([v.reshape(-1, D_MODEL) for v in xs], axis=0)

    w_main = jnp.concatenate([w_in[:, :, :_OFF_DT], w_in[:, :, _OFF_GATE:], w_in[:, :, _OFF_SCB:_OFF_GATE]],
                             axis=2).astype(BF16)
    w_dt = _pad_lanes(w_in[:, :, _OFF_DT:_OFF_SCB]).astype(BF16)
    dtb = _pad_lanes(ssd_dt_bias.reshape(depth, 1, 2 * HEADS))
    alog = _pad_lanes(ssd_a_log.reshape(depth, 1, 2 * HEADS))
    dskip = jnp.repeat(ssd_d, HEAD_DIM, axis=-1).reshape(depth, 1, INNER)
    ef, eb = _expansion_matrices()
    router_t = jnp.swapaxes(router_w, 1, 2)
    wg, wu, wd = w_gate.astype(BF16), w_up.astype(BF16), w_down.astype(BF16)
    row = lambda v: v.reshape(depth, 1, v.shape[-1])

    chunk_first, chunk_last = st.seq_flags(CHUNK)
    tile_flags = st.seq_flags(tq)
    conv_flags = st.seq_flags(tc)
    region_start, group_first, group_cap = st.block_tables()

    for l in range(depth):
        proj, dt_raw = _inproj(x, row(norm1_w), w_main, w_dt, l, st)
        xbc = _conv_xbc(proj, ssd_conv_w, row(ssd_conv_b), l, st, conv_flags)
        y1 = _ssd_fwd(xbc, dt_raw, dtb, alog, ef, l, st, chunk_first)
        x1, h2, probs_t = _ssd_bwd_merge(x, xbc, dt_raw, y1, proj, dtb, alog, eb, dskip, row(ssd_norm_w),
                                         w_ssd_out.astype(BF16), sc_conv_w, w_sc_out.astype(BF16), row(gate_b),
                                         w_o.astype(BF16), row(norm2_w), router_t, l, st, chunk_last, tile_flags)
        dests, posxs = [], []
        for g in range(len(xs)):
            pg = probs_t[:, st.off_g[g]:st.off_g[g] + st.t_g[g]].reshape(N_EXPERTS, st.t_g[g] // LANES, LANES)
            dest, posx = _route(pg, st.cap_g[g])
            dests.append(dest.reshape(N_EXPERTS, st.t_g[g]))
            posxs.append(posx)
        dest = jnp.concatenate(dests, axis=1)
        c0, c1 = _block_counts(posxs, st)
        xe = _dispatch(c0, c1, region_start, group_first, group_cap, dest, h2, st)
        ye = _moe_ffn(xe, wg, wu, wd, l, st)
        x = _combine(c0, c1, region_start, dest, x1, ye, st)

    fw = final_norm_w.reshape(1, D_MODEL)
    return [_final_norm(x, fw, st, g).reshape(v.shape) for g, v in enumerate(xs)]


def kernel(x_prompt, x_sample, norm1_w, w_in, ssd_conv_w, ssd_conv_b, ssd_dt_bias, ssd_a_log, ssd_d, ssd_norm_w, w_ssd_out, sc_conv_w, w_sc_out, gate_b, w_o, norm2_w, router_w, w_gate, w_up, w_down, final_norm_w):
    params = (norm1_w, w_in, ssd_conv_w, ssd_conv_b, ssd_dt_bias, ssd_a_log, ssd_d, ssd_norm_w, w_ssd_out, sc_conv_w,
              w_sc_out, gate_b, w_o, norm2_w, router_w, w_gate, w_up, w_down, final_norm_w)
    y_prompt, y_sample = _trunk_stream([x_prompt, x_sample], params, tm=1024, tq=512, tk=256, ts=512, tc=1024, cps=4)
    return (y_prompt, y_sample)
```

```python
import functools

import numpy as np
import jax
import jax.numpy as jnp
from jax import lax
from jax.experimental import pallas as pl
from jax.experimental.pallas import tpu as pltpu

F32 = jnp.float32
BF16 = jnp.bfloat16
I32 = jnp.int32

D_MODEL = 1024
HEADS = 24
HEAD_DIM = 64
INNER = HEADS * HEAD_DIM
GROUPS = 4
GROUP_W = INNER // GROUPS
STATE = 128
CHUNK = 128
SSD_CONV = 5
SC_WIDTH = 512
SC_CONV = 3
N_EXPERTS = 16
EXPERT_FF = 2048
CAPACITY_FACTOR = 2
EPS = 1e-6
LANES = 128
SUBLANES = 8

_OFF_XBC = INNER
_OFF_DT = _OFF_XBC + INNER + 2 * GROUPS * STATE
_OFF_SCB = _OFF_DT + 2 * HEADS
_OFF_GATE = _OFF_SCB + 3 * SC_WIDTH
_N_IN = _OFF_GATE + 2 * D_MODEL
COL_X = INNER
COL_B = 2 * INNER
COL_C = COL_B + GROUPS * STATE
COL_GATE = COL_C + GROUPS * STATE
COL_SCB = COL_GATE + 2 * D_MODEL
N_MAIN = COL_SCB + 3 * SC_WIDTH
CB = 512

BF16_ROWS = 2 * SUBLANES
PASS = 48
WIN = PASS + BF16_ROWS
XW = D_MODEL + LANES
GATE_STRIDE = N_EXPERTS

VMEM_LIMIT = 50 * 1024 * 1024
FUSED_VMEM_LIMIT = 58 * 1024 * 1024


def _cparams(sem):
    return pltpu.CompilerParams(dimension_semantics=sem, vmem_limit_bytes=VMEM_LIMIT)


def _sigmoid(v):
    return 0.5 * jnp.tanh(0.5 * v) + 0.5


def _split3(v):
    hi = v.astype(BF16)
    r = v - hi.astype(F32)
    mid = r.astype(BF16)
    lo = (r - mid.astype(F32)).astype(BF16)
    return hi, mid, lo


def _dot(a, b):
    return jnp.dot(a, b, preferred_element_type=F32)


def _dot_nt(a, b):
    return lax.dot_general(a, b, (((1,), (1,)), ((), ())), preferred_element_type=F32)


def _dot_tn(a, b):
    return lax.dot_general(a, b, (((0,), (0,)), ((), ())), preferred_element_type=F32)


def _dot3_l(a_f32, b_exact):
    hi, mid, lo = _split3(a_f32)
    return _dot(hi, b_exact) + _dot(mid, b_exact) + _dot(lo, b_exact)


def _dot3_r(a_exact, b_f32):
    hi, mid, lo = _split3(b_f32)
    return _dot(a_exact, hi) + _dot(a_exact, mid) + _dot(a_exact, lo)


class _Stream:
    def __init__(self, groups, tm, tq, tk, ts, tc, cps):
        self.groups = tuple(groups)
        self.tm, self.tq, self.tk, self.ts, self.tc, self.cps = tm, tq, tk, ts, tc, cps
        self.t_g = [n * l for n, l in self.groups]
        self.off_g = [int(v) for v in np.cumsum([0] + self.t_g[:-1])]
        self.total = int(sum(self.t_g))
        self.cap_g = [CAPACITY_FACTOR * t // N_EXPERTS for t in self.t_g]
        self.nkb_g = [t // tk for t in self.t_g]
        for (n, l), t, c in zip(self.groups, self.t_g, self.cap_g):
            assert l % (CHUNK * cps) == 0 and l % tq == 0 and l % tc == 0 and t % tk == 0 and c % ts == 0
            assert t % tm == 0 and tk % LANES == 0 and tc % CONV_ROWS == 0
        assert ts >= WIN
        self.region_g = [c + ts for c in self.cap_g]
        self.rows_per_expert = int(sum(self.region_g))
        self.tiles_per_expert = self.rows_per_expert // ts
        self.slot_rows = N_EXPERTS * self.rows_per_expert
        self.region_off_g = [int(v) for v in np.cumsum([0] + self.region_g[:-1])]

    def block_tables(self):
        rs, gfirst, cap = [], [], []
        for g, nkb in enumerate(self.nkb_g):
            for kb in range(nkb):
                gfirst.append(1 if kb == 0 else 0)
                cap.append(self.cap_g[g])
                rs.extend(e * self.rows_per_expert + self.region_off_g[g] for e in range(N_EXPERTS))
        return tuple(jnp.asarray(np.asarray(v, np.int32)) for v in (rs, gfirst, cap))

    def tile_tables(self):
        pad, src = [], []
        for e in range(N_EXPERTS):
            for g, c in enumerate(self.cap_g):
                n = c // self.ts
                base = len(pad)
                pad.extend([0] * n + [1])
                src.extend(list(range(base, base + n)) + [base + n - 1])
        return jnp.asarray(np.asarray(pad, np.int32)), jnp.asarray(np.asarray(src, np.int32))

    def seq_flags(self, tile):
        n_tiles = self.total // tile
        first = np.zeros((n_tiles,), np.int32)
        last = np.zeros((n_tiles,), np.int32)
        for (n, l), off in zip(self.groups, self.off_g):
            for s in range(n):
                first[(off + s * l) // tile] = 1
                last[(off + (s + 1) * l) // tile - 1] = 1
        return jnp.asarray(first), jnp.asarray(last)


def _inproj_body(x_ref, nw_ref, w_ref, wdt_ref, proj_ref, dt_ref, h_scr):
    @pl.when(pl.program_id(1) == 0)
    def _():
        x = x_ref[...]
        ms = jnp.mean(x * x, axis=-1, keepdims=True)
        h = (x * lax.rsqrt(ms + EPS) * nw_ref[...]).astype(BF16)
        h_scr[...] = h
        dt_ref[...] = _dot(h, wdt_ref[...])

    proj_ref[...] = _dot(h_scr[...], w_ref[...]).astype(BF16)


def _inproj(x, nw, w_main, w_dt, layer, st):
    t, tm = st.total, st.tm
    tn = N_MAIN // 2
    return pl.pallas_call(
        _inproj_body,
        grid=(t // tm, N_MAIN // tn),
        in_specs=[
            pl.BlockSpec((tm, D_MODEL), lambda i, n: (i, 0)),
            pl.BlockSpec((None, 1, D_MODEL), lambda i, n: (layer, 0, 0)),
            pl.BlockSpec((None, D_MODEL, tn), lambda i, n: (layer, 0, n)),
            pl.BlockSpec((None, D_MODEL, LANES), lambda i, n: (layer, 0, 0)),
        ],
        out_specs=[
            pl.BlockSpec((tm, tn), lambda i, n: (i, n)),
            pl.BlockSpec((tm, LANES), lambda i, n: (i, 0)),
        ],
        out_shape=[jax.ShapeDtypeStruct((t, N_MAIN), BF16), jax.ShapeDtypeStruct((t, LANES), F32)],
        scratch_shapes=[pltpu.VMEM((tm, D_MODEL), BF16)],
        compiler_params=_cparams(("arbitrary", "arbitrary")),
        name="inproj",
    )(x, nw, w_main, w_dt)


CONV_ROWS = 128


def _conv_body(topz, botz, main_ref, prev_ref, next_ref, w_ref, b_ref, o_ref, ext):
    i = pl.program_id(0)
    tq = main_ref.shape[0]
    halo = BF16_ROWS
    pad = SSD_CONV // 2
    zero = jnp.zeros_like(prev_ref[...])
    ext[0:halo, :] = jnp.where(topz[i] == 1, zero, prev_ref[...])
    ext[halo:halo + tq, :] = main_ref[...]
    ext[halo + tq:2 * halo + tq, :] = jnp.where(botz[i] == 1, zero, next_ref[...])
    win = CONV_ROWS + 2 * halo
    taps = [k for k in range(SSD_CONV) if k != pad]
    r = lax.broadcasted_iota(I32, (CONV_ROWS, win), 0)
    c = lax.broadcasted_iota(I32, (CONV_ROWS, win), 1)
    shift = jnp.concatenate([jnp.where(c == r + halo + k - pad, 1.0, 0.0).astype(BF16) for k in taps], axis=0)
    for sb in range(tq // CONV_ROWS):
        r0 = sb * CONV_ROWS
        moved = _dot(shift, ext[r0:r0 + win, :])
        acc = b_ref[...] + ext[halo + r0:halo + r0 + CONV_ROWS, :].astype(F32) * w_ref[pad:pad + 1, :]
        for n, k in enumerate(taps):
            acc = acc + moved[n * CONV_ROWS:(n + 1) * CONV_ROWS, :] * w_ref[k:k + 1, :]
        o_ref[r0:r0 + CONV_ROWS, :] = (acc * _sigmoid(acc)).astype(BF16)


def _conv_xbc(proj, conv_w, conv_b, layer, st, flags):
    t, tq = st.total, st.tc
    ncb = (COL_GATE - COL_X) // CB
    cb0 = COL_X // CB
    rpt = tq // BF16_ROWS
    last_rb = t // BF16_ROWS - 1
    grid_spec = pltpu.PrefetchScalarGridSpec(
        num_scalar_prefetch=2,
        grid=(t // tq, ncb),
        in_specs=[
            pl.BlockSpec((tq, CB), lambda i, j, a, b: (i, cb0 + j)),
            pl.BlockSpec((BF16_ROWS, CB), lambda i, j, a, b: (jnp.maximum(i * rpt - 1, 0), cb0 + j)),
            pl.BlockSpec((BF16_ROWS, CB), lambda i, j, a, b: (jnp.minimum((i + 1) * rpt, last_rb), cb0 + j)),
            pl.BlockSpec((None, SSD_CONV, CB), lambda i, j, a, b: (layer, 0, j)),
            pl.BlockSpec((None, 1, CB), lambda i, j, a, b: (layer, 0, j)),
        ],
        out_specs=pl.BlockSpec((tq, CB), lambda i, j, a, b: (i, j)),
        scratch_shapes=[pltpu.VMEM((tq + 2 * BF16_ROWS, CB), BF16)],
    )
    return pl.pallas_call(
        _conv_body,
        grid_spec=grid_spec,
        out_shape=jax.ShapeDtypeStruct((t, COL_GATE - COL_X), BF16),
        compiler_params=_cparams(("arbitrary", "arbitrary")),
        name="conv_xbc",
    )(flags[0], flags[1], proj, proj, proj, conv_w, conv_b)


def _ssd_scalars(dt_ref, rows, dtb_ref, alog_ref):
    v = dt_ref[rows, :] + dtb_ref[...]
    u = jnp.exp(-jnp.abs(v))
    w = 1.0 + u
    dt = jnp.maximum(v, 0.0) + jnp.where(w == 1.0, u, jnp.log(w) * (u / (w - 1.0)))
    da = dt * (-jnp.exp(alog_ref[...]))
    r = lax.broadcasted_iota(I32, (CHUNK, CHUNK), 0)
    c = lax.broadcasted_iota(I32, (CHUNK, CHUNK), 1)
    tri = jnp.where(c <= r, 1.0, 0.0).astype(BF16)
    upper = jnp.where(r <= c, 1.0, 0.0).astype(BF16)
    cs = _dot3_r(tri, da)
    da_t = da.T
    dt_t = dt.T
    cs_t = _dot3_l(da_t, upper)
    return dt, cs, cs - da, dt_t, cs_t, cs_t - da_t, r, c


def _expand(v, e_ref):
    hi = v.astype(BF16)
    mid = (v - hi.astype(F32)).astype(BF16)
    e = e_ref[...]
    return _dot(hi, e) + _dot(mid, e)


def _expand_pair(a, b, e_ref):
    out = _expand(jnp.concatenate([a, b], axis=0), e_ref)
    return out[:CHUNK, :], out[CHUNK:, :]


def _state_step(s_scr, xs_f32, w_tok, dec_lanes, b_ref, rows):
    xw = (xs_f32 * w_tok).astype(BF16)
    for g in range(GROUPS):
        sl = slice(g * GROUP_W, (g + 1) * GROUP_W)
        sc = _dot_tn(b_ref[rows, g * STATE:(g + 1) * STATE], xw[:, sl])
        s_scr[:, sl] = s_scr[:, sl] * dec_lanes[:, sl] + sc


def _ssd_fwd_body(first, xs_ref, b_ref, c_ref, dt_ref, dtb_ref, alog_ref, ef_ref, y_ref, s_scr):
    cps = xs_ref.shape[0] // CHUNK

    @pl.when(pl.program_id(0) == 0)
    def _():
        s_scr[...] = jnp.zeros_like(s_scr)

    for sub in range(cps):
        rows = slice(sub * CHUNK, (sub + 1) * CHUNK)
        keep = jnp.where(first[pl.program_id(0) * cps + sub] == 1, 0.0, 1.0)
        dt, cs, ex, dt_t, cs_t, ex_t, r, c = _ssd_scalars(dt_ref, rows, dtb_ref, alog_ref)
        w_out, w_st = _expand_pair(jnp.exp(cs) * keep, jnp.exp(cs[CHUNK - 1:CHUNK, :] - cs) * dt, ef_ref)
        for g in range(GROUPS):
            sl = slice(g * GROUP_W, (g + 1) * GROUP_W)
            yo = _dot(c_ref[rows, g * STATE:(g + 1) * STATE], s_scr[:, sl].astype(BF16))
            y_ref[rows, sl] = yo * w_out[:, sl]
        lane = lax.broadcasted_iota(I32, (CHUNK, 2 * HEAD_DIM), 1)
        lower = r >= c
        pairs_per_group = HEADS // GROUPS // 2
        for pi in range(HEADS // 2):
            g = pi // pairs_per_group
            if pi % pairs_per_group == 0:
                cb = _dot_nt(c_ref[rows, g * STATE:(g + 1) * STATE], b_ref[rows, g * STATE:(g + 1) * STATE])
            ms = []
            for h in (2 * pi, 2 * pi + 1):
                hb = HEADS + h
                arg = jnp.where(lower, cs[:, h:h + 1] - cs_t[h:h + 1, :], ex_t[hb:hb + 1, :] - ex[:, hb:hb + 1])
                d0 = dt_t[h:h + 1, :]
                d1 = dt_t[hb:hb + 1, :]
                w = jnp.where(r > c, d0, jnp.where(r < c, d1, d0 + d1))
                ms.append((jnp.exp(arg) * w * cb).astype(BF16))
            psl = slice(pi * 2 * HEAD_DIM, (pi + 1) * 2 * HEAD_DIM)
            xp = xs_ref[rows, psl]
            zero = jnp.zeros_like(xp)
            rhs = jnp.concatenate([jnp.where(lane < HEAD_DIM, xp, zero), jnp.where(lane >= HEAD_DIM, xp, zero)],
                                  axis=0)
            y_ref[rows, psl] = y_ref[rows, psl] + _dot(jnp.concatenate(ms, axis=1), rhs)
        _state_step(s_scr, xs_ref[rows, :].astype(F32), w_st, w_out[CHUNK - 1:CHUNK, :], b_ref, rows)


def _ssd_bwd_rows(last, step, xs_ref, b_ref, c_ref, dt_ref, y1_ref, z_ref, dtb_ref, alog_ref, eb_ref,
                  dskip_ref, nw_ref, s_scr, yn_scr, between=()):
    cps = xs_ref.shape[0] // CHUNK
    between = list(between)
    for sub in reversed(range(cps)):
        if between:
            between.pop(0)()
        rows = slice(sub * CHUNK, (sub + 1) * CHUNK)
        keep = jnp.where(last[step * cps + sub] == 1, 0.0, 1.0)
        dt, cs, ex, dt_t, cs_t, ex_t, r, c = _ssd_scalars(dt_ref, rows, dtb_ref, alog_ref)
        tot = cs[CHUNK - 1:CHUNK, :]
        w_out, w_st = _expand_pair(jnp.exp(tot - ex) * keep, jnp.exp(ex) * dt, eb_ref)
        xs = xs_ref[rows, :].astype(F32)
        z = z_ref[rows, :].astype(F32)
        zs = z * _sigmoid(z)
        for g in range(GROUPS):
            sl = slice(g * GROUP_W, (g + 1) * GROUP_W)
            yo = _dot(c_ref[rows, g * STATE:(g + 1) * STATE], s_scr[:, sl].astype(BF16))
            y = (y1_ref[rows, sl] + yo * w_out[:, sl] + xs[:, sl] * dskip_ref[:, sl]) * zs[:, sl]
            y = y * lax.rsqrt(jnp.mean(y * y, axis=-1, keepdims=True) + EPS)
            yn_scr[rows, sl] = (y * nw_ref[:, sl]).astype(BF16)
        _state_step(s_scr, xs, w_st, w_out[0:1, :], b_ref, rows)
    for rest in between:
        rest()


def _expansion_matrices():
    rows = np.arange(LANES)[:, None]
    cols = np.arange(INNER)[None, :] // HEAD_DIM
    ef = (rows == cols).astype(np.float32)
    eb = (rows == cols + HEADS).astype(np.float32)
    return jnp.asarray(ef, BF16), jnp.asarray(eb, BF16)


def _ssd_fwd(xbc, dt_raw, dtb, alog, ef, layer, st, first):
    t = st.total
    nb = GROUPS * STATE // CB
    rows = st.cps * CHUNK
    grid_spec = pltpu.PrefetchScalarGridSpec(
        num_scalar_prefetch=1,
        grid=(t // rows,),
        in_specs=[
            pl.BlockSpec((rows, INNER), lambda i, f: (i, 0)),
            pl.BlockSpec((rows, CB), lambda i, f: (i, INNER // CB)),
            pl.BlockSpec((rows, CB), lambda i, f: (i, INNER // CB + nb)),
            pl.BlockSpec((rows, LANES), lambda i, f: (i, 0)),
            pl.BlockSpec((None, 1, LANES), lambda i, f: (layer, 0, 0)),
            pl.BlockSpec((None, 1, LANES), lambda i, f: (layer, 0, 0)),
            pl.BlockSpec((LANES, INNER), lambda i, f: (0, 0)),
        ],
        out_specs=pl.BlockSpec((rows, INNER), lambda i, f: (i, 0)),
        scratch_shapes=[pltpu.VMEM((STATE, INNER), F32)],
    )
    return pl.pallas_call(
        _ssd_fwd_body,
        grid_spec=grid_spec,
        out_shape=jax.ShapeDtypeStruct((t, INNER), F32),
        compiler_params=_cparams(("arbitrary",)),
        name="ssd_fwd",
    )(first, xbc, xbc, xbc, dt_raw, dtb, alog, ef)


def _merge_stages(i, topz, botz, x_ref, o_ssd, gate_ref, scb_ref, scc_ref, scx_ref, ccp_ref, cxp_ref, ccn_ref, cxn_ref,
                  cw_ref, wsc_ref, gb_ref, wo_ref, nw_ref, rw_ref, x1_ref, h2_ref, pt_ref, ext):
    tm = x_ref.shape[0]
    halo = SUBLANES
    pad = SC_CONV // 2
    live = {}

    def short_conv():
        ext[0:halo, :] = jnp.where(topz[i] == 1, 0.0, ccp_ref[...].astype(F32) * cxp_ref[...].astype(F32))
        ext[halo:halo + tm, :] = scc_ref[...].astype(F32) * scx_ref[...].astype(F32)
        ext[halo + tm:2 * halo + tm, :] = jnp.where(botz[i] == 1, 0.0,
                                                    ccn_ref[...].astype(F32) * cxn_ref[...].astype(F32))
        v = ext[halo - pad:halo - pad + tm, :] * cw_ref[0:1, :]
        for k in range(1, SC_CONV):
            v = v + ext[halo - pad + k:halo - pad + k + tm, :] * cw_ref[k:k + 1, :]
        live["o_sc"] = _dot((scb_ref[...].astype(F32) * v).astype(BF16), wsc_ref[...])

    def gated_mix():
        g = _sigmoid(gate_ref[...].astype(F32) + gb_ref[...])
        mix = g[:, :D_MODEL] * o_ssd.astype(F32) + g[:, D_MODEL:] * live["o_sc"]
        x1 = x_ref[...] + _dot(mix.astype(BF16), wo_ref[...])
        x1_ref[...] = x1
        live["x1"] = x1

    def norm():
        x1 = live["x1"]
        h2 = x1 * lax.rsqrt(jnp.mean(x1 * x1, axis=-1, keepdims=True) + EPS) * nw_ref[...]
        h2_ref[:, :D_MODEL] = h2.astype(BF16)
        live["h2"] = h2

    def router():
        hh, hm, _ = _split3(live["h2"])
        rh, rm, _ = _split3(rw_ref[...])
        logits = _dot_nt(rh, hh) + _dot_nt(rh, hm) + _dot_nt(rm, hh)
        mx = jnp.max(logits, axis=0, keepdims=True)
        ex = jnp.exp(logits - mx)
        probs = ex / jnp.sum(ex, axis=0, keepdims=True)
        pt_ref[...] = probs
        pk = jnp.concatenate([probs, jnp.zeros((LANES - N_EXPERTS, tm), F32)], axis=0).T
        hi = pk.astype(BF16).astype(F32)
        mid = (pk - hi).astype(BF16).astype(F32)
        lo = pk - hi - mid
        lane = lax.broadcasted_iota(I32, (tm, LANES), 1)
        terms = jnp.where(lane < GATE_STRIDE, hi,
                          jnp.where(lane < 2 * GATE_STRIDE, pltpu.roll(mid, GATE_STRIDE, 1),
                                    jnp.where(lane < 3 * GATE_STRIDE, pltpu.roll(lo, 2 * GATE_STRIDE, 1), 0.0)))
        h2_ref[:, D_MODEL:] = terms.astype(BF16)

    return [short_conv, gated_mix, norm, router]


def _bwd_merge_body(last, topz, botz,
                    xs_ref, b_ref, c_ref, dt_ref, y1_ref, z_ref, dtb_ref, alog_ref, eb_ref, dskip_ref, snw_ref, wout_ref,
                    x_ref, gate_ref, scb_ref, scc_ref, scx_ref, ccp_ref, cxp_ref, ccn_ref, cxn_ref,
                    cw_ref, wsc_ref, gb_ref, wo_ref, nw_ref, rw_ref,
                    x1_ref, h2_ref, pt_ref, s_scr, yn_scr, ossd_scr, ext):
    i = pl.program_id(0)
    n_blocks = pl.num_programs(0) - 1

    @pl.when(i == 0)
    def _():
        s_scr[...] = jnp.zeros_like(s_scr)
        ossd_scr[...] = jnp.zeros_like(ossd_scr)

    slot = i % 2
    merge_tile = n_blocks - 1 - jnp.maximum(i - 1, 0)
    stages = _merge_stages(merge_tile, topz, botz, x_ref, ossd_scr[1 - slot], gate_ref, scb_ref, scc_ref, scx_ref,
                           ccp_ref, cxp_ref, ccn_ref, cxn_ref, cw_ref, wsc_ref, gb_ref, wo_ref, nw_ref, rw_ref,
                           x1_ref, h2_ref, pt_ref, ext)
    ssd_tile = n_blocks - 1 - jnp.minimum(i, n_blocks - 1)
    _ssd_bwd_rows(last, ssd_tile, xs_ref, b_ref, c_ref, dt_ref, y1_ref, z_ref, dtb_ref, alog_ref, eb_ref,
                  dskip_ref, snw_ref, s_scr, yn_scr, between=stages)
    ossd_scr[slot] = _dot(yn_scr[...], wout_ref[...]).astype(BF16)


def _ssd_bwd_merge(x, xbc, dt_raw, y1, proj, dtb, alog, eb, dskip, ssd_norm, w_ssd_out, sc_w, w_sc, gate_b, w_o, norm2,
                   router_t, layer, st, last, flags):
    t, tm = st.total, st.tq
    assert tm % CHUNK == 0
    nc = t // tm
    nb = GROUPS * STATE // CB
    rpt = tm // SUBLANES
    last_rb = t // SUBLANES - 1
    cscb, cscc, cscx = COL_SCB // CB, COL_SCB // CB + 1, COL_SCB // CB + 2
    sb = lambda i: nc - 1 - jnp.minimum(i, nc - 1)
    mb = lambda i: nc - 1 - jnp.maximum(i - 1, 0)
    prev = lambda i: jnp.maximum(mb(i) * rpt - 1, 0)
    nxt = lambda i: jnp.minimum((mb(i) + 1) * rpt, last_rb)
    const = lambda *idx: (lambda i, *_: idx)
    grid_spec = pltpu.PrefetchScalarGridSpec(
        num_scalar_prefetch=3,
        grid=(nc + 1,),
        in_specs=[
            pl.BlockSpec((tm, INNER), lambda i, *_: (sb(i), 0)),
            pl.BlockSpec((tm, CB), lambda i, *_: (sb(i), INNER // CB)),
            pl.BlockSpec((tm, CB), lambda i, *_: (sb(i), INNER // CB + nb)),
            pl.BlockSpec((tm, LANES), lambda i, *_: (sb(i), 0)),
            pl.BlockSpec((tm, INNER), lambda i, *_: (sb(i), 0)),
            pl.BlockSpec((tm, INNER), lambda i, *_: (sb(i), 0)),
            pl.BlockSpec((None, 1, LANES), const(layer, 0, 0)),
            pl.BlockSpec((None, 1, LANES), const(layer, 0, 0)),
            pl.BlockSpec((LANES, INNER), const(0, 0)),
            pl.BlockSpec((None, 1, INNER), const(layer, 0, 0)),
            pl.BlockSpec((None, 1, INNER), const(layer, 0, 0)),
            pl.BlockSpec((None, INNER, D_MODEL), const(layer, 0, 0)),
            pl.BlockSpec((tm, D_MODEL), lambda i, *_: (mb(i), 0)),
            pl.BlockSpec((tm, 2 * D_MODEL), lambda i, *_: (mb(i), COL_GATE // (2 * D_MODEL))),
            pl.BlockSpec((tm, CB), lambda i, *_: (mb(i), cscb)),
            pl.BlockSpec((tm, CB), lambda i, *_: (mb(i), cscc)),
            pl.BlockSpec((tm, CB), lambda i, *_: (mb(i), cscx)),
            pl.BlockSpec((SUBLANES, CB), lambda i, *_: (prev(i), cscc)),
            pl.BlockSpec((SUBLANES, CB), lambda i, *_: (prev(i), cscx)),
            pl.BlockSpec((SUBLANES, CB), lambda i, *_: (nxt(i), cscc)),
            pl.BlockSpec((SUBLANES, CB), lambda i, *_: (nxt(i), cscx)),
            pl.BlockSpec((None, SC_CONV, SC_WIDTH), const(layer, 0, 0)),
            pl.BlockSpec((None, SC_WIDTH, D_MODEL), const(layer, 0, 0)),
            pl.BlockSpec((None, 1, 2 * D_MODEL), const(layer, 0, 0)),
            pl.BlockSpec((None, D_MODEL, D_MODEL), const(layer, 0, 0)),
            pl.BlockSpec((None, 1, D_MODEL), const(layer, 0, 0)),
            pl.BlockSpec((None, N_EXPERTS, D_MODEL), const(layer, 0, 0)),
        ],
        out_specs=[
            pl.BlockSpec((tm, D_MODEL), lambda i, *_: (mb(i), 0)),
            pl.BlockSpec((tm, XW), lambda i, *_: (mb(i), 0)),
            pl.BlockSpec((N_EXPERTS, tm), lambda i, *_: (0, mb(i))),
        ],
        scratch_shapes=[
            pltpu.VMEM((STATE, INNER), F32),
            pltpu.VMEM((tm, INNER), BF16),
            pltpu.VMEM((2, tm, D_MODEL), BF16),
            pltpu.VMEM((tm + 2 * SUBLANES, SC_WIDTH), F32),
        ],
    )
    return pl.pallas_call(
        _bwd_merge_body,
        grid_spec=grid_spec,
        out_shape=[jax.ShapeDtypeStruct((t, D_MODEL), F32), jax.ShapeDtypeStruct((t, XW), BF16),
                   jax.ShapeDtypeStruct((N_EXPERTS, t), F32)],
        compiler_params=pltpu.CompilerParams(dimension_semantics=("arbitrary",), vmem_limit_bytes=FUSED_VMEM_LIMIT),
        name="ssd_bwd_merge",
    )(last, flags[0], flags[1], xbc, xbc, xbc, dt_raw, y1, proj, dtb, alog, eb, dskip, ssd_norm, w_ssd_out,
      x, proj, proj, proj, proj, proj, proj, proj, proj, sc_w, w_sc, gate_b, w_o, norm2, router_t)


def _route_body(p_ref, dest_ref, posx_ref, *, cap):
    rows = p_ref.shape[1]
    r = lax.broadcasted_iota(I32, (LANES, LANES), 0)
    c = lax.broadcasted_iota(I32, (LANES, LANES), 1)
    upper = jnp.where(r <= c, 1.0, 0.0).astype(BF16)
    last_col = jnp.where(r == LANES - 1, 1.0, 0.0).astype(BF16)
    rr = lax.broadcasted_iota(I32, (rows, rows), 0)
    rc = lax.broadcasted_iota(I32, (rows, rows), 1)
    below = jnp.where(rc < rr, 1.0, 0.0).astype(BF16)

    def count(mask):
        s = jnp.sum(jnp.where(mask, 1.0, 0.0), axis=0, keepdims=True)
        return jnp.sum(s, axis=1, keepdims=True)

    def excl_prefix(mask):
        x = jnp.where(mask, 1.0, 0.0)
        incl = _dot(x.astype(BF16), upper)
        row_tot = _dot(incl.astype(BF16), last_col)
        row_off = _dot(below, row_tot.astype(BF16))
        return incl + row_off - x

    def search(i, prefix):
        bit = jnp.left_shift(jnp.int32(1), 30 - i)
        out = []
        for e in range(N_EXPERTS):
            keys = pltpu.bitcast(p_ref[e], I32)
            cand = prefix[e] | bit
            out.append(jnp.where(count(keys >= cand) >= cap, cand, prefix[e]))
        return tuple(out)

    thr = lax.fori_loop(0, 31, search, tuple(jnp.zeros((1, 1), I32) for _ in range(N_EXPERTS)))
    for e in range(N_EXPERTS):
        keys = pltpu.bitcast(p_ref[e], I32)
        gt = keys > thr[e]
        eq = keys == thr[e]
        need = cap - count(gt)
        sel = gt | (eq & (excl_prefix(eq) < need))
        pos = excl_prefix(sel).astype(I32)
        posx_ref[e] = pos
        dest_ref[e] = jnp.where(sel, pos, -1)


def _route(probs3, cap):
    e, rows, _ = probs3.shape
    return pl.pallas_call(
        functools.partial(_route_body, cap=cap),
        out_shape=[jax.ShapeDtypeStruct((e, rows, LANES), I32), jax.ShapeDtypeStruct((e, rows, LANES), I32)],
        compiler_params=pltpu.CompilerParams(vmem_limit_bytes=VMEM_LIMIT),
        name="route",
    )(probs3)


def _block_counts(posx_list, st):
    rb = st.tk // LANES
    c0s, c1s = [], []
    for g, posx in enumerate(posx_list):
        c0 = posx[:, ::rb, 0]
        c1 = jnp.concatenate([c0[:, 1:], jnp.full((N_EXPERTS, 1), st.cap_g[g], I32)], axis=1)
        c0s.append(c0.T.reshape(-1))
        c1s.append(c1.T.reshape(-1))
    return jnp.concatenate(c0s), jnp.concatenate(c1s)


def _pass_window(c0_s, c1_s, rs_s, kb, e, k):
    c0 = c0_s[kb * N_EXPERTS + e]
    c1 = c1_s[kb * N_EXPERTS + e]
    lo = jnp.minimum(c0 + k * PASS, c1)
    base = (lo // BF16_ROWS) * BF16_ROWS
    row = pl.multiple_of(rs_s[kb * N_EXPERTS + e] + base, BF16_ROWS)
    return lo, base, row, jnp.minimum(lo + PASS, c1)


def _num_passes(c0_s, c1_s, kb):
    most = c1_s[kb * N_EXPERTS] - c0_s[kb * N_EXPERTS]
    for e in range(1, N_EXPERTS):
        most = jnp.maximum(most, c1_s[kb * N_EXPERTS + e] - c0_s[kb * N_EXPERTS + e])
    return (most + PASS - 1) // PASS


def _window_one_hot(c0_s, c1_s, rs_s, dest_ref, kb, k):
    tk = dest_ref.shape[1]
    slot = lax.broadcasted_iota(I32, (WIN, tk), 0)
    parts = []
    for e in range(N_EXPERTS):
        lo, base, _, _ = _pass_window(c0_s, c1_s, rs_s, kb, e, k)
        d = dest_ref[e:e + 1, :]
        rel = jnp.where(d >= lo, jnp.where(d < lo + PASS, d - base, -1), -1)
        parts.append(jnp.where(rel == slot, 1.0, 0.0).astype(BF16))
    return jnp.concatenate(parts, axis=0)


def _dispatch_body(c0_s, c1_s, rs_s, gfirst_s, cap_s, dest_ref, h_ref, xe_hbm, stage, carry, zeros, sem, cnt):
    kb = pl.program_id(0)

    def window_copy(slot, e, row):
        return pltpu.make_async_copy(stage.at[slot, pl.ds(e * WIN, WIN), :], xe_hbm.at[pl.ds(row, WIN), :], sem.at[0])

    def wait_all():
        for e in range(N_EXPERTS):
            window_copy(0, e, 0).wait()

    def pad_copy(e):
        row = pl.multiple_of(rs_s[kb * N_EXPERTS + e] + cap_s[kb], BF16_ROWS)
        return pltpu.make_async_copy(zeros, xe_hbm.at[pl.ds(row, zeros.shape[0]), :], sem.at[1])

    @pl.when(kb == 0)
    def _():
        cnt[0] = 0

    @pl.when(gfirst_s[kb] == 1)
    def _():
        carry[...] = jnp.zeros_like(carry)
        zeros[...] = jnp.zeros_like(zeros)
        for e in range(N_EXPERTS):
            pad_copy(e).start()
        for e in range(N_EXPERTS):
            pad_copy(e).wait()

    def one_pass(k, _):
        slot = cnt[0] % 2
        packed = _dot(_window_one_hot(c0_s, c1_s, rs_s, dest_ref, kb, k), h_ref[...])
        rows = []
        for e in range(N_EXPERTS):
            lo, base, row, nxt = _pass_window(c0_s, c1_s, rs_s, kb, e, k)
            head = packed[e * WIN:e * WIN + BF16_ROWS, :] + carry[e].astype(F32)
            stage[slot, e * WIN:e * WIN + BF16_ROWS, :] = head.astype(BF16)
            stage[slot, e * WIN + BF16_ROWS:(e + 1) * WIN, :] = packed[e * WIN + BF16_ROWS:(e + 1) * WIN, :].astype(BF16)
            off = pl.multiple_of(e * WIN + (nxt // BF16_ROWS) * BF16_ROWS - base, BF16_ROWS)
            carry[e] = stage[slot, pl.ds(off, BF16_ROWS), :]
            rows.append(row)

        @pl.when(cnt[0] > 0)
        def _():
            wait_all()

        for e in range(N_EXPERTS):
            window_copy(slot, e, rows[e]).start()
        cnt[0] = cnt[0] + 1
        return 0

    lax.fori_loop(0, _num_passes(c0_s, c1_s, kb), one_pass, 0)

    @pl.when((kb == pl.num_programs(0) - 1) & (cnt[0] > 0))
    def _():
        wait_all()


def _dispatch(c0, c1, rs, gfirst, cap, dest, h2a, st):
    tk = st.tk
    grid_spec = pltpu.PrefetchScalarGridSpec(
        num_scalar_prefetch=5,
        grid=(st.total // tk,),
        in_specs=[
            pl.BlockSpec((N_EXPERTS, tk), lambda i, *_: (0, i)),
            pl.BlockSpec((tk, XW), lambda i, *_: (i, 0)),
        ],
        out_specs=pl.BlockSpec(memory_space=pl.ANY),
        scratch_shapes=[
            pltpu.VMEM((2, N_EXPERTS * WIN, XW), BF16),
            pltpu.VMEM((N_EXPERTS, BF16_ROWS, XW), BF16),
            pltpu.VMEM((st.ts, XW), BF16),
            pltpu.SemaphoreType.DMA((2,)),
            pltpu.SMEM((1,), I32),
        ],
    )
    return pl.pallas_call(
        _dispatch_body,
        grid_spec=grid_spec,
        out_shape=jax.ShapeDtypeStruct((st.slot_rows, XW), BF16),
        compiler_params=_cparams(("arbitrary",)),
        name="moe_dispatch",
    )(c0, c1, rs, gfirst, cap, dest, h2a)


def _moe_ffn_body(pad_s, src_s, xe_ref, wg_ref, wu_ref, wd_ref, o_ref, *, tiles_per_expert):
    i = pl.program_id(0)

    @pl.when(pad_s[i] == 1)
    def _():
        o_ref[...] = jnp.zeros_like(o_ref)

    @pl.when(pad_s[i] == 0)
    def _():
        e = i // tiles_per_expert
        xe = xe_ref[:, :D_MODEL]
        terms = xe_ref[:, D_MODEL:].astype(F32)
        lane = lax.broadcasted_iota(I32, terms.shape, 1)
        mine = (lane < 3 * GATE_STRIDE) & (lane % GATE_STRIDE == e)
        gate = jnp.sum(jnp.where(mine, terms, 0.0), axis=1, keepdims=True)
        a = _dot(xe, wg_ref[...])
        u = _dot(xe, wu_ref[...])
        act = (a * _sigmoid(a) * u).astype(BF16)
        o_ref[...] = (_dot(act, wd_ref[...]) * gate).astype(BF16)


def _moe_ffn(xe, wg, wu, wd, layer, st):
    ts = st.ts
    tpe = st.tiles_per_expert
    pad, src = st.tile_tables()
    grid_spec = pltpu.PrefetchScalarGridSpec(
        num_scalar_prefetch=2,
        grid=(N_EXPERTS * tpe,),
        in_specs=[
            pl.BlockSpec((ts, XW), lambda i, p, s: (s[i], 0)),
            pl.BlockSpec((None, None, D_MODEL, EXPERT_FF), lambda i, p, s: (layer, i // tpe, 0, 0)),
            pl.BlockSpec((None, None, D_MODEL, EXPERT_FF), lambda i, p, s: (layer, i // tpe, 0, 0)),
            pl.BlockSpec((None, None, EXPERT_FF, D_MODEL), lambda i, p, s: (layer, i // tpe, 0, 0)),
        ],
        out_specs=pl.BlockSpec((ts, D_MODEL), lambda i, p, s: (i, 0)),
    )
    return pl.pallas_call(
        functools.partial(_moe_ffn_body, tiles_per_expert=tpe),
        grid_spec=grid_spec,
        out_shape=jax.ShapeDtypeStruct((st.slot_rows, D_MODEL), BF16),
        compiler_params=_cparams(("arbitrary",)),
        name="moe_ffn",
    )(pad, src, xe, wg, wu, wd)


OVERFLOW_SLOT = 2


def _combine_body(c0_s, c1_s, rs_s, dest_ref, x_ref, ye_hbm, o_ref, buf, sem):
    kb = pl.program_id(0)
    nkb = pl.num_programs(0)

    def window_copy(slot, e, row):
        return pltpu.make_async_copy(ye_hbm.at[pl.ds(row, WIN), :], buf.at[slot, pl.ds(e * WIN, WIN), :], sem.at[slot])

    def fetch(block, k, slot):
        for e in range(N_EXPERTS):
            window_copy(slot, e, _pass_window(c0_s, c1_s, rs_s, block, e, k)[2]).start()

    def wait_all(slot):
        for e in range(N_EXPERTS):
            window_copy(slot, e, 0).wait()

    @pl.when(kb == 0)
    def _():
        fetch(0, 0, 0)

    slot = kb % 2
    wait_all(slot)

    @pl.when(kb + 1 < nkb)
    def _():
        fetch(kb + 1, 0, 1 - slot)

    o_ref[...] = x_ref[...] + _dot_tn(_window_one_hot(c0_s, c1_s, rs_s, dest_ref, kb, 0), buf[slot])

    def extra_pass(k, _):
        fetch(kb, k, OVERFLOW_SLOT)
        wait_all(OVERFLOW_SLOT)
        o_ref[...] += _dot_tn(_window_one_hot(c0_s, c1_s, rs_s, dest_ref, kb, k), buf[OVERFLOW_SLOT])
        return 0

    lax.fori_loop(1, _num_passes(c0_s, c1_s, kb), extra_pass, 0)


def _combine(c0, c1, rs, dest, x1, ye, st):
    tk = st.tk
    grid_spec = pltpu.PrefetchScalarGridSpec(
        num_scalar_prefetch=3,
        grid=(st.total // tk,),
        in_specs=[
            pl.BlockSpec((N_EXPERTS, tk), lambda i, *_: (0, i)),
            pl.BlockSpec((tk, D_MODEL), lambda i, *_: (i, 0)),
            pl.BlockSpec(memory_space=pl.ANY),
        ],
        out_specs=pl.BlockSpec((tk, D_MODEL), lambda i, *_: (i, 0)),
        scratch_shapes=[
            pltpu.VMEM((3, N_EXPERTS * WIN, D_MODEL), BF16),
            pltpu.SemaphoreType.DMA((3,)),
        ],
    )
    return pl.pallas_call(
        _combine_body,
        grid_spec=grid_spec,
        out_shape=jax.ShapeDtypeStruct((st.total, D_MODEL), F32),
        compiler_params=_cparams(("arbitrary",)),
        name="moe_combine",
    )(c0, c1, rs, dest, x1, ye)


def _final_norm_body(x_ref, w_ref, o_ref):
    x = x_ref[...]
    o_ref[...] = x * lax.rsqrt(jnp.mean(x * x, axis=-1, keepdims=True) + EPS) * w_ref[...]


def _final_norm(x, w, st, group):
    t, tm = st.t_g[group], st.tm
    first = st.off_g[group] // tm
    return pl.pallas_call(
        _final_norm_body,
        grid=(t // tm,),
        in_specs=[pl.BlockSpec((tm, D_MODEL), lambda i: (first + i, 0)), pl.BlockSpec((1, D_MODEL), lambda i: (0, 0))],
        out_specs=pl.BlockSpec((tm, D_MODEL), lambda i: (i, 0)),
        out_shape=jax.ShapeDtypeStruct((t, D_MODEL), F32),
        compiler_params=_cparams(("arbitrary",)),
        name="final_norm",
    )(x, w)


def _pad_lanes(v, width=LANES):
    return jnp.pad(v, [(0, 0)] * (v.ndim - 1) + [(0, width - v.shape[-1])])


def _trunk_stream(xs, params, tm, tq, tk, ts, tc, cps):
    (norm1_w, w_in, ssd_conv_w, ssd_conv_b, ssd_dt_bias, ssd_a_log, ssd_d, ssd_norm_w, w_ssd_out, sc_conv_w,
     w_sc_out, gate_b, w_o, norm2_w, router_w, w_gate, w_up, w_down, final_norm_w) = params
    depth = w_in.shape[0]
    st = _Stream([(x.shape[0], x.shape[1]) for x in xs], tm, tq, tk, ts, tc, cps)
    x = jnp.concatenate([v.reshape(-1, D_MODEL) for v in xs], axis=0)

    w_main = jnp.concatenate([w_in[:, :, :_OFF_DT], w_in[:, :, _OFF_GATE:], w_in[:, :, _OFF_SCB:_OFF_GATE]],
                             axis=2).astype(BF16)
    w_dt = _pad_lanes(w_in[:, :, _OFF_DT:_OFF_SCB]).astype(BF16)
    dtb = _pad_lanes(ssd_dt_bias.reshape(depth, 1, 2 * HEADS))
    alog = _pad_lanes(ssd_a_log.reshape(depth, 1, 2 * HEADS))
    dskip = jnp.repeat(ssd_d, HEAD_DIM, axis=-1).reshape(depth, 1, INNER)
    ef, eb = _expansion_matrices()
    router_t = jnp.swapaxes(router_w, 1, 2)
    wg, wu, wd = w_gate.astype(BF16), w_up.astype(BF16), w_down.astype(BF16)
    row = lambda v: v.reshape(depth, 1, v.shape[-1])

    chunk_first, chunk_last = st.seq_flags(CHUNK)
    tile_flags = st.seq_flags(tq)
    conv_flags = st.seq_flags(tc)
    region_start, group_first, group_cap = st.block_tables()

    for l in range(depth):
        proj, dt_raw = _inproj(x, row(norm1_w), w_main, w_dt, l, st)
        xbc = _conv_xbc(proj, ssd_conv_w, row(ssd_conv_b), l, st, conv_flags)
        y1 = _ssd_fwd(xbc, dt_raw, dtb, alog, ef, l, st, chunk_first)
        x1, h2, probs_t = _ssd_bwd_merge(x, xbc, dt_raw, y1, proj, dtb, alog, eb, dskip, row(ssd_norm_w),
                                         w_ssd_out.astype(BF16), sc_conv_w, w_sc_out.astype(BF16), row(gate_b),
                                         w_o.astype(BF16), row(norm2_w), router_t, l, st, chunk_last, tile_flags)
        dests, posxs = [], []
        for g in range(len(xs)):
            pg = probs_t[:, st.off_g[g]:st.off_g[g] + st.t_g[g]].reshape(N_EXPERTS, st.t_g[g] // LANES, LANES)
            dest, posx = _route(pg, st.cap_g[g])
            dests.append(dest.reshape(N_EXPERTS, st.t_g[g]))
            posxs.append(posx)
        dest = jnp.concatenate(dests, axis=1)
        c0, c1 = _block_counts(posxs, st)
        xe = _dispatch(c0, c1, region_start, group_first, group_cap, dest, h2, st)
        ye = _moe_ffn(xe, wg, wu, wd, l, st)
        x = _combine(c0, c1, region_start, dest, x1, ye, st)

    fw = final_norm_w.reshape(1, D_MODEL)
    return [_final_norm(x, fw, st, g).reshape(v.shape) for g, v in enumerate(xs)]


def kernel(x_prompt, x_sample, norm1_w, w_in, ssd_conv_w, ssd_conv_b, ssd_dt_bias, ssd_a_log, ssd_d, ssd_norm_w, w_ssd_out, sc_conv_w, w_sc_out, gate_b, w_o, norm2_w, router_w, w_gate, w_up, w_down, final_norm_w):
    params = (norm1_w, w_in, ssd_conv_w, ssd_conv_b, ssd_dt_bias, ssd_a_log, ssd_d, ssd_norm_w, w_ssd_out, sc_conv_w,
              w_sc_out, gate_b, w_o, norm2_w, router_w, w_gate, w_up, w_down, final_norm_w)
    y_prompt, y_sample = _trunk_stream([x_prompt, x_sample], params, tm=1024, tq=512, tk=256, ts=512, tc=2048, cps=8)
    return (y_prompt, y_sample)
```

```python
import functools

import numpy as np
import jax
import jax.numpy as jnp
from jax import lax
from jax.experimental import pallas as pl
from jax.experimental.pallas import tpu as pltpu

F32 = jnp.float32
BF16 = jnp.bfloat16
I32 = jnp.int32

D_MODEL = 1024
HEADS = 24
HEAD_DIM = 64
INNER = HEADS * HEAD_DIM
GROUPS = 4
GROUP_W = INNER // GROUPS
STATE = 128
CHUNK = 128
SSD_CONV = 5
SC_WIDTH = 512
SC_CONV = 3
N_EXPERTS = 16
EXPERT_FF = 2048
CAPACITY_FACTOR = 2
EPS = 1e-6
LANES = 128
SUBLANES = 8

_OFF_XBC = INNER
_OFF_DT = _OFF_XBC + INNER + 2 * GROUPS * STATE
_OFF_SCB = _OFF_DT + 2 * HEADS
_OFF_GATE = _OFF_SCB + 3 * SC_WIDTH
_N_IN = _OFF_GATE + 2 * D_MODEL
COL_X = INNER
COL_B = 2 * INNER
COL_C = COL_B + GROUPS * STATE
COL_GATE = COL_C + GROUPS * STATE
COL_SCB = COL_GATE + 2 * D_MODEL
N_MAIN = COL_SCB + 3 * SC_WIDTH
CB = 512

BF16_ROWS = 2 * SUBLANES
PASS = 48
WIN = PASS + BF16_ROWS
XW = D_MODEL + LANES
GATE_STRIDE = N_EXPERTS

VMEM_LIMIT = 50 * 1024 * 1024
FUSED_VMEM_LIMIT = 58 * 1024 * 1024


def _cparams(sem, limit=VMEM_LIMIT):
    return pltpu.CompilerParams(dimension_semantics=sem, vmem_limit_bytes=limit)


def _sigmoid(v):
    return 0.5 * jnp.tanh(0.5 * v) + 0.5


def _split3(v):
    hi = v.astype(BF16)
    r = v - hi.astype(F32)
    mid = r.astype(BF16)
    lo = (r - mid.astype(F32)).astype(BF16)
    return hi, mid, lo


def _dot(a, b):
    return jnp.dot(a, b, preferred_element_type=F32)


def _dot_nt(a, b):
    return lax.dot_general(a, b, (((1,), (1,)), ((), ())), preferred_element_type=F32)


def _dot_tn(a, b):
    return lax.dot_general(a, b, (((0,), (0,)), ((), ())), preferred_element_type=F32)


def _dot3_l(a_f32, b_exact):
    hi, mid, lo = _split3(a_f32)
    return _dot(hi, b_exact) + _dot(mid, b_exact) + _dot(lo, b_exact)


def _dot3_r(a_exact, b_f32):
    hi, mid, lo = _split3(b_f32)
    return _dot(a_exact, hi) + _dot(a_exact, mid) + _dot(a_exact, lo)


class _Stream:
    def __init__(self, groups, tm, tq, tk, ts, tc, cps):
        self.groups = tuple(groups)
        self.tm, self.tq, self.tk, self.ts, self.tc, self.cps = tm, tq, tk, ts, tc, cps
        self.t_g = [n * l for n, l in self.groups]
        self.off_g = [int(v) for v in np.cumsum([0] + self.t_g[:-1])]
        self.total = int(sum(self.t_g))
        self.cap_g = [CAPACITY_FACTOR * t // N_EXPERTS for t in self.t_g]
        self.nkb_g = [t // tk for t in self.t_g]
        for (n, l), t, c in zip(self.groups, self.t_g, self.cap_g):
            assert l % (CHUNK * cps) == 0 and l % tq == 0 and l % tc == 0 and t % tk == 0 and c % ts == 0
            assert t % tm == 0 and tk % LANES == 0 and tc % CONV_ROWS == 0
        assert ts >= WIN
        self.region_g = [c + ts for c in self.cap_g]
        self.rows_per_expert = int(sum(self.region_g))
        self.tiles_per_expert = self.rows_per_expert // ts
        self.slot_rows = N_EXPERTS * self.rows_per_expert
        self.region_off_g = [int(v) for v in np.cumsum([0] + self.region_g[:-1])]

    def block_tables(self):
        rs, gfirst, cap = [], [], []
        for g, nkb in enumerate(self.nkb_g):
            for kb in range(nkb):
                gfirst.append(1 if kb == 0 else 0)
                cap.append(self.cap_g[g])
                rs.extend(e * self.rows_per_expert + self.region_off_g[g] for e in range(N_EXPERTS))
        return tuple(jnp.asarray(np.asarray(v, np.int32)) for v in (rs, gfirst, cap))

    def tile_tables(self):
        pad, src = [], []
        for e in range(N_EXPERTS):
            for g, c in enumerate(self.cap_g):
                n = c // self.ts
                base = len(pad)
                pad.extend([0] * n + [1])
                src.extend(list(range(base, base + n)) + [base + n - 1])
        return jnp.asarray(np.asarray(pad, np.int32)), jnp.asarray(np.asarray(src, np.int32))

    def seq_flags(self, tile):
        n_tiles = self.total // tile
        first = np.zeros((n_tiles,), np.int32)
        last = np.zeros((n_tiles,), np.int32)
        for (n, l), off in zip(self.groups, self.off_g):
            for s in range(n):
                first[(off + s * l) // tile] = 1
                last[(off + (s + 1) * l) // tile - 1] = 1
        return jnp.asarray(first), jnp.asarray(last)


def _part_specs(parts, tile, tile_of):
    specs, starts, start = [], [], 0
    for p in parts:
        n = p.shape[0] // tile
        specs.append(pl.BlockSpec((tile, D_MODEL), lambda *a, s=start, n=n: (jnp.clip(tile_of(*a) - s, 0, n - 1), 0)))
        starts.append(start)
        start += n
    return specs, tuple(starts)


def _read_parts(tile, refs, starts):
    val = refs[0][...]
    for ref, s in zip(refs[1:], starts[1:]):
        val = jnp.where(tile >= s, ref[...], val)
    return val


def _inproj_body(*refs, starts):
    x_refs = refs[:len(starts)]
    nw_ref, w_ref, wdt_ref, proj_ref, dt_ref, h_scr = refs[len(starts):]

    @pl.when(pl.program_id(1) == 0)
    def _():
        x = _read_parts(pl.program_id(0), x_refs, starts)
        ms = jnp.mean(x * x, axis=-1, keepdims=True)
        h = (x * lax.rsqrt(ms + EPS) * nw_ref[...]).astype(BF16)
        h_scr[...] = h
        dt_ref[...] = _dot(h, wdt_ref[...])

    proj_ref[...] = _dot(h_scr[...], w_ref[...]).astype(BF16)


def _inproj(x_parts, nw, w_main, w_dt, layer, st):
    t, tm = st.total, st.tm
    tn = N_MAIN // 2
    x_specs, starts = _part_specs(x_parts, tm, lambda i, n: i)
    return pl.pallas_call(
        functools.partial(_inproj_body, starts=starts),
        grid=(t // tm, N_MAIN // tn),
        in_specs=x_specs + [
            pl.BlockSpec((None, 1, D_MODEL), lambda i, n: (layer, 0, 0)),
            pl.BlockSpec((None, D_MODEL, tn), lambda i, n: (layer, 0, n)),
            pl.BlockSpec((None, D_MODEL, LANES), lambda i, n: (layer, 0, 0)),
        ],
        out_specs=[
            pl.BlockSpec((tm, tn), lambda i, n: (i, n)),
            pl.BlockSpec((tm, LANES), lambda i, n: (i, 0)),
        ],
        out_shape=[jax.ShapeDtypeStruct((t, N_MAIN), BF16), jax.ShapeDtypeStruct((t, LANES), F32)],
        scratch_shapes=[pltpu.VMEM((tm, D_MODEL), BF16)],
        compiler_params=_cparams(("arbitrary", "arbitrary"), FUSED_VMEM_LIMIT),
        name="inproj",
    )(*x_parts, nw, w_main, w_dt)


CONV_ROWS = 128


def _conv_body(topz, botz, main_ref, prev_ref, next_ref, w_ref, b_ref, o_ref, ext):
    i = pl.program_id(0)
    tq = main_ref.shape[0]
    halo = BF16_ROWS
    pad = SSD_CONV // 2
    zero = jnp.zeros_like(prev_ref[...])
    ext[0:halo, :] = jnp.where(topz[i] == 1, zero, prev_ref[...])
    ext[halo:halo + tq, :] = main_ref[...]
    ext[halo + tq:2 * halo + tq, :] = jnp.where(botz[i] == 1, zero, next_ref[...])
    win = CONV_ROWS + 2 * halo
    taps = [k for k in range(SSD_CONV) if k != pad]
    r = lax.broadcasted_iota(I32, (CONV_ROWS, win), 0)
    c = lax.broadcasted_iota(I32, (CONV_ROWS, win), 1)
    shift = jnp.concatenate([jnp.where(c == r + halo + k - pad, 1.0, 0.0).astype(BF16) for k in taps], axis=0)
    for sb in range(tq // CONV_ROWS):
        r0 = sb * CONV_ROWS
        moved = _dot(shift, ext[r0:r0 + win, :])
        acc = b_ref[...] + ext[halo + r0:halo + r0 + CONV_ROWS, :].astype(F32) * w_ref[pad:pad + 1, :]
        for n, k in enumerate(taps):
            acc = acc + moved[n * CONV_ROWS:(n + 1) * CONV_ROWS, :] * w_ref[k:k + 1, :]
        o_ref[r0:r0 + CONV_ROWS, :] = (acc * _sigmoid(acc)).astype(BF16)


def _conv_xbc(proj, conv_w, conv_b, layer, st, flags):
    t, tq = st.total, st.tc
    ncb = (COL_GATE - COL_X) // CB
    cb0 = COL_X // CB
    rpt = tq // BF16_ROWS
    last_rb = t // BF16_ROWS - 1
    grid_spec = pltpu.PrefetchScalarGridSpec(
        num_scalar_prefetch=2,
        grid=(t // tq, ncb),
        in_specs=[
            pl.BlockSpec((tq, CB), lambda i, j, a, b: (i, cb0 + j)),
            pl.BlockSpec((BF16_ROWS, CB), lambda i, j, a, b: (jnp.maximum(i * rpt - 1, 0), cb0 + j)),
            pl.BlockSpec((BF16_ROWS, CB), lambda i, j, a, b: (jnp.minimum((i + 1) * rpt, last_rb), cb0 + j)),
            pl.BlockSpec((None, SSD_CONV, CB), lambda i, j, a, b: (layer, 0, j)),
            pl.BlockSpec((None, 1, CB), lambda i, j, a, b: (layer, 0, j)),
        ],
        out_specs=pl.BlockSpec((tq, CB), lambda i, j, a, b: (i, j)),
        scratch_shapes=[pltpu.VMEM((tq + 2 * BF16_ROWS, CB), BF16)],
    )
    return pl.pallas_call(
        _conv_body,
        grid_spec=grid_spec,
        out_shape=jax.ShapeDtypeStruct((t, COL_GATE - COL_X), BF16),
        compiler_params=_cparams(("arbitrary", "arbitrary")),
        name="conv_xbc",
    )(flags[0], flags[1], proj, proj, proj, conv_w, conv_b)


def _ssd_scalars(dt_ref, rows, dtb_ref, alog_ref):
    v = dt_ref[rows, :] + dtb_ref[...]
    u = jnp.exp(-jnp.abs(v))
    w = 1.0 + u
    dt = jnp.maximum(v, 0.0) + jnp.where(w == 1.0, u, jnp.log(w) * (u / (w - 1.0)))
    da = dt * (-jnp.exp(alog_ref[...]))
    r = lax.broadcasted_iota(I32, (CHUNK, CHUNK), 0)
    c = lax.broadcasted_iota(I32, (CHUNK, CHUNK), 1)
    tri = jnp.where(c <= r, 1.0, 0.0).astype(BF16)
    upper = jnp.where(r <= c, 1.0, 0.0).astype(BF16)
    cs = _dot3_r(tri, da)
    da_t = da.T
    dt_t = dt.T
    cs_t = _dot3_l(da_t, upper)
    return dt, cs, cs - da, dt_t, cs_t, cs_t - da_t, r, c


def _expand(v, e_ref):
    hi = v.astype(BF16)
    mid = (v - hi.astype(F32)).astype(BF16)
    e = e_ref[...]
    return _dot(hi, e) + _dot(mid, e)


def _expand_pair(a, b, e_ref):
    out = _expand(jnp.concatenate([a, b], axis=0), e_ref)
    return out[:CHUNK, :], out[CHUNK:, :]


def _state_step(s_scr, xs_f32, w_tok, dec_lanes, b_ref, rows):
    xw = (xs_f32 * w_tok).astype(BF16)
    for g in range(GROUPS):
        sl = slice(g * GROUP_W, (g + 1) * GROUP_W)
        sc = _dot_tn(b_ref[rows, g * STATE:(g + 1) * STATE], xw[:, sl])
        s_scr[:, sl] = s_scr[:, sl] * dec_lanes[:, sl] + sc


def _ssd_fwd_body(first, xs_ref, b_ref, c_ref, dt_ref, dtb_ref, alog_ref, ef_ref, y_ref, s_scr):
    cps = xs_ref.shape[0] // CHUNK

    @pl.when(pl.program_id(0) == 0)
    def _():
        s_scr[...] = jnp.zeros_like(s_scr)

    for sub in range(cps):
        rows = slice(sub * CHUNK, (sub + 1) * CHUNK)
        keep = jnp.where(first[pl.program_id(0) * cps + sub] == 1, 0.0, 1.0)
        dt, cs, ex, dt_t, cs_t, ex_t, r, c = _ssd_scalars(dt_ref, rows, dtb_ref, alog_ref)
        w_out, w_st = _expand_pair(jnp.exp(cs) * keep, jnp.exp(cs[CHUNK - 1:CHUNK, :] - cs) * dt, ef_ref)
        for g in range(GROUPS):
            sl = slice(g * GROUP_W, (g + 1) * GROUP_W)
            yo = _dot(c_ref[rows, g * STATE:(g + 1) * STATE], s_scr[:, sl].astype(BF16))
            y_ref[rows, sl] = yo * w_out[:, sl]
        lane = lax.broadcasted_iota(I32, (CHUNK, 2 * HEAD_DIM), 1)
        lower = r >= c
        pairs_per_group = HEADS // GROUPS // 2
        for pi in range(HEADS // 2):
            g = pi // pairs_per_group
            if pi % pairs_per_group == 0:
                cb = _dot_nt(c_ref[rows, g * STATE:(g + 1) * STATE], b_ref[rows, g * STATE:(g + 1) * STATE])
            ms = []
            for h in (2 * pi, 2 * pi + 1):
                hb = HEADS + h
                arg = jnp.where(lower, cs[:, h:h + 1] - cs_t[h:h + 1, :], ex_t[hb:hb + 1, :] - ex[:, hb:hb + 1])
                d0 = dt_t[h:h + 1, :]
                d1 = dt_t[hb:hb + 1, :]
                w = jnp.where(r > c, d0, jnp.where(r < c, d1, d0 + d1))
                ms.append((jnp.exp(arg) * w * cb).astype(BF16))
            psl = slice(pi * 2 * HEAD_DIM, (pi + 1) * 2 * HEAD_DIM)
            xp = xs_ref[rows, psl]
            zero = jnp.zeros_like(xp)
            rhs = jnp.concatenate([jnp.where(lane < HEAD_DIM, xp, zero), jnp.where(lane >= HEAD_DIM, xp, zero)],
                                  axis=0)
            y_ref[rows, psl] = y_ref[rows, psl] + _dot(jnp.concatenate(ms, axis=1), rhs)
        _state_step(s_scr, xs_ref[rows, :].astype(F32), w_st, w_out[CHUNK - 1:CHUNK, :], b_ref, rows)


def _ssd_bwd_rows(last, step, xs_ref, b_ref, c_ref, dt_ref, y1_ref, z_ref, dtb_ref, alog_ref, eb_ref,
                  dskip_ref, nw_ref, s_scr, yn_scr, between=()):
    cps = xs_ref.shape[0] // CHUNK
    between = list(between)
    for sub in reversed(range(cps)):
        if between:
            between.pop(0)()
        rows = slice(sub * CHUNK, (sub + 1) * CHUNK)
        keep = jnp.where(last[step * cps + sub] == 1, 0.0, 1.0)
        dt, cs, ex, dt_t, cs_t, ex_t, r, c = _ssd_scalars(dt_ref, rows, dtb_ref, alog_ref)
        tot = cs[CHUNK - 1:CHUNK, :]
        w_out, w_st = _expand_pair(jnp.exp(tot - ex) * keep, jnp.exp(ex) * dt, eb_ref)
        xs = xs_ref[rows, :].astype(F32)
        z = z_ref[rows, :].astype(F32)
        zs = z * _sigmoid(z)
        for g in range(GROUPS):
            sl = slice(g * GROUP_W, (g + 1) * GROUP_W)
            yo = _dot(c_ref[rows, g * STATE:(g + 1) * STATE], s_scr[:, sl].astype(BF16))
            y = (y1_ref[rows, sl] + yo * w_out[:, sl] + xs[:, sl] * dskip_ref[:, sl]) * zs[:, sl]
            y = y * lax.rsqrt(jnp.mean(y * y, axis=-1, keepdims=True) + EPS)
            yn_scr[rows, sl] = (y * nw_ref[:, sl]).astype(BF16)
        _state_step(s_scr, xs, w_st, w_out[0:1, :], b_ref, rows)
    for rest in between:
        rest()


def _expansion_matrices():
    rows = np.arange(LANES)[:, None]
    cols = np.arange(INNER)[None, :] // HEAD_DIM
    ef = (rows == cols).astype(np.float32)
    eb = (rows == cols + HEADS).astype(np.float32)
    return jnp.asarray(ef, BF16), jnp.asarray(eb, BF16)


def _ssd_fwd(xbc, dt_raw, dtb, alog, ef, layer, st, first):
    t = st.total
    nb = GROUPS * STATE // CB
    rows = st.cps * CHUNK
    grid_spec = pltpu.PrefetchScalarGridSpec(
        num_scalar_prefetch=1,
        grid=(t // rows,),
        in_specs=[
            pl.BlockSpec((rows, INNER), lambda i, f: (i, 0)),
            pl.BlockSpec((rows, CB), lambda i, f: (i, INNER // CB)),
            pl.BlockSpec((rows, CB), lambda i, f: (i, INNER // CB + nb)),
            pl.BlockSpec((rows, LANES), lambda i, f: (i, 0)),
            pl.BlockSpec((None, 1, LANES), lambda i, f: (layer, 0, 0)),
            pl.BlockSpec((None, 1, LANES), lambda i, f: (layer, 0, 0)),
            pl.BlockSpec((LANES, INNER), lambda i, f: (0, 0)),
        ],
        out_specs=pl.BlockSpec((rows, INNER), lambda i, f: (i, 0)),
        scratch_shapes=[pltpu.VMEM((STATE, INNER), F32)],
    )
    return pl.pallas_call(
        _ssd_fwd_body,
        grid_spec=grid_spec,
        out_shape=jax.ShapeDtypeStruct((t, INNER), F32),
        compiler_params=_cparams(("arbitrary",)),
        name="ssd_fwd",
    )(first, xbc, xbc, xbc, dt_raw, dtb, alog, ef)


def _merge_stages(i, topz, botz, read_x, o_ssd, gate_ref, scb_ref, scc_ref, scx_ref, ccp_ref, cxp_ref, ccn_ref, cxn_ref,
                  cw_ref, wsc_ref, gb_ref, wo_ref, nw_ref, rw_ref, x1_ref, h2_ref, pt_ref, ext):
    tm = gate_ref.shape[0]
    halo = SUBLANES
    pad = SC_CONV // 2
    live = {}

    def short_conv():
        ext[0:halo, :] = jnp.where(topz[i] == 1, 0.0, ccp_ref[...].astype(F32) * cxp_ref[...].astype(F32))
        ext[halo:halo + tm, :] = scc_ref[...].astype(F32) * scx_ref[...].astype(F32)
        ext[halo + tm:2 * halo + tm, :] = jnp.where(botz[i] == 1, 0.0,
                                                    ccn_ref[...].astype(F32) * cxn_ref[...].astype(F32))
        v = ext[halo - pad:halo - pad + tm, :] * cw_ref[0:1, :]
        for k in range(1, SC_CONV):
            v = v + ext[halo - pad + k:halo - pad + k + tm, :] * cw_ref[k:k + 1, :]
        live["o_sc"] = _dot((scb_ref[...].astype(F32) * v).astype(BF16), wsc_ref[...])

    def gated_mix():
        g = _sigmoid(gate_ref[...].astype(F32) + gb_ref[...])
        mix = g[:, :D_MODEL] * o_ssd.astype(F32) + g[:, D_MODEL:] * live["o_sc"]
        x1 = read_x() + _dot(mix.astype(BF16), wo_ref[...])
        x1_ref[...] = x1
        live["x1"] = x1

    def norm():
        x1 = live["x1"]
        h2 = x1 * lax.rsqrt(jnp.mean(x1 * x1, axis=-1, keepdims=True) + EPS) * nw_ref[...]
        h2_ref[:, :D_MODEL] = h2.astype(BF16)
        live["h2"] = h2

    def router():
        hh, hm, _ = _split3(live["h2"])
        rh, rm, _ = _split3(rw_ref[...])
        logits = _dot_nt(rh, hh) + _dot_nt(rh, hm) + _dot_nt(rm, hh)
        mx = jnp.max(logits, axis=0, keepdims=True)
        ex = jnp.exp(logits - mx)
        probs = ex / jnp.sum(ex, axis=0, keepdims=True)
        pt_ref[...] = probs
        pk = jnp.concatenate([probs, jnp.zeros((LANES - N_EXPERTS, tm), F32)], axis=0).T
        hi = pk.astype(BF16).astype(F32)
        mid = (pk - hi).astype(BF16).astype(F32)
        lo = pk - hi - mid
        lane = lax.broadcasted_iota(I32, (tm, LANES), 1)
        terms = jnp.where(lane < GATE_STRIDE, hi,
                          jnp.where(lane < 2 * GATE_STRIDE, pltpu.roll(mid, GATE_STRIDE, 1),
                                    jnp.where(lane < 3 * GATE_STRIDE, pltpu.roll(lo, 2 * GATE_STRIDE, 1), 0.0)))
        h2_ref[:, D_MODEL:] = terms.astype(BF16)

    return [short_conv, gated_mix, norm, router]


def _bwd_merge_body(last, topz, botz, *refs, x_starts):
    (xs_ref, b_ref, c_ref, dt_ref, y1_ref, z_ref, dtb_ref, alog_ref, eb_ref, dskip_ref, snw_ref, wout_ref) = refs[:12]
    x_refs = refs[12:12 + len(x_starts)]
    (gate_ref, scb_ref, scc_ref, scx_ref, ccp_ref, cxp_ref, ccn_ref, cxn_ref, cw_ref, wsc_ref, gb_ref, wo_ref, nw_ref,
     rw_ref, x1_ref, h2_ref, pt_ref, s_scr, yn_scr, ossd_scr, ext) = refs[12 + len(x_starts):]
    i = pl.program_id(0)
    n_blocks = pl.num_programs(0) - 1

    @pl.when(i == 0)
    def _():
        s_scr[...] = jnp.zeros_like(s_scr)
        ossd_scr[...] = jnp.zeros_like(ossd_scr)

    slot = i % 2
    merge_tile = n_blocks - 1 - jnp.maximum(i - 1, 0)
    read_x = functools.partial(_read_parts, merge_tile, x_refs, x_starts)
    stages = _merge_stages(merge_tile, topz, botz, read_x, ossd_scr[1 - slot], gate_ref, scb_ref, scc_ref, scx_ref,
                           ccp_ref, cxp_ref, ccn_ref, cxn_ref, cw_ref, wsc_ref, gb_ref, wo_ref, nw_ref, rw_ref,
                           x1_ref, h2_ref, pt_ref, ext)
    ssd_tile = n_blocks - 1 - jnp.minimum(i, n_blocks - 1)
    _ssd_bwd_rows(last, ssd_tile, xs_ref, b_ref, c_ref, dt_ref, y1_ref, z_ref, dtb_ref, alog_ref, eb_ref,
                  dskip_ref, snw_ref, s_scr, yn_scr, between=stages)
    ossd_scr[slot] = _dot(yn_scr[...], wout_ref[...]).astype(BF16)


def _ssd_bwd_merge(x_parts, xbc, dt_raw, y1, proj, dtb, alog, eb, dskip, ssd_norm, w_ssd_out, sc_w, w_sc, gate_b, w_o, norm2,
                   router_t, layer, st, last, flags):
    t, tm = st.total, st.tq
    assert tm % CHUNK == 0
    nc = t // tm
    nb = GROUPS * STATE // CB
    rpt = tm // SUBLANES
    last_rb = t // SUBLANES - 1
    cscb, cscc, cscx = COL_SCB // CB, COL_SCB // CB + 1, COL_SCB // CB + 2
    sb = lambda i: nc - 1 - jnp.minimum(i, nc - 1)
    mb = lambda i: nc - 1 - jnp.maximum(i - 1, 0)
    prev = lambda i: jnp.maximum(mb(i) * rpt - 1, 0)
    nxt = lambda i: jnp.minimum((mb(i) + 1) * rpt, last_rb)
    const = lambda *idx: (lambda i, *_: idx)
    x_specs, x_starts = _part_specs(x_parts, tm, lambda i, *_: mb(i))
    grid_spec = pltpu.PrefetchScalarGridSpec(
        num_scalar_prefetch=3,
        grid=(nc + 1,),
        in_specs=[
            pl.BlockSpec((tm, INNER), lambda i, *_: (sb(i), 0)),
            pl.BlockSpec((tm, CB), lambda i, *_: (sb(i), INNER // CB)),
            pl.BlockSpec((tm, CB), lambda i, *_: (sb(i), INNER // CB + nb)),
            pl.BlockSpec((tm, LANES), lambda i, *_: (sb(i), 0)),
            pl.BlockSpec((tm, INNER), lambda i, *_: (sb(i), 0)),
            pl.BlockSpec((tm, INNER), lambda i, *_: (sb(i), 0)),
            pl.BlockSpec((None, 1, LANES), const(layer, 0, 0)),
            pl.BlockSpec((None, 1, LANES), const(layer, 0, 0)),
            pl.BlockSpec((LANES, INNER), const(0, 0)),
            pl.BlockSpec((None, 1, INNER), const(layer, 0, 0)),
            pl.BlockSpec((None, 1, INNER), const(layer, 0, 0)),
            pl.BlockSpec((None, INNER, D_MODEL), const(layer, 0, 0)),
        ] + x_specs + [
            pl.BlockSpec((tm, 2 * D_MODEL), lambda i, *_: (mb(i), COL_GATE // (2 * D_MODEL))),
            pl.BlockSpec((tm, CB), lambda i, *_: (mb(i), cscb)),
            pl.BlockSpec((tm, CB), lambda i, *_: (mb(i), cscc)),
            pl.BlockSpec((tm, CB), lambda i, *_: (mb(i), cscx)),
            pl.BlockSpec((SUBLANES, CB), lambda i, *_: (prev(i), cscc)),
            pl.BlockSpec((SUBLANES, CB), lambda i, *_: (prev(i), cscx)),
            pl.BlockSpec((SUBLANES, CB), lambda i, *_: (nxt(i), cscc)),
            pl.BlockSpec((SUBLANES, CB), lambda i, *_: (nxt(i), cscx)),
            pl.BlockSpec((None, SC_CONV, SC_WIDTH), const(layer, 0, 0)),
            pl.BlockSpec((None, SC_WIDTH, D_MODEL), const(layer, 0, 0)),
            pl.BlockSpec((None, 1, 2 * D_MODEL), const(layer, 0, 0)),
            pl.BlockSpec((None, D_MODEL, D_MODEL), const(layer, 0, 0)),
            pl.BlockSpec((None, 1, D_MODEL), const(layer, 0, 0)),
            pl.BlockSpec((None, N_EXPERTS, D_MODEL), const(layer, 0, 0)),
        ],
        out_specs=[
            pl.BlockSpec((tm, D_MODEL), lambda i, *_: (mb(i), 0)),
            pl.BlockSpec((tm, XW), lambda i, *_: (mb(i), 0)),
            pl.BlockSpec((N_EXPERTS, tm), lambda i, *_: (0, mb(i))),
        ],
        scratch_shapes=[
            pltpu.VMEM((STATE, INNER), F32),
            pltpu.VMEM((tm, INNER), BF16),
            pltpu.VMEM((2, tm, D_MODEL), BF16),
            pltpu.VMEM((tm + 2 * SUBLANES, SC_WIDTH), F32),
        ],
    )
    return pl.pallas_call(
        functools.partial(_bwd_merge_body, x_starts=x_starts),
        grid_spec=grid_spec,
        out_shape=[jax.ShapeDtypeStruct((t, D_MODEL), F32), jax.ShapeDtypeStruct((t, XW), BF16),
                   jax.ShapeDtypeStruct((N_EXPERTS, t), F32)],
        compiler_params=pltpu.CompilerParams(dimension_semantics=("arbitrary",), vmem_limit_bytes=FUSED_VMEM_LIMIT),
        name="ssd_bwd_merge",
    )(last, flags[0], flags[1], xbc, xbc, xbc, dt_raw, y1, proj, dtb, alog, eb, dskip, ssd_norm, w_ssd_out,
      *x_parts, proj, proj, proj, proj, proj, proj, proj, proj, sc_w, w_sc, gate_b, w_o, norm2, router_t)


def _route_body(p_ref, dest_ref, posx_ref, *, cap):
    rows = p_ref.shape[1]
    r = lax.broadcasted_iota(I32, (LANES, LANES), 0)
    c = lax.broadcasted_iota(I32, (LANES, LANES), 1)
    upper = jnp.where(r <= c, 1.0, 0.0).astype(BF16)
    last_col = jnp.where(r == LANES - 1, 1.0, 0.0).astype(BF16)
    rr = lax.broadcasted_iota(I32, (rows, rows), 0)
    rc = lax.broadcasted_iota(I32, (rows, rows), 1)
    below = jnp.where(rc < rr, 1.0, 0.0).astype(BF16)

    def count(mask):
        s = jnp.sum(jnp.where(mask, 1.0, 0.0), axis=0, keepdims=True)
        return jnp.sum(s, axis=1, keepdims=True)

    def excl_prefix(mask):
        x = jnp.where(mask, 1.0, 0.0)
        incl = _dot(x.astype(BF16), upper)
        row_tot = _dot(incl.astype(BF16), last_col)
        row_off = _dot(below, row_tot.astype(BF16))
        return incl + row_off - x

    def search(i, prefix):
        bit = jnp.left_shift(jnp.int32(1), 30 - i)
        out = []
        for e in range(N_EXPERTS):
            keys = pltpu.bitcast(p_ref[e], I32)
            cand = prefix[e] | bit
            out.append(jnp.where(count(keys >= cand) >= cap, cand, prefix[e]))
        return tuple(out)

    thr = lax.fori_loop(0, 31, search, tuple(jnp.zeros((1, 1), I32) for _ in range(N_EXPERTS)))
    for e in range(N_EXPERTS):
        keys = pltpu.bitcast(p_ref[e], I32)
        gt = keys > thr[e]
        eq = keys == thr[e]
        need = cap - count(gt)
        sel = gt | (eq & (excl_prefix(eq) < need))
        pos = excl_prefix(sel).astype(I32)
        posx_ref[e] = pos
        dest_ref[e] = jnp.where(sel, pos, -1)


def _route(probs3, cap):
    e, rows, _ = probs3.shape
    return pl.pallas_call(
        functools.partial(_route_body, cap=cap),
        out_shape=[jax.ShapeDtypeStruct((e, rows, LANES), I32), jax.ShapeDtypeStruct((e, rows, LANES), I32)],
        compiler_params=pltpu.CompilerParams(vmem_limit_bytes=VMEM_LIMIT),
        name="route",
    )(probs3)


def _block_counts(posx_list, st):
    rb = st.tk // LANES
    c0s, c1s = [], []
    for g, posx in enumerate(posx_list):
        c0 = posx[:, ::rb, 0]
        c1 = jnp.concatenate([c0[:, 1:], jnp.full((N_EXPERTS, 1), st.cap_g[g], I32)], axis=1)
        c0s.append(c0.T.reshape(-1))
        c1s.append(c1.T.reshape(-1))
    return jnp.concatenate(c0s), jnp.concatenate(c1s)


def _pass_window(c0_s, c1_s, rs_s, kb, e, k):
    c0 = c0_s[kb * N_EXPERTS + e]
    c1 = c1_s[kb * N_EXPERTS + e]
    lo = jnp.minimum(c0 + k * PASS, c1)
    base = (lo // BF16_ROWS) * BF16_ROWS
    row = pl.multiple_of(rs_s[kb * N_EXPERTS + e] + base, BF16_ROWS)
    return lo, base, row, jnp.minimum(lo + PASS, c1)


def _num_passes(c0_s, c1_s, kb):
    most = c1_s[kb * N_EXPERTS] - c0_s[kb * N_EXPERTS]
    for e in range(1, N_EXPERTS):
        most = jnp.maximum(most, c1_s[kb * N_EXPERTS + e] - c0_s[kb * N_EXPERTS + e])
    return (most + PASS - 1) // PASS


def _window_one_hot(c0_s, c1_s, rs_s, dest_ref, kb, k):
    tk = dest_ref.shape[1]
    slot = lax.broadcasted_iota(I32, (WIN, tk), 0)
    parts = []
    for e in range(N_EXPERTS):
        lo, base, _, _ = _pass_window(c0_s, c1_s, rs_s, kb, e, k)
        d = dest_ref[e:e + 1, :]
        rel = jnp.where(d >= lo, jnp.where(d < lo + PASS, d - base, -1), -1)
        parts.append(jnp.where(rel == slot, 1.0, 0.0).astype(BF16))
    return jnp.concatenate(parts, axis=0)


def _dispatch_body(c0_s, c1_s, rs_s, gfirst_s, cap_s, dest_ref, h_ref, xe_hbm, stage, carry, zeros, sem, cnt):
    kb = pl.program_id(0)

    def window_copy(slot, e, row):
        return pltpu.make_async_copy(stage.at[slot, pl.ds(e * WIN, WIN), :], xe_hbm.at[pl.ds(row, WIN), :], sem.at[0])

    def wait_all():
        for e in range(N_EXPERTS):
            window_copy(0, e, 0).wait()

    def pad_copy(e):
        row = pl.multiple_of(rs_s[kb * N_EXPERTS + e] + cap_s[kb], BF16_ROWS)
        return pltpu.make_async_copy(zeros, xe_hbm.at[pl.ds(row, zeros.shape[0]), :], sem.at[1])

    @pl.when(kb == 0)
    def _():
        cnt[0] = 0

    @pl.when(gfirst_s[kb] == 1)
    def _():
        carry[...] = jnp.zeros_like(carry)
        zeros[...] = jnp.zeros_like(zeros)
        for e in range(N_EXPERTS):
            pad_copy(e).start()
        for e in range(N_EXPERTS):
            pad_copy(e).wait()

    def one_pass(k, _):
        slot = cnt[0] % 2
        packed = _dot(_window_one_hot(c0_s, c1_s, rs_s, dest_ref, kb, k), h_ref[...])
        rows = []
        for e in range(N_EXPERTS):
            lo, base, row, nxt = _pass_window(c0_s, c1_s, rs_s, kb, e, k)
            head = packed[e * WIN:e * WIN + BF16_ROWS, :] + carry[e].astype(F32)
            stage[slot, e * WIN:e * WIN + BF16_ROWS, :] = head.astype(BF16)
            stage[slot, e * WIN + BF16_ROWS:(e + 1) * WIN, :] = packed[e * WIN + BF16_ROWS:(e + 1) * WIN, :].astype(BF16)
            off = pl.multiple_of(e * WIN + (nxt // BF16_ROWS) * BF16_ROWS - base, BF16_ROWS)
            carry[e] = stage[slot, pl.ds(off, BF16_ROWS), :]
            rows.append(row)

        @pl.when(cnt[0] > 0)
        def _():
            wait_all()

        for e in range(N_EXPERTS):
            window_copy(slot, e, rows[e]).start()
        cnt[0] = cnt[0] + 1
        return 0

    lax.fori_loop(0, _num_passes(c0_s, c1_s, kb), one_pass, 0)

    @pl.when((kb == pl.num_programs(0) - 1) & (cnt[0] > 0))
    def _():
        wait_all()


def _dispatch(c0, c1, rs, gfirst, cap, dest, h2a, st):
    tk = st.tk
    grid_spec = pltpu.PrefetchScalarGridSpec(
        num_scalar_prefetch=5,
        grid=(st.total // tk,),
        in_specs=[
            pl.BlockSpec((N_EXPERTS, tk), lambda i, *_: (0, i)),
            pl.BlockSpec((tk, XW), lambda i, *_: (i, 0)),
        ],
        out_specs=pl.BlockSpec(memory_space=pl.ANY),
        scratch_shapes=[
            pltpu.VMEM((2, N_EXPERTS * WIN, XW), BF16),
            pltpu.VMEM((N_EXPERTS, BF16_ROWS, XW), BF16),
            pltpu.VMEM((st.ts, XW), BF16),
            pltpu.SemaphoreType.DMA((2,)),
            pltpu.SMEM((1,), I32),
        ],
    )
    return pl.pallas_call(
        _dispatch_body,
        grid_spec=grid_spec,
        out_shape=jax.ShapeDtypeStruct((st.slot_rows, XW), BF16),
        compiler_params=_cparams(("arbitrary",)),
        name="moe_dispatch",
    )(c0, c1, rs, gfirst, cap, dest, h2a)


def _moe_ffn_body(pad_s, src_s, xe_ref, wg_ref, wu_ref, wd_ref, o_ref, *, tiles_per_expert):
    i = pl.program_id(0)

    @pl.when(pad_s[i] == 1)
    def _():
        o_ref[...] = jnp.zeros_like(o_ref)

    @pl.when(pad_s[i] == 0)
    def _():
        e = i // tiles_per_expert
        xe = xe_ref[:, :D_MODEL]
        terms = xe_ref[:, D_MODEL:].astype(F32)
        lane = lax.broadcasted_iota(I32, terms.shape, 1)
        mine = (lane < 3 * GATE_STRIDE) & (lane % GATE_STRIDE == e)
        gate = jnp.sum(jnp.where(mine, terms, 0.0), axis=1, keepdims=True)
        a = _dot(xe, wg_ref[...])
        u = _dot(xe, wu_ref[...])
        act = (a * _sigmoid(a) * u).astype(BF16)
        o_ref[...] = (_dot(act, wd_ref[...]) * gate).astype(BF16)


def _moe_ffn(xe, wg, wu, wd, layer, st):
    ts = st.ts
    tpe = st.tiles_per_expert
    pad, src = st.tile_tables()
    grid_spec = pltpu.PrefetchScalarGridSpec(
        num_scalar_prefetch=2,
        grid=(N_EXPERTS * tpe,),
        in_specs=[
            pl.BlockSpec((ts, XW), lambda i, p, s: (s[i], 0)),
            pl.BlockSpec((None, None, D_MODEL, EXPERT_FF), lambda i, p, s: (layer, i // tpe, 0, 0)),
            pl.BlockSpec((None, None, D_MODEL, EXPERT_FF), lambda i, p, s: (layer, i // tpe, 0, 0)),
            pl.BlockSpec((None, None, EXPERT_FF, D_MODEL), lambda i, p, s: (layer, i // tpe, 0, 0)),
        ],
        out_specs=pl.BlockSpec((ts, D_MODEL), lambda i, p, s: (i, 0)),
    )
    return pl.pallas_call(
        functools.partial(_moe_ffn_body, tiles_per_expert=tpe),
        grid_spec=grid_spec,
        out_shape=jax.ShapeDtypeStruct((st.slot_rows, D_MODEL), BF16),
        compiler_params=_cparams(("arbitrary",)),
        name="moe_ffn",
    )(pad, src, xe, wg, wu, wd)


OVERFLOW_SLOT = 2


def _combine_body(c0_s, c1_s, rs_s, dest_ref, x_ref, ye_hbm, o_ref, buf, sem):
    kb = pl.program_id(0)
    nkb = pl.num_programs(0)

    def window_copy(slot, e, row):
        return pltpu.make_async_copy(ye_hbm.at[pl.ds(row, WIN), :], buf.at[slot, pl.ds(e * WIN, WIN), :], sem.at[slot])

    def fetch(block, k, slot):
        for e in range(N_EXPERTS):
            window_copy(slot, e, _pass_window(c0_s, c1_s, rs_s, block, e, k)[2]).start()

    def wait_all(slot):
        for e in range(N_EXPERTS):
            window_copy(slot, e, 0).wait()

    @pl.when(kb == 0)
    def _():
        fetch(0, 0, 0)

    slot = kb % 2
    wait_all(slot)

    @pl.when(kb + 1 < nkb)
    def _():
        fetch(kb + 1, 0, 1 - slot)

    o_ref[...] = x_ref[...] + _dot_tn(_window_one_hot(c0_s, c1_s, rs_s, dest_ref, kb, 0), buf[slot])

    def extra_pass(k, _):
        fetch(kb, k, OVERFLOW_SLOT)
        wait_all(OVERFLOW_SLOT)
        o_ref[...] += _dot_tn(_window_one_hot(c0_s, c1_s, rs_s, dest_ref, kb, k), buf[OVERFLOW_SLOT])
        return 0

    lax.fori_loop(1, _num_passes(c0_s, c1_s, kb), extra_pass, 0)


def _combine(c0, c1, rs, dest, x1, ye, st):
    tk = st.tk
    grid_spec = pltpu.PrefetchScalarGridSpec(
        num_scalar_prefetch=3,
        grid=(st.total // tk,),
        in_specs=[
            pl.BlockSpec((N_EXPERTS, tk), lambda i, *_: (0, i)),
            pl.BlockSpec((tk, D_MODEL), lambda i, *_: (i, 0)),
            pl.BlockSpec(memory_space=pl.ANY),
        ],
        out_specs=pl.BlockSpec((tk, D_MODEL), lambda i, *_: (i, 0)),
        scratch_shapes=[
            pltpu.VMEM((3, N_EXPERTS * WIN, D_MODEL), BF16),
            pltpu.SemaphoreType.DMA((3,)),
        ],
    )
    return pl.pallas_call(
        _combine_body,
        grid_spec=grid_spec,
        out_shape=jax.ShapeDtypeStruct((st.total, D_MODEL), F32),
        compiler_params=_cparams(("arbitrary",)),
        name="moe_combine",
    )(c0, c1, rs, dest, x1, ye)


def _final_norm_body(x_ref, w_ref, o_ref):
    x = x_ref[...]
    o_ref[...] = x * lax.rsqrt(jnp.mean(x * x, axis=-1, keepdims=True) + EPS) * w_ref[...]


def _final_norm(x, w, st, group):
    t, tm = st.t_g[group], st.tm
    first = st.off_g[group] // tm
    return pl.pallas_call(
        _final_norm_body,
        grid=(t // tm,),
        in_specs=[pl.BlockSpec((tm, D_MODEL), lambda i: (first + i, 0)), pl.BlockSpec((1, D_MODEL), lambda i: (0, 0))],
        out_specs=pl.BlockSpec((tm, D_MODEL), lambda i: (i, 0)),
        out_shape=jax.ShapeDtypeStruct((t, D_MODEL), F32),
        compiler_params=_cparams(("arbitrary",)),
        name="final_norm",
    )(x, w)


def _pad_lanes(v, width=LANES):
    return jnp.pad(v, [(0, 0)] * (v.ndim - 1) + [(0, width - v.shape[-1])])


def _trunk_stream(xs, params, tm, tq, tk, ts, tc, cps):
    (norm1_w, w_in, ssd_conv_w, ssd_conv_b, ssd_dt_bias, ssd_a_log, ssd_d, ssd_norm_w, w_ssd_out, sc_conv_w,
     w_sc_out, gate_b, w_o, norm2_w, router_w, w_gate, w_up, w_down, final_norm_w) = params
    depth = w_in.shape[0]
    st = _Stream([(x.shape[0], x.shape[1]) for x in xs], tm, tq, tk, ts, tc, cps)
    x_parts = [v.reshape(-1, D_MODEL) for v in xs]

    w_main = jnp.concatenate([w_in[:, :, :_OFF_DT], w_in[:, :, _OFF_GATE:], w_in[:, :, _OFF_SCB:_OFF_GATE]],
                             axis=2).astype(BF16)
    w_dt = _pad_lanes(w_in[:, :, _OFF_DT:_OFF_SCB]).astype(BF16)
    dtb = _pad_lanes(ssd_dt_bias.reshape(depth, 1, 2 * HEADS))
    alog = _pad_lanes(ssd_a_log.reshape(depth, 1, 2 * HEADS))
    dskip = jnp.repeat(ssd_d, HEAD_DIM, axis=-1).reshape(depth, 1, INNER)
    ef, eb = _expansion_matrices()
    router_t = jnp.swapaxes(router_w, 1, 2)
    wg, wu, wd = w_gate.astype(BF16), w_up.astype(BF16), w_down.astype(BF16)
    row = lambda v: v.reshape(depth, 1, v.shape[-1])

    chunk_first, chunk_last = st.seq_flags(CHUNK)
    tile_flags = st.seq_flags(tq)
    conv_flags = st.seq_flags(tc)
    region_start, group_first, group_cap = st.block_tables()

    for l in range(depth):
        proj, dt_raw = _inproj(x_parts, row(norm1_w), w_main, w_dt, l, st)
        xbc = _conv_xbc(proj, ssd_conv_w, row(ssd_conv_b), l, st, conv_flags)
        y1 = _ssd_fwd(xbc, dt_raw, dtb, alog, ef, l, st, chunk_first)
        x1, h2, probs_t = _ssd_bwd_merge(x_parts, xbc, dt_raw, y1, proj, dtb, alog, eb, dskip, row(ssd_norm_w),
                                         w_ssd_out.astype(BF16), sc_conv_w, w_sc_out.astype(BF16), row(gate_b),
                                         w_o.astype(BF16), row(norm2_w), router_t, l, st, chunk_last, tile_flags)
        dests, posxs = [], []
        for g in range(len(xs)):
            pg = probs_t[:, st.off_g[g]:st.off_g[g] + st.t_g[g]].reshape(N_EXPERTS, st.t_g[g] // LANES, LANES)
            dest, posx = _route(pg, st.cap_g[g])
            dests.append(dest.reshape(N_EXPERTS, st.t_g[g]))
            posxs.append(posx)
        dest = jnp.concatenate(dests, axis=1)
        c0, c1 = _block_counts(posxs, st)
        xe = _dispatch(c0, c1, region_start, group_first, group_cap, dest, h2, st)
        ye = _moe_ffn(xe, wg, wu, wd, l, st)
        x_parts = [_combine(c0, c1, region_start, dest, x1, ye, st)]

    fw = final_norm_w.reshape(1, D_MODEL)
    return [_final_norm(x_parts[0], fw, st, g).reshape(v.shape) for g, v in enumerate(xs)]


def kernel(x_prompt, x_sample, norm1_w, w_in, ssd_conv_w, ssd_conv_b, ssd_dt_bias, ssd_a_log, ssd_d, ssd_norm_w, w_ssd_out, sc_conv_w, w_sc_out, gate_b, w_o, norm2_w, router_w, w_gate, w_up, w_down, final_norm_w):
    params = (norm1_w, w_in, ssd_conv_w, ssd_conv_b, ssd_dt_bias, ssd_a_log, ssd_d, ssd_norm_w, w_ssd_out, sc_conv_w,
              w_sc_out, gate_b, w_o, norm2_w, router_w, w_gate, w_up, w_down, final_norm_w)
    y_prompt, y_sample = _trunk_stream([x_prompt, x_sample], params, tm=1024, tq=512, tk=256, ts=512, tc=2048, cps=8)
    return (y_prompt, y_sample)
```

```python
import functools

import numpy as np
import jax
import jax.numpy as jnp
from jax import lax
from jax.experimental import pallas as pl
from jax.experimental.pallas import tpu as pltpu

F32 = jnp.float32
BF16 = jnp.bfloat16
I32 = jnp.int32

D_MODEL = 1024
HEADS = 24
HEAD_DIM = 64
INNER = HEADS * HEAD_DIM
GROUPS = 4
GROUP_W = INNER // GROUPS
STATE = 128
CHUNK = 128
SSD_CONV = 5
SC_WIDTH = 512
SC_CONV = 3
N_EXPERTS = 16
EXPERT_FF = 2048
CAPACITY_FACTOR = 2
EPS = 1e-6
LANES = 128
SUBLANES = 8

_OFF_XBC = INNER
_OFF_DT = _OFF_XBC + INNER + 2 * GROUPS * STATE
_OFF_SCB = _OFF_DT + 2 * HEADS
_OFF_GATE = _OFF_SCB + 3 * SC_WIDTH
_N_IN = _OFF_GATE + 2 * D_MODEL
COL_X = INNER
COL_B = 2 * INNER
COL_C = COL_B + GROUPS * STATE
COL_GATE = COL_C + GROUPS * STATE
COL_SCB = COL_GATE + 2 * D_MODEL
N_MAIN = COL_SCB + 3 * SC_WIDTH
CB = 512

BF16_ROWS = 2 * SUBLANES
PASS = 48
WIN = PASS + BF16_ROWS
XW = D_MODEL + LANES
GATE_STRIDE = N_EXPERTS

VMEM_LIMIT = 50 * 1024 * 1024
FUSED_VMEM_LIMIT = 58 * 1024 * 1024


def _cparams(sem, limit=VMEM_LIMIT):
    return pltpu.CompilerParams(dimension_semantics=sem, vmem_limit_bytes=limit)


def _sigmoid(v):
    return 0.5 * jnp.tanh(0.5 * v) + 0.5


def _split3(v):
    hi = v.astype(BF16)
    r = v - hi.astype(F32)
    mid = r.astype(BF16)
    lo = (r - mid.astype(F32)).astype(BF16)
    return hi, mid, lo


def _dot(a, b):
    return jnp.dot(a, b, preferred_element_type=F32)


def _dot_nt(a, b):
    return lax.dot_general(a, b, (((1,), (1,)), ((), ())), preferred_element_type=F32)


def _dot_tn(a, b):
    return lax.dot_general(a, b, (((0,), (0,)), ((), ())), preferred_element_type=F32)


def _dot3_l(a_f32, b_exact):
    hi, mid, lo = _split3(a_f32)
    return _dot(hi, b_exact) + _dot(mid, b_exact) + _dot(lo, b_exact)


def _dot3_r(a_exact, b_f32):
    hi, mid, lo = _split3(b_f32)
    return _dot(a_exact, hi) + _dot(a_exact, mid) + _dot(a_exact, lo)


class _Stream:
    def __init__(self, groups, tm, tq, tk, ts, tc, cps):
        self.groups = tuple(groups)
        self.tm, self.tq, self.tk, self.ts, self.tc, self.cps = tm, tq, tk, ts, tc, cps
        self.t_g = [n * l for n, l in self.groups]
        self.off_g = [int(v) for v in np.cumsum([0] + self.t_g[:-1])]
        self.total = int(sum(self.t_g))
        self.cap_g = [CAPACITY_FACTOR * t // N_EXPERTS for t in self.t_g]
        self.nkb_g = [t // tk for t in self.t_g]
        for (n, l), t, c in zip(self.groups, self.t_g, self.cap_g):
            assert l % (CHUNK * cps) == 0 and l % tq == 0 and l % tc == 0 and t % tk == 0 and c % ts == 0
            assert t % tm == 0 and tk % LANES == 0 and tc % CONV_ROWS == 0
        assert ts >= WIN
        self.region_g = [c + ts for c in self.cap_g]
        self.rows_per_expert = int(sum(self.region_g))
        self.tiles_per_expert = self.rows_per_expert // ts
        self.slot_rows = N_EXPERTS * self.rows_per_expert
        self.region_off_g = [int(v) for v in np.cumsum([0] + self.region_g[:-1])]

    def block_tables(self):
        rs, gfirst, cap = [], [], []
        for g, nkb in enumerate(self.nkb_g):
            for kb in range(nkb):
                gfirst.append(1 if kb == 0 else 0)
                cap.append(self.cap_g[g])
                rs.extend(e * self.rows_per_expert + self.region_off_g[g] for e in range(N_EXPERTS))
        return tuple(jnp.asarray(np.asarray(v, np.int32)) for v in (rs, gfirst, cap))

    def tile_tables(self):
        pad, src = [], []
        for e in range(N_EXPERTS):
            for g, c in enumerate(self.cap_g):
                n = c // self.ts
                base = len(pad)
                pad.extend([0] * n + [1])
                src.extend(list(range(base, base + n)) + [base + n - 1])
        return jnp.asarray(np.asarray(pad, np.int32)), jnp.asarray(np.asarray(src, np.int32))

    def seq_flags(self, tile):
        n_tiles = self.total // tile
        first = np.zeros((n_tiles,), np.int32)
        last = np.zeros((n_tiles,), np.int32)
        for (n, l), off in zip(self.groups, self.off_g):
            for s in range(n):
                first[(off + s * l) // tile] = 1
                last[(off + (s + 1) * l) // tile - 1] = 1
        return jnp.asarray(first), jnp.asarray(last)


def _part_specs(parts, tile, tile_of):
    specs, starts, start = [], [], 0
    for p in parts:
        n = p.shape[0] // tile
        specs.append(pl.BlockSpec((tile, D_MODEL), lambda *a, s=start, n=n: (jnp.clip(tile_of(*a) - s, 0, n - 1), 0)))
        starts.append(start)
        start += n
    return specs, tuple(starts)


def _read_parts(tile, refs, starts):
    val = refs[0][...]
    for ref, s in zip(refs[1:], starts[1:]):
        val = jnp.where(tile >= s, ref[...], val)
    return val


def _inproj_body(*refs, starts):
    x_refs = refs[:len(starts)]
    nw_ref, w_ref, wdt_ref, proj_ref, dt_ref, h_scr = refs[len(starts):]

    @pl.when(pl.program_id(1) == 0)
    def _():
        x = _read_parts(pl.program_id(0), x_refs, starts)
        ms = jnp.mean(x * x, axis=-1, keepdims=True)
        h = (x * lax.rsqrt(ms + EPS) * nw_ref[...]).astype(BF16)
        h_scr[...] = h
        dt_ref[...] = _dot(h, wdt_ref[...])

    proj_ref[...] = _dot(h_scr[...], w_ref[...]).astype(BF16)


def _inproj(x_parts, nw, w_main, w_dt, layer, st):
    t, tm = st.total, st.tm
    tn = N_MAIN // 2
    x_specs, starts = _part_specs(x_parts, tm, lambda i, n: i)
    return pl.pallas_call(
        functools.partial(_inproj_body, starts=starts),
        grid=(t // tm, N_MAIN // tn),
        in_specs=x_specs + [
            pl.BlockSpec((None, 1, D_MODEL), lambda i, n: (layer, 0, 0)),
            pl.BlockSpec((None, D_MODEL, tn), lambda i, n: (layer, 0, n)),
            pl.BlockSpec((None, D_MODEL, LANES), lambda i, n: (layer, 0, 0)),
        ],
        out_specs=[
            pl.BlockSpec((tm, tn), lambda i, n: (i, n)),
            pl.BlockSpec((tm, LANES), lambda i, n: (i, 0)),
        ],
        out_shape=[jax.ShapeDtypeStruct((t, N_MAIN), BF16), jax.ShapeDtypeStruct((t, LANES), F32)],
        scratch_shapes=[pltpu.VMEM((tm, D_MODEL), BF16)],
        compiler_params=_cparams(("arbitrary", "arbitrary"), FUSED_VMEM_LIMIT),
        name="inproj",
    )(*x_parts, nw, w_main, w_dt)


CONV_ROWS = 128


def _conv_body(topz, botz, main_ref, prev_ref, next_ref, w_ref, b_ref, o_ref, ext):
    i = pl.program_id(0)
    tq = main_ref.shape[0]
    halo = BF16_ROWS
    pad = SSD_CONV // 2
    zero = jnp.zeros_like(prev_ref[...])
    ext[0:halo, :] = jnp.where(topz[i] == 1, zero, prev_ref[...])
    ext[halo:halo + tq, :] = main_ref[...]
    ext[halo + tq:2 * halo + tq, :] = jnp.where(botz[i] == 1, zero, next_ref[...])
    win = CONV_ROWS + 2 * halo
    taps = [k for k in range(SSD_CONV) if k != pad]
    r = lax.broadcasted_iota(I32, (CONV_ROWS, win), 0)
    c = lax.broadcasted_iota(I32, (CONV_ROWS, win), 1)
    shift = jnp.concatenate([jnp.where(c == r + halo + k - pad, 1.0, 0.0).astype(BF16) for k in taps], axis=0)
    for sb in range(tq // CONV_ROWS):
        r0 = sb * CONV_ROWS
        moved = _dot(shift, ext[r0:r0 + win, :])
        acc = b_ref[...] + ext[halo + r0:halo + r0 + CONV_ROWS, :].astype(F32) * w_ref[pad:pad + 1, :]
        for n, k in enumerate(taps):
            acc = acc + moved[n * CONV_ROWS:(n + 1) * CONV_ROWS, :] * w_ref[k:k + 1, :]
        o_ref[r0:r0 + CONV_ROWS, :] = (acc * _sigmoid(acc)).astype(BF16)


def _conv_xbc(proj, conv_w, conv_b, layer, st, flags):
    t, tq = st.total, st.tc
    ncb = (COL_GATE - COL_X) // CB
    cb0 = COL_X // CB
    rpt = tq // BF16_ROWS
    last_rb = t // BF16_ROWS - 1
    grid_spec = pltpu.PrefetchScalarGridSpec(
        num_scalar_prefetch=2,
        grid=(t // tq, ncb),
        in_specs=[
            pl.BlockSpec((tq, CB), lambda i, j, a, b: (i, cb0 + j)),
            pl.BlockSpec((BF16_ROWS, CB), lambda i, j, a, b: (jnp.maximum(i * rpt - 1, 0), cb0 + j)),
            pl.BlockSpec((BF16_ROWS, CB), lambda i, j, a, b: (jnp.minimum((i + 1) * rpt, last_rb), cb0 + j)),
            pl.BlockSpec((None, SSD_CONV, CB), lambda i, j, a, b: (layer, 0, j)),
            pl.BlockSpec((None, 1, CB), lambda i, j, a, b: (layer, 0, j)),
        ],
        out_specs=pl.BlockSpec((tq, CB), lambda i, j, a, b: (i, j)),
        scratch_shapes=[pltpu.VMEM((tq + 2 * BF16_ROWS, CB), BF16)],
    )
    return pl.pallas_call(
        _conv_body,
        grid_spec=grid_spec,
        out_shape=jax.ShapeDtypeStruct((t, COL_GATE - COL_X), BF16),
        compiler_params=_cparams(("arbitrary", "arbitrary")),
        name="conv_xbc",
    )(flags[0], flags[1], proj, proj, proj, conv_w, conv_b)


def _ssd_scalars(dt_ref, rows, dtb_ref, alog_ref):
    v = dt_ref[rows, :] + dtb_ref[...]
    u = jnp.exp(-jnp.abs(v))
    w = 1.0 + u
    dt = jnp.maximum(v, 0.0) + jnp.where(w == 1.0, u, jnp.log(w) * (u / (w - 1.0)))
    da = dt * (-jnp.exp(alog_ref[...]))
    r = lax.broadcasted_iota(I32, (CHUNK, CHUNK), 0)
    c = lax.broadcasted_iota(I32, (CHUNK, CHUNK), 1)
    tri = jnp.where(c <= r, 1.0, 0.0).astype(BF16)
    upper = jnp.where(r <= c, 1.0, 0.0).astype(BF16)
    cs = _dot3_r(tri, da)
    da_t = da.T
    dt_t = dt.T
    cs_t = _dot3_l(da_t, upper)
    return dt, cs, cs - da, dt_t, cs_t, cs_t - da_t, r, c


def _expand(v, e_ref):
    hi = v.astype(BF16)
    mid = (v - hi.astype(F32)).astype(BF16)
    e = e_ref[...]
    return _dot(hi, e) + _dot(mid, e)


def _expand_pair(a, b, e_ref):
    out = _expand(jnp.concatenate([a, b], axis=0), e_ref)
    return out[:CHUNK, :], out[CHUNK:, :]


def _state_step(s_scr, xs_f32, w_tok, dec_lanes, b_ref, rows):
    xw = (xs_f32 * w_tok).astype(BF16)
    for g in range(GROUPS):
        sl = slice(g * GROUP_W, (g + 1) * GROUP_W)
        sc = _dot_tn(b_ref[rows, g * STATE:(g + 1) * STATE], xw[:, sl])
        s_scr[:, sl] = s_scr[:, sl] * dec_lanes[:, sl] + sc


def _ssd_fwd_body(first, xs_ref, b_ref, c_ref, dt_ref, dtb_ref, alog_ref, ef_ref, y_ref, s_scr):
    cps = xs_ref.shape[0] // CHUNK

    @pl.when(pl.program_id(0) == 0)
    def _():
        s_scr[...] = jnp.zeros_like(s_scr)

    for sub in range(cps):
        rows = slice(sub * CHUNK, (sub + 1) * CHUNK)
        keep = jnp.where(first[pl.program_id(0) * cps + sub] == 1, 0.0, 1.0)
        dt, cs, ex, dt_t, cs_t, ex_t, r, c = _ssd_scalars(dt_ref, rows, dtb_ref, alog_ref)
        w_out, w_st = _expand_pair(jnp.exp(cs) * keep, jnp.exp(cs[CHUNK - 1:CHUNK, :] - cs) * dt, ef_ref)
        for g in range(GROUPS):
            sl = slice(g * GROUP_W, (g + 1) * GROUP_W)
            yo = _dot(c_ref[rows, g * STATE:(g + 1) * STATE], s_scr[:, sl].astype(BF16))
            y_ref[rows, sl] = yo * w_out[:, sl]
        lane = lax.broadcasted_iota(I32, (CHUNK, 2 * HEAD_DIM), 1)
        lower = r >= c
        pairs_per_group = HEADS // GROUPS // 2
        for pi in range(HEADS // 2):
            g = pi // pairs_per_group
            if pi % pairs_per_group == 0:
                cb = _dot_nt(c_ref[rows, g * STATE:(g + 1) * STATE], b_ref[rows, g * STATE:(g + 1) * STATE])
            ms = []
            for h in (2 * pi, 2 * pi + 1):
                hb = HEADS + h
                arg = jnp.where(lower, cs[:, h:h + 1] - cs_t[h:h + 1, :], ex_t[hb:hb + 1, :] - ex[:, hb:hb + 1])
                d0 = dt_t[h:h + 1, :]
                d1 = dt_t[hb:hb + 1, :]
                w = jnp.where(r > c, d0, jnp.where(r < c, d1, d0 + d1))
                ms.append((jnp.exp(arg) * w * cb).astype(BF16))
            psl = slice(pi * 2 * HEAD_DIM, (pi + 1) * 2 * HEAD_DIM)
            xp = xs_ref[rows, psl]
            zero = jnp.zeros_like(xp)
            rhs = jnp.concatenate([jnp.where(lane < HEAD_DIM, xp, zero), jnp.where(lane >= HEAD_DIM, xp, zero)],
                                  axis=0)
            y_ref[rows, psl] = y_ref[rows, psl] + _dot(jnp.concatenate(ms, axis=1), rhs)
        _state_step(s_scr, xs_ref[rows, :].astype(F32), w_st, w_out[CHUNK - 1:CHUNK, :], b_ref, rows)


def _ssd_bwd_rows(last, step, xs_ref, b_ref, c_ref, dt_ref, y1_ref, z_ref, dtb_ref, alog_ref, eb_ref,
                  dskip_ref, nw_ref, s_scr, yn_scr, between=()):
    cps = xs_ref.shape[0] // CHUNK
    between = list(between)
    for sub in reversed(range(cps)):
        if between:
            between.pop(0)()
        rows = slice(sub * CHUNK, (sub + 1) * CHUNK)
        keep = jnp.where(last[step * cps + sub] == 1, 0.0, 1.0)
        dt, cs, ex, dt_t, cs_t, ex_t, r, c = _ssd_scalars(dt_ref, rows, dtb_ref, alog_ref)
        tot = cs[CHUNK - 1:CHUNK, :]
        w_out, w_st = _expand_pair(jnp.exp(tot - ex) * keep, jnp.exp(ex) * dt, eb_ref)
        xs = xs_ref[rows, :].astype(F32)
        z = z_ref[rows, :].astype(F32)
        zs = z * _sigmoid(z)
        for g in range(GROUPS):
            sl = slice(g * GROUP_W, (g + 1) * GROUP_W)
            yo = _dot(c_ref[rows, g * STATE:(g + 1) * STATE], s_scr[:, sl].astype(BF16))
            y = (y1_ref[rows, sl] + yo * w_out[:, sl] + xs[:, sl] * dskip_ref[:, sl]) * zs[:, sl]
            y = y * lax.rsqrt(jnp.mean(y * y, axis=-1, keepdims=True) + EPS)
            yn_scr[rows, sl] = (y * nw_ref[:, sl]).astype(BF16)
        _state_step(s_scr, xs, w_st, w_out[0:1, :], b_ref, rows)
    for rest in between:
        rest()


def _expansion_matrices():
    rows = np.arange(LANES)[:, None]
    cols = np.arange(INNER)[None, :] // HEAD_DIM
    ef = (rows == cols).astype(np.float32)
    eb = (rows == cols + HEADS).astype(np.float32)
    return jnp.asarray(ef, BF16), jnp.asarray(eb, BF16)


def _ssd_fwd(xbc, dt_raw, dtb, alog, ef, layer, st, first):
    t = st.total
    nb = GROUPS * STATE // CB
    rows = st.cps * CHUNK
    grid_spec = pltpu.PrefetchScalarGridSpec(
        num_scalar_prefetch=1,
        grid=(t // rows,),
        in_specs=[
            pl.BlockSpec((rows, INNER), lambda i, f: (i, 0)),
            pl.BlockSpec((rows, CB), lambda i, f: (i, INNER // CB)),
            pl.BlockSpec((rows, CB), lambda i, f: (i, INNER // CB + nb)),
            pl.BlockSpec((rows, LANES), lambda i, f: (i, 0)),
            pl.BlockSpec((None, 1, LANES), lambda i, f: (layer, 0, 0)),
            pl.BlockSpec((None, 1, LANES), lambda i, f: (layer, 0, 0)),
            pl.BlockSpec((LANES, INNER), lambda i, f: (0, 0)),
        ],
        out_specs=pl.BlockSpec((rows, INNER), lambda i, f: (i, 0)),
        scratch_shapes=[pltpu.VMEM((STATE, INNER), F32)],
    )
    return pl.pallas_call(
        _ssd_fwd_body,
        grid_spec=grid_spec,
        out_shape=jax.ShapeDtypeStruct((t, INNER), F32),
        compiler_params=_cparams(("arbitrary",)),
        name="ssd_fwd",
    )(first, xbc, xbc, xbc, dt_raw, dtb, alog, ef)


def _merge_stages(i, topz, botz, read_x, o_ssd, gate_ref, scb_ref, scc_ref, scx_ref, ccp_ref, cxp_ref, ccn_ref, cxn_ref,
                  cw_ref, wsc_ref, gb_ref, wo_ref, nw_ref, rw_ref, x1_ref, h2_ref, pt_ref, ext):
    tm = gate_ref.shape[0]
    halo = SUBLANES
    pad = SC_CONV // 2
    live = {}

    def short_conv():
        ext[0:halo, :] = jnp.where(topz[i] == 1, 0.0, ccp_ref[...].astype(F32) * cxp_ref[...].astype(F32))
        ext[halo:halo + tm, :] = scc_ref[...].astype(F32) * scx_ref[...].astype(F32)
        ext[halo + tm:2 * halo + tm, :] = jnp.where(botz[i] == 1, 0.0,
                                                    ccn_ref[...].astype(F32) * cxn_ref[...].astype(F32))
        v = ext[halo - pad:halo - pad + tm, :] * cw_ref[0:1, :]
        for k in range(1, SC_CONV):
            v = v + ext[halo - pad + k:halo - pad + k + tm, :] * cw_ref[k:k + 1, :]
        live["o_sc"] = _dot((scb_ref[...].astype(F32) * v).astype(BF16), wsc_ref[...])

    def gated_mix():
        g = _sigmoid(gate_ref[...].astype(F32) + gb_ref[...])
        mix = g[:, :D_MODEL] * o_ssd.astype(F32) + g[:, D_MODEL:] * live["o_sc"]
        x1 = read_x() + _dot(mix.astype(BF16), wo_ref[...])
        x1_ref[...] = x1
        live["x1"] = x1

    def norm():
        x1 = live["x1"]
        h2 = x1 * lax.rsqrt(jnp.mean(x1 * x1, axis=-1, keepdims=True) + EPS) * nw_ref[...]
        h2_ref[:, :D_MODEL] = h2.astype(BF16)
        live["h2"] = h2

    def router():
        hh, hm, _ = _split3(live["h2"])
        rh, rm, _ = _split3(rw_ref[...])
        logits = _dot_nt(rh, hh) + _dot_nt(rh, hm) + _dot_nt(rm, hh)
        mx = jnp.max(logits, axis=0, keepdims=True)
        ex = jnp.exp(logits - mx)
        probs = ex / jnp.sum(ex, axis=0, keepdims=True)
        pt_ref[...] = probs
        pk = jnp.concatenate([probs, jnp.zeros((LANES - N_EXPERTS, tm), F32)], axis=0).T
        hi = pk.astype(BF16).astype(F32)
        mid = (pk - hi).astype(BF16).astype(F32)
        lo = pk - hi - mid
        lane = lax.broadcasted_iota(I32, (tm, LANES), 1)
        terms = jnp.where(lane < GATE_STRIDE, hi,
                          jnp.where(lane < 2 * GATE_STRIDE, pltpu.roll(mid, GATE_STRIDE, 1),
                                    jnp.where(lane < 3 * GATE_STRIDE, pltpu.roll(lo, 2 * GATE_STRIDE, 1), 0.0)))
        h2_ref[:, D_MODEL:] = terms.astype(BF16)

    return [short_conv, gated_mix, norm, router]


def _bwd_merge_body(last, topz, botz, *refs, x_starts):
    (xs_ref, b_ref, c_ref, dt_ref, y1_ref, z_ref, dtb_ref, alog_ref, eb_ref, dskip_ref, snw_ref, wout_ref) = refs[:12]
    x_refs = refs[12:12 + len(x_starts)]
    (gate_ref, scb_ref, scc_ref, scx_ref, ccp_ref, cxp_ref, ccn_ref, cxn_ref, cw_ref, wsc_ref, gb_ref, wo_ref, nw_ref,
     rw_ref, x1_ref, h2_ref, pt_ref, s_scr, yn_scr, ossd_scr, ext) = refs[12 + len(x_starts):]
    i = pl.program_id(0)
    n_blocks = pl.num_programs(0) - 1

    @pl.when(i == 0)
    def _():
        s_scr[...] = jnp.zeros_like(s_scr)
        ossd_scr[...] = jnp.zeros_like(ossd_scr)

    slot = i % 2
    merge_tile = n_blocks - 1 - jnp.maximum(i - 1, 0)
    read_x = functools.partial(_read_parts, merge_tile, x_refs, x_starts)
    stages = _merge_stages(merge_tile, topz, botz, read_x, ossd_scr[1 - slot], gate_ref, scb_ref, scc_ref, scx_ref,
                           ccp_ref, cxp_ref, ccn_ref, cxn_ref, cw_ref, wsc_ref, gb_ref, wo_ref, nw_ref, rw_ref,
                           x1_ref, h2_ref, pt_ref, ext)
    ssd_tile = n_blocks - 1 - jnp.minimum(i, n_blocks - 1)
    _ssd_bwd_rows(last, ssd_tile, xs_ref, b_ref, c_ref, dt_ref, y1_ref, z_ref, dtb_ref, alog_ref, eb_ref,
                  dskip_ref, snw_ref, s_scr, yn_scr, between=stages)
    ossd_scr[slot] = _dot(yn_scr[...], wout_ref[...]).astype(BF16)


def _ssd_bwd_merge(x_parts, xbc, dt_raw, y1, proj, dtb, alog, eb, dskip, ssd_norm, w_ssd_out, sc_w, w_sc, gate_b, w_o, norm2,
                   router_t, layer, st, last, flags):
    t, tm = st.total, st.tq
    assert tm % CHUNK == 0
    nc = t // tm
    nb = GROUPS * STATE // CB
    rpt = tm // SUBLANES
    last_rb = t // SUBLANES - 1
    cscb, cscc, cscx = COL_SCB // CB, COL_SCB // CB + 1, COL_SCB // CB + 2
    sb = lambda i: nc - 1 - jnp.minimum(i, nc - 1)
    mb = lambda i: nc - 1 - jnp.maximum(i - 1, 0)
    prev = lambda i: jnp.maximum(mb(i) * rpt - 1, 0)
    nxt = lambda i: jnp.minimum((mb(i) + 1) * rpt, last_rb)
    const = lambda *idx: (lambda i, *_: idx)
    x_specs, x_starts = _part_specs(x_parts, tm, lambda i, *_: mb(i))
    grid_spec = pltpu.PrefetchScalarGridSpec(
        num_scalar_prefetch=3,
        grid=(nc + 1,),
        in_specs=[
            pl.BlockSpec((tm, INNER), lambda i, *_: (sb(i), 0)),
            pl.BlockSpec((tm, CB), lambda i, *_: (sb(i), INNER // CB)),
            pl.BlockSpec((tm, CB), lambda i, *_: (sb(i), INNER // CB + nb)),
            pl.BlockSpec((tm, LANES), lambda i, *_: (sb(i), 0)),
            pl.BlockSpec((tm, INNER), lambda i, *_: (sb(i), 0)),
            pl.BlockSpec((tm, INNER), lambda i, *_: (sb(i), 0)),
            pl.BlockSpec((None, 1, LANES), const(layer, 0, 0)),
            pl.BlockSpec((None, 1, LANES), const(layer, 0, 0)),
            pl.BlockSpec((LANES, INNER), const(0, 0)),
            pl.BlockSpec((None, 1, INNER), const(layer, 0, 0)),
            pl.BlockSpec((None, 1, INNER), const(layer, 0, 0)),
            pl.BlockSpec((None, INNER, D_MODEL), const(layer, 0, 0)),
        ] + x_specs + [
            pl.BlockSpec((tm, 2 * D_MODEL), lambda i, *_: (mb(i), COL_GATE // (2 * D_MODEL))),
            pl.BlockSpec((tm, CB), lambda i, *_: (mb(i), cscb)),
            pl.BlockSpec((tm, CB), lambda i, *_: (mb(i), cscc)),
            pl.BlockSpec((tm, CB), lambda i, *_: (mb(i), cscx)),
            pl.BlockSpec((SUBLANES, CB), lambda i, *_: (prev(i), cscc)),
            pl.BlockSpec((SUBLANES, CB), lambda i, *_: (prev(i), cscx)),
            pl.BlockSpec((SUBLANES, CB), lambda i, *_: (nxt(i), cscc)),
            pl.BlockSpec((SUBLANES, CB), lambda i, *_: (nxt(i), cscx)),
            pl.BlockSpec((None, SC_CONV, SC_WIDTH), const(layer, 0, 0)),
            pl.BlockSpec((None, SC_WIDTH, D_MODEL), const(layer, 0, 0)),
            pl.BlockSpec((None, 1, 2 * D_MODEL), const(layer, 0, 0)),
            pl.BlockSpec((None, D_MODEL, D_MODEL), const(layer, 0, 0)),
            pl.BlockSpec((None, 1, D_MODEL), const(layer, 0, 0)),
            pl.BlockSpec((None, N_EXPERTS, D_MODEL), const(layer, 0, 0)),
        ],
        out_specs=[
            pl.BlockSpec((tm, D_MODEL), lambda i, *_: (mb(i), 0)),
            pl.BlockSpec((tm, XW), lambda i, *_: (mb(i), 0)),
            pl.BlockSpec((N_EXPERTS, tm), lambda i, *_: (0, mb(i))),
        ],
        scratch_shapes=[
            pltpu.VMEM((STATE, INNER), F32),
            pltpu.VMEM((tm, INNER), BF16),
            pltpu.VMEM((2, tm, D_MODEL), BF16),
            pltpu.VMEM((tm + 2 * SUBLANES, SC_WIDTH), F32),
        ],
    )
    return pl.pallas_call(
        functools.partial(_bwd_merge_body, x_starts=x_starts),
        grid_spec=grid_spec,
        out_shape=[jax.ShapeDtypeStruct((t, D_MODEL), F32), jax.ShapeDtypeStruct((t, XW), BF16),
                   jax.ShapeDtypeStruct((N_EXPERTS, t), F32)],
        compiler_params=pltpu.CompilerParams(dimension_semantics=("arbitrary",), vmem_limit_bytes=FUSED_VMEM_LIMIT),
        name="ssd_bwd_merge",
    )(last, flags[0], flags[1], xbc, xbc, xbc, dt_raw, y1, proj, dtb, alog, eb, dskip, ssd_norm, w_ssd_out,
      *x_parts, proj, proj, proj, proj, proj, proj, proj, proj, sc_w, w_sc, gate_b, w_o, norm2, router_t)


def _route_body(p_ref, dest_ref, posx_ref, *, cap):
    rows = p_ref.shape[1]
    r = lax.broadcasted_iota(I32, (LANES, LANES), 0)
    c = lax.broadcasted_iota(I32, (LANES, LANES), 1)
    upper = jnp.where(r <= c, 1.0, 0.0).astype(BF16)
    last_col = jnp.where(r == LANES - 1, 1.0, 0.0).astype(BF16)
    rr = lax.broadcasted_iota(I32, (rows, rows), 0)
    rc = lax.broadcasted_iota(I32, (rows, rows), 1)
    below = jnp.where(rc < rr, 1.0, 0.0).astype(BF16)

    def count(mask):
        s = jnp.sum(jnp.where(mask, 1.0, 0.0), axis=0, keepdims=True)
        return jnp.sum(s, axis=1, keepdims=True)

    def excl_prefix(mask):
        x = jnp.where(mask, 1.0, 0.0)
        incl = _dot(x.astype(BF16), upper)
        row_tot = _dot(incl.astype(BF16), last_col)
        row_off = _dot(below, row_tot.astype(BF16))
        return incl + row_off - x

    def search(i, prefix):
        bit = jnp.left_shift(jnp.int32(1), 30 - i)
        out = []
        for e in range(N_EXPERTS):
            keys = pltpu.bitcast(p_ref[e], I32)
            cand = prefix[e] | bit
            out.append(jnp.where(count(keys >= cand) >= cap, cand, prefix[e]))
        return tuple(out)

    thr = lax.fori_loop(0, 31, search, tuple(jnp.zeros((1, 1), I32) for _ in range(N_EXPERTS)))
    for e in range(N_EXPERTS):
        keys = pltpu.bitcast(p_ref[e], I32)
        gt = keys > thr[e]
        eq = keys == thr[e]
        need = cap - count(gt)
        sel = gt | (eq & (excl_prefix(eq) < need))
        pos = excl_prefix(sel).astype(I32)
        posx_ref[e] = pos
        dest_ref[e] = jnp.where(sel, pos, -1)


def _route(probs3, cap):
    e, rows, _ = probs3.shape
    return pl.pallas_call(
        functools.partial(_route_body, cap=cap),
        out_shape=[jax.ShapeDtypeStruct((e, rows, LANES), I32), jax.ShapeDtypeStruct((e, rows, LANES), I32)],
        compiler_params=pltpu.CompilerParams(vmem_limit_bytes=VMEM_LIMIT),
        name="route",
    )(probs3)


def _block_counts(posx_list, st):
    rb = st.tk // LANES
    c0s, c1s = [], []
    for g, posx in enumerate(posx_list):
        c0 = posx[:, ::rb, 0]
        c1 = jnp.concatenate([c0[:, 1:], jnp.full((N_EXPERTS, 1), st.cap_g[g], I32)], axis=1)
        c0s.append(c0.T.reshape(-1))
        c1s.append(c1.T.reshape(-1))
    return jnp.concatenate(c0s), jnp.concatenate(c1s)


def _pass_window(c0_s, c1_s, rs_s, kb, e, k):
    c0 = c0_s[kb * N_EXPERTS + e]
    c1 = c1_s[kb * N_EXPERTS + e]
    lo = jnp.minimum(c0 + k * PASS, c1)
    base = (lo // BF16_ROWS) * BF16_ROWS
    row = pl.multiple_of(rs_s[kb * N_EXPERTS + e] + base, BF16_ROWS)
    return lo, base, row, jnp.minimum(lo + PASS, c1)


def _num_passes(c0_s, c1_s, kb):
    most = c1_s[kb * N_EXPERTS] - c0_s[kb * N_EXPERTS]
    for e in range(1, N_EXPERTS):
        most = jnp.maximum(most, c1_s[kb * N_EXPERTS + e] - c0_s[kb * N_EXPERTS + e])
    return (most + PASS - 1) // PASS


def _window_one_hot(c0_s, c1_s, rs_s, dest_ref, kb, k):
    tk = dest_ref.shape[1]
    slot = lax.broadcasted_iota(I32, (WIN, tk), 0)
    parts = []
    for e in range(N_EXPERTS):
        lo, base, _, _ = _pass_window(c0_s, c1_s, rs_s, kb, e, k)
        d = dest_ref[e:e + 1, :]
        rel = jnp.where(d >= lo, jnp.where(d < lo + PASS, d - base, -1), -1)
        parts.append(jnp.where(rel == slot, 1.0, 0.0).astype(BF16))
    return jnp.concatenate(parts, axis=0)


def _dispatch_body(c0_s, c1_s, rs_s, gfirst_s, cap_s, dest_ref, h_ref, xe_hbm, stage, carry, zeros, sem, cnt):
    kb = pl.program_id(0)

    def window_copy(slot, e, row):
        return pltpu.make_async_copy(stage.at[slot, pl.ds(e * WIN, WIN), :], xe_hbm.at[pl.ds(row, WIN), :], sem.at[0])

    def wait_all():
        for e in range(N_EXPERTS):
            window_copy(0, e, 0).wait()

    def pad_copy(e):
        row = pl.multiple_of(rs_s[kb * N_EXPERTS + e] + cap_s[kb], BF16_ROWS)
        return pltpu.make_async_copy(zeros, xe_hbm.at[pl.ds(row, zeros.shape[0]), :], sem.at[1])

    @pl.when(kb == 0)
    def _():
        cnt[0] = 0

    @pl.when(gfirst_s[kb] == 1)
    def _():
        carry[...] = jnp.zeros_like(carry)
        zeros[...] = jnp.zeros_like(zeros)
        for e in range(N_EXPERTS):
            pad_copy(e).start()
        for e in range(N_EXPERTS):
            pad_copy(e).wait()

    def one_pass(k, _):
        slot = cnt[0] % 2
        packed = _dot(_window_one_hot(c0_s, c1_s, rs_s, dest_ref, kb, k), h_ref[...])
        rows = []
        for e in range(N_EXPERTS):
            lo, base, row, nxt = _pass_window(c0_s, c1_s, rs_s, kb, e, k)
            head = packed[e * WIN:e * WIN + BF16_ROWS, :] + carry[e].astype(F32)
            stage[slot, e * WIN:e * WIN + BF16_ROWS, :] = head.astype(BF16)
            stage[slot, e * WIN + BF16_ROWS:(e + 1) * WIN, :] = packed[e * WIN + BF16_ROWS:(e + 1) * WIN, :].astype(BF16)
            off = pl.multiple_of(e * WIN + (nxt // BF16_ROWS) * BF16_ROWS - base, BF16_ROWS)
            carry[e] = stage[slot, pl.ds(off, BF16_ROWS), :]
            rows.append(row)

        @pl.when(cnt[0] > 0)
        def _():
            wait_all()

        for e in range(N_EXPERTS):
            window_copy(slot, e, rows[e]).start()
        cnt[0] = cnt[0] + 1
        return 0

    lax.fori_loop(0, _num_passes(c0_s, c1_s, kb), one_pass, 0)

    @pl.when((kb == pl.num_programs(0) - 1) & (cnt[0] > 0))
    def _():
        wait_all()


def _dispatch(c0, c1, rs, gfirst, cap, dest, h2a, st):
    tk = st.tk
    grid_spec = pltpu.PrefetchScalarGridSpec(
        num_scalar_prefetch=5,
        grid=(st.total // tk,),
        in_specs=[
            pl.BlockSpec((N_EXPERTS, tk), lambda i, *_: (0, i)),
            pl.BlockSpec((tk, XW), lambda i, *_: (i, 0)),
        ],
        out_specs=pl.BlockSpec(memory_space=pl.ANY),
        scratch_shapes=[
            pltpu.VMEM((2, N_EXPERTS * WIN, XW), BF16),
            pltpu.VMEM((N_EXPERTS, BF16_ROWS, XW), BF16),
            pltpu.VMEM((st.ts, XW), BF16),
            pltpu.SemaphoreType.DMA((2,)),
            pltpu.SMEM((1,), I32),
        ],
    )
    return pl.pallas_call(
        _dispatch_body,
        grid_spec=grid_spec,
        out_shape=jax.ShapeDtypeStruct((st.slot_rows, XW), BF16),
        compiler_params=_cparams(("arbitrary",)),
        name="moe_dispatch",
    )(c0, c1, rs, gfirst, cap, dest, h2a)


def _moe_ffn_body(pad_s, src_s, xe_ref, wg_ref, wu_ref, wd_ref, o_ref, *, tiles_per_expert):
    i = pl.program_id(0)

    @pl.when(pad_s[i] == 1)
    def _():
        o_ref[...] = jnp.zeros_like(o_ref)

    @pl.when(pad_s[i] == 0)
    def _():
        e = i // tiles_per_expert
        xe = xe_ref[:, :D_MODEL]
        terms = xe_ref[:, D_MODEL:].astype(F32)
        lane = lax.broadcasted_iota(I32, terms.shape, 1)
        mine = (lane < 3 * GATE_STRIDE) & (lane % GATE_STRIDE == e)
        gate = jnp.sum(jnp.where(mine, terms, 0.0), axis=1, keepdims=True)
        a = _dot(xe, wg_ref[...])
        u = _dot(xe, wu_ref[...])
        act = (a * _sigmoid(a) * u).astype(BF16)
        o_ref[...] = (_dot(act, wd_ref[...]) * gate).astype(BF16)


def _moe_ffn(xe, wg, wu, wd, layer, st):
    ts = st.ts
    tpe = st.tiles_per_expert
    pad, src = st.tile_tables()
    grid_spec = pltpu.PrefetchScalarGridSpec(
        num_scalar_prefetch=2,
        grid=(N_EXPERTS * tpe,),
        in_specs=[
            pl.BlockSpec((ts, XW), lambda i, p, s: (s[i], 0)),
            pl.BlockSpec((None, None, D_MODEL, EXPERT_FF), lambda i, p, s: (layer, i // tpe, 0, 0)),
            pl.BlockSpec((None, None, D_MODEL, EXPERT_FF), lambda i, p, s: (layer, i // tpe, 0, 0)),
            pl.BlockSpec((None, None, EXPERT_FF, D_MODEL), lambda i, p, s: (layer, i // tpe, 0, 0)),
        ],
        out_specs=pl.BlockSpec((ts, D_MODEL), lambda i, p, s: (i, 0)),
    )
    return pl.pallas_call(
        functools.partial(_moe_ffn_body, tiles_per_expert=tpe),
        grid_spec=grid_spec,
        out_shape=jax.ShapeDtypeStruct((st.slot_rows, D_MODEL), BF16),
        compiler_params=_cparams(("arbitrary",)),
        name="moe_ffn",
    )(pad, src, xe, wg, wu, wd)


OVERFLOW_SLOT = 2


def _combine_body(c0_s, c1_s, rs_s, dest_ref, x_ref, ye_hbm, o_ref, buf, sem):
    kb = pl.program_id(0)
    nkb = pl.num_programs(0)

    def window_copy(slot, e, row):
        return pltpu.make_async_copy(ye_hbm.at[pl.ds(row, WIN), :], buf.at[slot, pl.ds(e * WIN, WIN), :], sem.at[slot])

    def fetch(block, k, slot):
        for e in range(N_EXPERTS):
            window_copy(slot, e, _pass_window(c0_s, c1_s, rs_s, block, e, k)[2]).start()

    def wait_all(slot):
        for e in range(N_EXPERTS):
            window_copy(slot, e, 0).wait()

    @pl.when(kb == 0)
    def _():
        fetch(0, 0, 0)

    slot = kb % 2
    wait_all(slot)

    @pl.when(kb + 1 < nkb)
    def _():
        fetch(kb + 1, 0, 1 - slot)

    o_ref[...] = x_ref[...] + _dot_tn(_window_one_hot(c0_s, c1_s, rs_s, dest_ref, kb, 0), buf[slot])

    def extra_pass(k, _):
        fetch(kb, k, OVERFLOW_SLOT)
        wait_all(OVERFLOW_SLOT)
        o_ref[...] += _dot_tn(_window_one_hot(c0_s, c1_s, rs_s, dest_ref, kb, k), buf[OVERFLOW_SLOT])
        return 0

    lax.fori_loop(1, _num_passes(c0_s, c1_s, kb), extra_pass, 0)


def _combine(c0, c1, rs, dest, x1, ye, st):
    tk = st.tk
    grid_spec = pltpu.PrefetchScalarGridSpec(
        num_scalar_prefetch=3,
        grid=(st.total // tk,),
        in_specs=[
            pl.BlockSpec((N_EXPERTS, tk), lambda i, *_: (0, i)),
            pl.BlockSpec((tk, D_MODEL), lambda i, *_: (i, 0)),
            pl.BlockSpec(memory_space=pl.ANY),
        ],
        out_specs=pl.BlockSpec((tk, D_MODEL), lambda i, *_: (i, 0)),
        scratch_shapes=[
            pltpu.VMEM((3, N_EXPERTS * WIN, D_MODEL), BF16),
            pltpu.SemaphoreType.DMA((3,)),
        ],
    )
    return pl.pallas_call(
        _combine_body,
        grid_spec=grid_spec,
        out_shape=jax.ShapeDtypeStruct((st.total, D_MODEL), F32),
        compiler_params=_cparams(("arbitrary",)),
        name="moe_combine",
    )(c0, c1, rs, dest, x1, ye)


def _final_norm_body(x_ref, w_ref, o_ref):
    x = x_ref[...]
    o_ref[...] = x * lax.rsqrt(jnp.mean(x * x, axis=-1, keepdims=True) + EPS) * w_ref[...]


def _final_norm(x, w, st, group):
    t, tm = st.t_g[group], st.tm
    first = st.off_g[group] // tm
    return pl.pallas_call(
        _final_norm_body,
        grid=(t // tm,),
        in_specs=[pl.BlockSpec((tm, D_MODEL), lambda i: (first + i, 0)), pl.BlockSpec((1, D_MODEL), lambda i: (0, 0))],
        out_specs=pl.BlockSpec((tm, D_MODEL), lambda i: (i, 0)),
        out_shape=jax.ShapeDtypeStruct((t, D_MODEL), F32),
        compiler_params=_cparams(("arbitrary",)),
        name="final_norm",
    )(x, w)


def _pad_lanes(v, width=LANES):
    return jnp.pad(v, [(0, 0)] * (v.ndim - 1) + [(0, width - v.shape[-1])])


def _trunk_stream(xs, params, tm, tq, tk, ts, tc, cps):
    (norm1_w, w_in, ssd_conv_w, ssd_conv_b, ssd_dt_bias, ssd_a_log, ssd_d, ssd_norm_w, w_ssd_out, sc_conv_w,
     w_sc_out, gate_b, w_o, norm2_w, router_w, w_gate, w_up, w_down, final_norm_w) = params
    depth = w_in.shape[0]
    st = _Stream([(x.shape[0], x.shape[1]) for x in xs], tm, tq, tk, ts, tc, cps)
    x_parts = [v.reshape(-1, D_MODEL) for v in xs]

    w_main = jnp.concatenate([w_in[:, :, :_OFF_DT], w_in[:, :, _OFF_GATE:], w_in[:, :, _OFF_SCB:_OFF_GATE]],
                             axis=2).astype(BF16)
    w_dt = _pad_lanes(w_in[:, :, _OFF_DT:_OFF_SCB]).astype(BF16)
    dtb = _pad_lanes(ssd_dt_bias.reshape(depth, 1, 2 * HEADS))
    alog = _pad_lanes(ssd_a_log.reshape(depth, 1, 2 * HEADS))
    dskip = jnp.repeat(ssd_d, HEAD_DIM, axis=-1).reshape(depth, 1, INNER)
    ef, eb = _expansion_matrices()
    router_t = jnp.swapaxes(router_w, 1, 2)
    wg, wu, wd = w_gate.astype(BF16), w_up.astype(BF16), w_down.astype(BF16)
    row = lambda v: v.reshape(depth, 1, v.shape[-1])

    chunk_first, chunk_last = st.seq_flags(CHUNK)
    tile_flags = st.seq_flags(tq)
    conv_flags = st.seq_flags(tc)
    region_start, group_first, group_cap = st.block_tables()

    for l in range(depth):
        proj, dt_raw = _inproj(x_parts, row(norm1_w), w_main, w_dt, l, st)
        xbc = _conv_xbc(proj, ssd_conv_w, row(ssd_conv_b), l, st, conv_flags)
        y1 = _ssd_fwd(xbc, dt_raw, dtb, alog, ef, l, st, chunk_first)
        x1, h2, probs_t = _ssd_bwd_merge(x_parts, xbc, dt_raw, y1, proj, dtb, alog, eb, dskip, row(ssd_norm_w),
                                         w_ssd_out.astype(BF16), sc_conv_w, w_sc_out.astype(BF16), row(gate_b),
                                         w_o.astype(BF16), row(norm2_w), router_t, l, st, chunk_last, tile_flags)
        dests, posxs = [], []
        for g in range(len(xs)):
            pg = probs_t[:, st.off_g[g]:st.off_g[g] + st.t_g[g]].reshape(N_EXPERTS, st.t_g[g] // LANES, LANES)
            dest, posx = _route(pg, st.cap_g[g])
            dests.append(dest.reshape(N_EXPERTS, st.t_g[g]))
            posxs.append(posx)
        dest = jnp.concatenate(dests, axis=1)
        c0, c1 = _block_counts(posxs, st)
        xe = _dispatch(c0, c1, region_start, group_first, group_cap, dest, h2, st)
        ye = _moe_ffn(xe, wg, wu, wd, l, st)
        x_parts = [_combine(c0, c1, region_start, dest, x1, ye, st)]

    fw = final_norm_w.reshape(1, D_MODEL)
    return [_final_norm(x_parts[0], fw, st, g).reshape(v.shape) for g, v in enumerate(xs)]


def kernel(x_prompt, x_sample, norm1_w, w_in, ssd_conv_w, ssd_conv_b, ssd_dt_bias, ssd_a_log, ssd_d, ssd_norm_w, w_ssd_out, sc_conv_w, w_sc_out, gate_b, w_o, norm2_w, router_w, w_gate, w_up, w_down, final_norm_w):
    params = (norm1_w, w_in, ssd_conv_w, ssd_conv_b, ssd_dt_bias, ssd_a_log, ssd_d, ssd_norm_w, w_ssd_out, sc_conv_w,
              w_sc_out, gate_b, w_o, norm2_w, router_w, w_gate, w_up, w_down, final_norm_w)
    y_prompt, y_sample = _trunk_stream([x_prompt, x_sample], params, tm=1024, tq=512, tk=256, ts=512, tc=4096, cps=8)
    return (y_prompt, y_sample)
```
